```python
import jax, jax.numpy as jnp
from jax import lax
import numpy as np

D_MODEL = 2048
BATCH = 1
SEQ = 16384
DEPTH = 1

D_MIX = D_MODEL
FOURIER_WIDTH = D_MIX // 2
N_FOURIER_GROUPS = 8
FOURIER_GROUP = FOURIER_WIDTH // N_FOURIER_GROUPS
RET_WIDTH = D_MIX - FOURIER_WIDTH
N_RET_HEADS = 4
RET_HEAD_DIM = RET_WIDTH // N_RET_HEADS
RET_CHUNK = 128
ROPE_BASE = 10000.0
D_IN = FOURIER_WIDTH + 4 * RET_WIDTH
N_EXPERTS = 32
TOP_K = 4
D_FF = D_MODEL
SWIGLU_LIMIT = 7.0
SWIGLU_ALPHA = 1.702
MOE_BLOCK = 128
LN_EPS = 1e-5
GN_EPS = 1e-6
DEEPNORM_ALPHA = (2.0 * DEPTH) ** 0.25
DEEPNORM_BETA = (8.0 * DEPTH) ** -0.25

kernel_name = "fnet_retnet_gptoss_deepnorm_hybrid"


def layer_norm(x, g, b, eps=LN_EPS):
    xf = x.astype(jnp.float32)
    mu = xf.mean(-1, keepdims=True)
    var = jnp.square(xf - mu).mean(-1, keepdims=True)
    return ((xf - mu) * lax.rsqrt(var + eps)).astype(x.dtype) * g + b


def fourier_mix(u):
    B, S, _ = u.shape
    ug = u.reshape(B, S, N_FOURIER_GROUPS, FOURIER_GROUP).astype(jnp.float32)
    y = jnp.fft.fftn(ug, axes=(1, 3), norm="ortho").real
    return y.reshape(B, S, FOURIER_WIDTH).astype(u.dtype)


def rotary(x, pos):
    half = x.shape[-1] // 2
    inv = ROPE_BASE ** (-jnp.arange(half, dtype=jnp.float32) / half)
    ang = pos.astype(jnp.float32)[:, None] * inv[None, :]
    cos = jnp.cos(ang)[None, :, None, :]
    sin = jnp.sin(ang)[None, :, None, :]
    x1 = x[..., :half].astype(jnp.float32)
    x2 = x[..., half:].astype(jnp.float32)
    return jnp.concatenate([x1 * cos - x2 * sin, x2 * cos + x1 * sin], -1).astype(x.dtype)


def chunk_retention(q, k, v, log_gamma, inclusive):
    B, H, S, Dk = q.shape
    Dv = v.shape[-1]
    C = RET_CHUNK
    N = S // C
    qc = q.reshape(B, H, N, C, Dk)
    kc = k.reshape(B, H, N, C, Dk)
    vc = v.reshape(B, H, N, C, Dv)
    idx = jnp.arange(C, dtype=jnp.float32)
    diff = idx[:, None] - idx[None, :]
    mask = (diff >= 0) if inclusive else (diff > 0)
    decay_intra = jnp.where(mask[None], jnp.exp(log_gamma[:, None, None] * jnp.maximum(diff, 0.0)[None]), 0.0)
    scores = jnp.einsum('bhnid,bhnjd->bhnij', qc, kc) * decay_intra[None, :, None].astype(q.dtype)
    intra = jnp.einsum('bhnij,bhnjd->bhnid', scores, vc)
    lg = log_gamma[:, None]
    k_to_end = jnp.exp(lg * (C - 1 - idx)[None])
    q_from_start = jnp.exp(lg * (idx + 1)[None])
    chunk_decay = jnp.exp(log_gamma * C)
    kv = jnp.einsum('bhncd,bhnce->bhnde', kc.astype(jnp.float32) * k_to_end[None, :, None, :, None],
                    vc.astype(jnp.float32))
    kv_t = jnp.moveaxis(kv, 2, 0)

    def step(state, kv_n):
        return state * chunk_decay[None, :, None, None] + kv_n, state

    _, states = lax.scan(step, jnp.zeros_like(kv_t[0]), kv_t)
    states = jnp.moveaxis(states, 0, 2)
    cross = jnp.einsum('bhncd,bhnde->bhnce', qc.astype(jnp.float32) * q_from_start[None, :, None, :, None], states)
    return (intra + cross.astype(intra.dtype)).reshape(B, H, S, Dv)


def retention_mix(uq, uk, uv, ug, log_g_fwd, log_g_bwd, gn_g, pos):
    B, S, _ = uq.shape
    heads = lambda t: t.reshape(B, S, N_RET_HEADS, RET_HEAD_DIM)
    q = rotary(heads(uq), pos)
    k = rotary(heads(uk), pos) * (RET_HEAD_DIM ** -0.5)
    v = heads(uv)
    q, k, v = (jnp.swapaxes(t, 1, 2) for t in (q, k, v))
    fwd = chunk_retention(q, k, v, log_g_fwd, True)
    flip = lambda t: jnp.flip(t, axis=2)
    bwd = flip(chunk_retention(flip(q), flip(k), flip(v), log_g_bwd, False))
    o = jnp.swapaxes(fwd + bwd, 1, 2)
    of = o.astype(jnp.float32)
    mu = of.mean(-1, keepdims=True)
    var = jnp.square(of - mu).mean(-1, keepdims=True)
    o = ((of - mu) * lax.rsqrt(var + GN_EPS)).astype(uq.dtype).reshape(B, S, RET_WIDTH) * gn_g
    return jax.nn.silu(ug) * o


def moe(h, layer, w_router, b_router, w_up, b_up, w_down, b_down):
    B, S, D = h.shape
    T = B * S
    xt = h.reshape(T, D)
    logits = (xt @ w_router[layer] + b_router[layer]).astype(jnp.float32)
    top_logit, top_idx = lax.top_k(logits, TOP_K)
    gates = jax.nn.softmax(top_logit, axis=-1)
    e_flat = top_idx.reshape(-1).astype(jnp.int32)
    tok_flat = jnp.repeat(jnp.arange(T, dtype=jnp.int32), TOP_K)
    g_flat = gates.reshape(-1)
    order = jnp.argsort(e_flat)
    e_sorted = e_flat[order]
    counts = jnp.bincount(e_flat, length=N_EXPERTS)
    start = jnp.cumsum(counts) - counts
    padded = (counts + MOE_BLOCK - 1) // MOE_BLOCK * MOE_BLOCK
    pad_end = jnp.cumsum(padded)
    pad_start = pad_end - padded
    dst = pad_start[e_sorted] + jnp.arange(T * TOP_K, dtype=jnp.int32) - start[e_sorted]
    R = T * TOP_K + N_EXPERTS * MOE_BLOCK
    NB = R // MOE_BLOCK
    row_tok = jnp.full((R,), T, jnp.int32).at[dst].set(tok_flat[order])
    row_gate = jnp.zeros((R,), jnp.float32).at[dst].set(g_flat[order])
    block_exp = jnp.minimum(jnp.searchsorted(pad_end, jnp.arange(NB) * MOE_BLOCK, side='right'),
                            N_EXPERTS - 1).astype(jnp.int32)
    x_pad = jnp.concatenate([xt, jnp.zeros((1, D), xt.dtype)], 0)

    def expert_block(args):
        toks, e = args
        xb = x_pad[toks]
        hc = xb @ w_up[layer, e] + b_up[layer, e]
        glu = jnp.minimum(hc[:, :D_FF], SWIGLU_LIMIT)
        lin = jnp.clip(hc[:, D_FF:], -SWIGLU_LIMIT, SWIGLU_LIMIT)
        act = glu * jax.nn.sigmoid(SWIGLU_ALPHA * glu) * (lin + 1.0)
        return act @ w_down[layer, e] + b_down[layer, e]

    y_rows = lax.map(expert_block, (row_tok.reshape(NB, MOE_BLOCK), block_exp))
    y_rows = y_rows.reshape(R, D) * row_gate[:, None].astype(xt.dtype)
    y = jax.ops.segment_sum(y_rows, row_tok, num_segments=T + 1)[:T]
    return y.reshape(B, S, D)


def setup_inputs(seed: int = 0) -> dict:
    key = jax.random.key(seed)
    ks = jax.random.split(key, 20)
    f32 = jnp.float32
    nrm = lambda k, shape, s: jax.random.normal(k, shape, f32) * s
    gain = lambda k, shape: 1.0 + 0.02 * jax.random.normal(k, shape, f32)
    x = jax.random.normal(ks[0], (BATCH, SEQ, D_MODEL), f32)
    ln0_g = gain(ks[1], (D_MODEL,))
    ln0_b = nrm(ks[2], (D_MODEL,), 0.02)
    w_in = nrm(ks[3], (DEPTH, D_MODEL, D_IN), D_MODEL ** -0.5)
    v_lo = FOURIER_WIDTH + 2 * RET_WIDTH
    col_scale = jnp.ones((D_IN,), f32).at[v_lo:v_lo + RET_WIDTH].set(DEEPNORM_BETA)
    w_in = w_in * col_scale
    gamma = 1.0 - 2.0 ** (-5.0 - np.arange(N_RET_HEADS, dtype=np.float32))
    base = jnp.asarray(np.log(-np.log(gamma)), f32)
    ret_decay_fwd = base[None] + nrm(ks[4], (DEPTH, N_RET_HEADS), 0.01)
    ret_decay_bwd = base[None] + nrm(ks[5], (DEPTH, N_RET_HEADS), 0.01)
    ret_gn_g = gain(ks[6], (DEPTH, RET_WIDTH))
    w_out = nrm(ks[7], (DEPTH, D_MIX, D_MODEL), DEEPNORM_BETA * D_MIX ** -0.5)
    ln1_g = gain(ks[8], (DEPTH, D_MODEL))
    ln1_b = nrm(ks[9], (DEPTH, D_MODEL), 0.02)
    w_router = nrm(ks[10], (DEPTH, D_MODEL, N_EXPERTS), D_MODEL ** -0.5)
    b_router = nrm(ks[11], (DEPTH, N_EXPERTS), 0.01)
    w_up = nrm(ks[12], (DEPTH, N_EXPERTS, D_MODEL, 2 * D_FF), DEEPNORM_BETA * D_MODEL ** -0.5)
    b_up = nrm(ks[13], (DEPTH, N_EXPERTS, 2 * D_FF), 0.01)
    w_down = nrm(ks[14], (DEPTH, N_EXPERTS, D_FF, D_MODEL), DEEPNORM_BETA * D_FF ** -0.5)
    b_down = nrm(ks[15], (DEPTH, N_EXPERTS, D_MODEL), 0.01)
    ln2_g = gain(ks[16], (DEPTH, D_MODEL))
    ln2_b = nrm(ks[17], (DEPTH, D_MODEL), 0.02)
    return {"x": x, "ln0_g": ln0_g, "ln0_b": ln0_b, "w_in": w_in,
            "ret_decay_fwd": ret_decay_fwd, "ret_decay_bwd": ret_decay_bwd, "ret_gn_g": ret_gn_g,
            "w_out": w_out, "ln1_g": ln1_g, "ln1_b": ln1_b,
            "w_router": w_router, "b_router": b_router, "w_up": w_up, "b_up": b_up,
            "w_down": w_down, "b_down": b_down, "ln2_g": ln2_g, "ln2_b": ln2_b}


def reference(x, ln0_g, ln0_b, w_in, ret_decay_fwd, ret_decay_bwd, ret_gn_g, w_out, ln1_g, ln1_b,
              w_router, b_router, w_up, b_up, w_down, b_down, ln2_g, ln2_b):
    B, S, D = x.shape
    pos = jnp.arange(S, dtype=jnp.int32)
    F, Rw = FOURIER_WIDTH, RET_WIDTH
    h = layer_norm(x, ln0_g, ln0_b)
    for l in range(DEPTH):
        u = h @ w_in[l]
        uf, uq, uk, uv, ug = jnp.split(u, [F, F + Rw, F + 2 * Rw, F + 3 * Rw], axis=-1)
        log_g_fwd = -jnp.exp(ret_decay_fwd[l].astype(jnp.float32))
        log_g_bwd = -jnp.exp(ret_decay_bwd[l].astype(jnp.float32))
        y_f = fourier_mix(uf)
        y_r = retention_mix(uq, uk, uv, ug, log_g_fwd, log_g_bwd, ret_gn_g[l], pos)
        mix = jnp.concatenate([y_f, y_r], axis=-1) @ w_out[l]
        h = layer_norm(DEEPNORM_ALPHA * h + mix, ln1_g[l], ln1_b[l])
        ffn = moe(h, l, w_router, b_router, w_up, b_up, w_down, b_down)
        h = layer_norm(DEEPNORM_ALPHA * h + ffn, ln2_g[l], ln2_b[l])
    return h
```

```python
import functools

import numpy as np
import jax
import jax.numpy as jnp
from jax import lax
from jax.experimental import pallas as pl
from jax.experimental.pallas import tpu as pltpu

F32 = jnp.float32
BF16 = jnp.bfloat16
I32 = jnp.int32
U32 = jnp.uint32

N_FOURIER_GROUPS = 8
N_RET_HEADS = 4
ROPE_BASE = 10000.0
TOP_K = 4
SWIGLU_LIMIT = 7.0
SWIGLU_ALPHA = 1.702
LN_EPS = 1e-5
GN_EPS = 1e-6
DEPTH = 1
DEEPNORM_ALPHA = (2.0 * DEPTH) ** 0.25

V7X_VMEM_BYTES = 64 * 2**20
VMEM_LIMIT_BYTES = V7X_VMEM_BYTES * 3 // 4
LANES = 128

IN_PROJ_ROWS = 512
FOURIER_COLS_PER_STEP = 4
RET_CHUNK = 256
OUT_PROJ_ROWS = 256
DISPATCH_ROWS = 256
EXPERT_ROWS = 512
EXPERT_FF_COLS = 256
COMBINE_ROWS = 128


def _params(*semantics):
    return pltpu.CompilerParams(dimension_semantics=semantics, vmem_limit_bytes=VMEM_LIMIT_BYTES)


def _layer_norm(x, g, b):
    mu = jnp.mean(x, axis=-1, keepdims=True)
    xc = x - mu
    var = jnp.mean(xc * xc, axis=-1, keepdims=True)
    return xc * lax.rsqrt(var + LN_EPS) * g + b


def _in_proj_kernel(x_ref, g_ref, b_ref, w_ref, cos_ref, sin_ref, h_ref, uf_ref, ur_ref, hb_ref, *, head_dim):
    j = pl.program_id(1)

    @pl.when(j == 0)
    def _():
        h = _layer_norm(x_ref[...], g_ref[...], b_ref[...])
        h_ref[...] = h
        hb_ref[...] = h.astype(BF16)

    acc = jnp.dot(hb_ref[...], w_ref[...], preferred_element_type=F32)

    @pl.when(j == 0)
    def _():
        uf_ref[...] = acc.astype(BF16)

    @pl.when((j == 1) | (j == 2))
    def _():
        cos = cos_ref[...]
        sin = sin_ref[...]
        half = head_dim // 2
        for lo in range(0, acc.shape[1], head_dim):
            x1 = acc[:, lo:lo + half]
            x2 = acc[:, lo + half:lo + head_dim]
            ur_ref[:, lo:lo + half] = (x1 * cos - x2 * sin).astype(BF16)
            ur_ref[:, lo + half:lo + head_dim] = (x2 * cos + x1 * sin).astype(BF16)

    @pl.when(j >= 3)
    def _():
        ur_ref[...] = acc.astype(BF16)


def _in_proj(x2, ln_g, ln_b, w_bf, cos, sin, head_dim):
    T, D = x2.shape
    tn = D // 2
    n_col = w_bf.shape[1] // tn
    assert n_col == 5 and w_bf.shape[1] == 5 * tn
    tm = min(IN_PROJ_ROWS, T)
    half = head_dim // 2
    return pl.pallas_call(
        functools.partial(_in_proj_kernel, head_dim=head_dim),
        grid=(T // tm, n_col),
        in_specs=[
            pl.BlockSpec((tm, D), lambda i, j: (i, 0)),
            pl.BlockSpec((1, D), lambda i, j: (0, 0)),
            pl.BlockSpec((1, D), lambda i, j: (0, 0)),
            pl.BlockSpec((D, tn), lambda i, j: (0, j)),
            pl.BlockSpec((tm, half), lambda i, j: (i, 0)),
            pl.BlockSpec((tm, half), lambda i, j: (i, 0)),
        ],
        out_specs=[
            pl.BlockSpec((tm, D), lambda i, j: (i, 0)),
            pl.BlockSpec((tm, tn), lambda i, j: (i, 0)),
            pl.BlockSpec((tm, tn), lambda i, j: (i, jnp.maximum(j - 1, 0))),
        ],
        out_shape=[
            jax.ShapeDtypeStruct((T, D), F32),
            jax.ShapeDtypeStruct((T, tn), BF16),
            jax.ShapeDtypeStruct((T, 4 * tn), BF16),
        ],
        scratch_shapes=[pltpu.VMEM((tm, D), BF16)],
        compiler_params=_params("parallel", "arbitrary"),
        name="in_proj",
    )(x2, ln_g, ln_b, w_bf, cos, sin)


def _fourier_tables(n1, n2, gw):
    def cos_sin(n, r, c):
        ang = 2.0 * np.pi * ((np.outer(r, c) % n).astype(np.float64)) / n
        return np.cos(ang), np.sin(ang)

    cg, sg = cos_sin(gw, np.arange(gw), np.arange(gw))
    c1, s1 = cos_sin(n1, np.arange(n1), np.arange(n1))
    c3, s3 = cos_sin(n2, np.arange(n2), np.arange(n2))
    tc, ts = cos_sin(n1 * n2, np.arange(n1), np.arange(n2))
    cs = np.concatenate([cg, sg], axis=1)
    m1 = np.block([[c1, -s1], [-s1, -c1]])
    m3 = np.concatenate([c3, s3], axis=1)
    as32 = lambda a: jnp.asarray(a.astype(np.float32))
    return as32(cs).astype(BF16), as32(m1).astype(BF16), as32(m3).astype(BF16), as32(tc), as32(ts)


def _fourier_a_kernel(uf_ref, cs_ref, m1_ref, twc_ref, tws_ref, zz_ref, *, nb, gw):
    bb = pl.program_id(0)
    n1 = uf_ref.shape[0]
    width = uf_ref.shape[1] // nb
    cs = cs_ref[...]
    m1 = m1_ref[...]
    lane = lax.broadcasted_iota(I32, twc_ref.shape, 1)
    for t in range(nb):
        x = uf_ref[:, t * width:(t + 1) * width]
        a_parts, b_parts = [], []
        for lo in range(0, width, gw):
            ab = jnp.dot(x[:, lo:lo + gw], cs, preferred_element_type=F32)
            a_parts.append(ab[:, :gw])
            b_parts.append(ab[:, gw:])
        v = jnp.concatenate([jnp.concatenate(a_parts, axis=1), jnp.concatenate(b_parts, axis=1)], axis=0)
        z = jnp.dot(m1, v.astype(BF16), preferred_element_type=F32)
        zr, zi = z[:n1], z[n1:]
        sel = lane == bb * nb + t
        tc = jnp.sum(jnp.where(sel, twc_ref[...], 0.0), axis=1, keepdims=True)
        ts = jnp.sum(jnp.where(sel, tws_ref[...], 0.0), axis=1, keepdims=True)
        base = t * 2 * width
        zz_ref[:, base:base + width] = (zr * tc + zi * ts).astype(BF16)
        zz_ref[:, base + width:base + 2 * width] = (zi * tc - zr * ts).astype(BF16)


def _fourier_b_kernel(zz_ref, m3_ref, yf_ref, *, scale):
    nc = zz_ref.shape[0]
    width = zz_ref.shape[2] // 2
    m3 = m3_ref[...]
    for t in range(nc):
        z = zz_ref[t]
        stacked = jnp.concatenate([z[:, :width], z[:, width:]], axis=0)
        x = jnp.dot(m3, stacked, preferred_element_type=F32)
        yf_ref[:, t * width:(t + 1) * width] = (x * scale).astype(BF16)


def _fourier_mix(uf, gw):
    T, F = uf.shape
    n1 = min(LANES, T)
    n2 = T // n1
    assert n1 * n2 == T
    cs, m1, m3, twc, tws = _fourier_tables(n1, n2, gw)
    nb = min(FOURIER_COLS_PER_STEP, n2)
    nc = min(FOURIER_COLS_PER_STEP, n1)
    zz = pl.pallas_call(
        functools.partial(_fourier_a_kernel, nb=nb, gw=gw),
        grid=(n2 // nb,),
        in_specs=[
            pl.BlockSpec((n1, nb * F), lambda b: (0, b)),
            pl.BlockSpec(cs.shape, lambda b: (0, 0)),
            pl.BlockSpec(m1.shape, lambda b: (0, 0)),
            pl.BlockSpec(twc.shape, lambda b: (0, 0)),
            pl.BlockSpec(tws.shape, lambda b: (0, 0)),
        ],
        out_specs=pl.BlockSpec((n1, nb * 2 * F), lambda b: (0, b)),
        out_shape=jax.ShapeDtypeStruct((n1, n2 * 2 * F), BF16),
        compiler_params=_params("parallel"),
        name="fourier_a",
    )(uf.reshape(n1, n2 * F), cs, m1, twc, tws)
    yf = pl.pallas_call(
        functools.partial(_fourier_b_kernel, scale=float(1.0 / np.sqrt(T * gw))),
        grid=(n1 // nc,),
        in_specs=[
            pl.BlockSpec((nc, n2, 2 * F), lambda c: (c, 0, 0)),
            pl.BlockSpec(m3.shape, lambda c: (0, 0)),
        ],
        out_specs=pl.BlockSpec((n2, nc * F), lambda c: (0, c)),
        out_shape=jax.ShapeDtypeStruct((n2, n1 * F), BF16),
        compiler_params=_params("parallel"),
        name="fourier_b",
    )(zz.reshape(n1, n2, 2 * F), m3)
    return yf.reshape(T, F)


_TAB_K_TO_END_F, _TAB_Q_FROM_START_F, _TAB_K_TO_END_B, _TAB_Q_FROM_START_B, _TAB_CHUNK_F, _TAB_CHUNK_B = range(6)


def _retention_kernel(lgf_ref, lgb_ref, qf_ref, kf_ref, vf_ref, qb_ref, kb_ref, vb_ref, of_ref, ob_ref,
                      sf_ref, sb_ref, dmat_ref, tab_ref, *, kscale):
    head = pl.program_id(0)
    n = pl.program_id(1)
    C, hd = qf_ref.shape

    @pl.when(n == 0)
    def _():
        lgf = lgf_ref[head]
        lgb = lgb_ref[head]
        r = lax.broadcasted_iota(I32, (C, C), 0).astype(F32)
        c = lax.broadcasted_iota(I32, (C, C), 1).astype(F32)
        diff = r - c
        dmat_ref[...] = kscale * jnp.where(diff >= 0.0, jnp.exp(lgf * jnp.maximum(diff, 0.0)),
                                           jnp.exp(lgb * jnp.maximum(-diff, 0.0)))
        rr = lax.broadcasted_iota(I32, (C, hd), 0).astype(F32)
        tab_ref[_TAB_K_TO_END_F] = kscale * jnp.exp(lgf * (C - 1.0 - rr))
        tab_ref[_TAB_Q_FROM_START_F] = jnp.exp(lgf * (rr + 1.0))
        tab_ref[_TAB_K_TO_END_B] = kscale * jnp.exp(lgb * rr)
        tab_ref[_TAB_Q_FROM_START_B] = jnp.exp(lgb * (C - rr))
        tab_ref[_TAB_CHUNK_F] = jnp.exp(jnp.full((C, hd), lgf * C, F32))
        tab_ref[_TAB_CHUNK_B] = jnp.exp(jnp.full((C, hd), lgb * C, F32))
        sf_ref[...] = jnp.zeros_like(sf_ref)
        sb_ref[...] = jnp.zeros_like(sb_ref)

    nt = (((1,), (1,)), ((), ()))
    tn = (((0,), (0,)), ((), ()))

    q = qf_ref[...]
    k = kf_ref[...]
    v = vf_ref[...]
    scores = lax.dot_general(q, k, nt, preferred_element_type=F32) * dmat_ref[...]
    intra = jnp.dot(scores.astype(BF16), v, preferred_element_type=F32)
    cross = jnp.dot(q, sf_ref[...].astype(BF16), preferred_element_type=F32) * tab_ref[_TAB_Q_FROM_START_F]
    of_ref[...] = intra + cross
    v_dec = (v.astype(F32) * tab_ref[_TAB_K_TO_END_F]).astype(BF16)
    sf_ref[...] = sf_ref[...] * tab_ref[_TAB_CHUNK_F] + lax.dot_general(k, v_dec, tn, preferred_element_type=F32)

    q = qb_ref[...]
    k = kb_ref[...]
    v = vb_ref[...]
    ob_ref[...] = jnp.dot(q, sb_ref[...].astype(BF16), preferred_element_type=F32) * tab_ref[_TAB_Q_FROM_START_B]
    v_dec = (v.astype(F32) * tab_ref[_TAB_K_TO_END_B]).astype(BF16)
    sb_ref[...] = sb_ref[...] * tab_ref[_TAB_CHUNK_B] + lax.dot_general(k, v_dec, tn, preferred_element_type=F32)


def _retention(ur, lg_fwd, lg_bwd, n_heads, head_dim):
    T = ur.shape[0]
    C = min(RET_CHUNK, T)
    assert C == head_dim
    N = T // C
    H = n_heads
    fwd = lambda col: (lambda h, n, *_: (n, col * H + h))
    bwd = lambda col: (lambda h, n, *_: (N - 1 - n, col * H + h))
    blk = lambda imap: pl.BlockSpec((C, head_dim), imap)
    return pl.pallas_call(
        functools.partial(_retention_kernel, kscale=float(head_dim) ** -0.5),
        grid_spec=pltpu.PrefetchScalarGridSpec(
            num_scalar_prefetch=2,
            grid=(H, N),
            in_specs=[blk(fwd(0)), blk(fwd(1)), blk(fwd(2)), blk(bwd(0)), blk(bwd(1)), blk(bwd(2))],
            out_specs=[blk(lambda h, n, *_: (n, h)), blk(lambda h, n, *_: (N - 1 - n, h))],
            scratch_shapes=[
                pltpu.VMEM((head_dim, head_dim), F32),
                pltpu.VMEM((head_dim, head_dim), F32),
                pltpu.VMEM((C, C), F32),
                pltpu.VMEM((6, C, head_dim), F32),
            ],
        ),
        out_shape=[jax.ShapeDtypeStruct((T, H * head_dim), F32)] * 2,
        compiler_params=_params("arbitrary", "arbitrary"),
        name="retention",
    )(lg_fwd, lg_bwd, ur, ur, ur, ur, ur, ur)


def _pack_bf16_pair(lo, hi):
    lo_bits = lax.bitcast_convert_type(lo.astype(BF16).astype(F32), U32)
    hi_bits = lax.bitcast_convert_type(hi.astype(BF16).astype(F32), U32)
    return (hi_bits & jnp.uint32(0xFFFF0000)) | (lo_bits >> 16)


def _unpack_bf16_pair(words):
    lo = lax.bitcast_convert_type(words << 16, F32).astype(BF16)
    hi = lax.bitcast_convert_type(words & jnp.uint32(0xFFFF0000), F32).astype(BF16)
    return lo, hi


def _out_proj_kernel(yf_ref, of_ref, ob_ref, g_ref, gng_ref, h_ref, wo_ref, lng_ref, lnb_ref, wr_ref, br_ref,
                     h1_ref, h1p_ref, topi_ref, gate_ref, rank_ref, cnt_ref, carry_ref, *, n_heads, top_k):
    i = pl.program_id(0)
    tm, rw = of_ref.shape
    hd = rw // n_heads
    fw = yf_ref.shape[1]
    D = h_ref.shape[1]
    E = wr_ref.shape[0]

    @pl.when(i == 0)
    def _():
        carry_ref[...] = jnp.zeros_like(carry_ref)

    o = of_ref[...] + ob_ref[...]
    parts = []
    for lo in range(0, rw, hd):
        oh = o[:, lo:lo + hd]
        mu = jnp.mean(oh, axis=-1, keepdims=True)
        d = oh - mu
        var = jnp.mean(d * d, axis=-1, keepdims=True)
        parts.append(d * lax.rsqrt(var + GN_EPS))
    on = jnp.concatenate(parts, axis=-1) * gng_ref[...]
    g = g_ref[...].astype(F32)
    yr = (g / (1.0 + jnp.exp(-g))) * on

    mix = (jnp.dot(yf_ref[...], wo_ref[:fw, :], preferred_element_type=F32)
           + jnp.dot(yr.astype(BF16), wo_ref[fw:, :], preferred_element_type=F32))
    h1 = _layer_norm(DEEPNORM_ALPHA * h_ref[...] + mix, lng_ref[...], lnb_ref[...])
    h1_ref[...] = h1
    h1p_ref[...] = _pack_bf16_pair(h1[:, :D // 2], h1[:, D // 2:])

    logits = lax.dot_general(wr_ref[...], h1, (((1,), (1,)), ((), ())), precision=lax.Precision.HIGHEST,
                             preferred_element_type=F32) + br_ref[...]
    eidx = lax.broadcasted_iota(I32, (E, tm), 0)
    cur = logits
    sel_idx, sel_val = [], []
    for _ in range(top_k):
        m = jnp.max(cur, axis=0, keepdims=True)
        idx = jnp.min(jnp.where(cur == m, eidx, E), axis=0, keepdims=True)
        sel_idx.append(idx)
        sel_val.append(m)
        cur = jnp.where(eidx == idx, -jnp.inf, cur)
    ex = [jnp.exp(val - sel_val[0]) for val in sel_val]
    denom = ex[0]
    for e_r in ex[1:]:
        denom = denom + e_r
    topi_ref[...] = jnp.concatenate(sel_idx, axis=0)
    gate_ref[...] = jnp.concatenate([e_r / denom for e_r in ex], axis=0)

    member = jnp.zeros((E, tm), F32)
    for idx in sel_idx:
        member = member + jnp.where(eidx == idx, 1.0, 0.0)
    earlier = lax.broadcasted_iota(I32, (tm, tm), 0) < lax.broadcasted_iota(I32, (tm, tm), 1)
    prefix = jnp.dot(member.astype(BF16), jnp.where(earlier, 1.0, 0.0).astype(BF16), preferred_element_type=F32)
    rank_full = prefix + carry_ref[...]
    ranks = [jnp.sum(jnp.where(eidx == idx, rank_full, 0.0), axis=0, keepdims=True) for idx in sel_idx]
    rank_ref[...] = jnp.concatenate(ranks, axis=0).astype(I32)
    carry = carry_ref[...] + jnp.sum(member, axis=1, keepdims=True)
    carry_ref[...] = carry
    cnt_ref[...] = jnp.broadcast_to(carry, cnt_ref.shape).astype(I32)


def _out_proj(yf, o_f, o_b, ur, gn_g, h, wo_bf, ln_g, ln_b, wr_t, br, n_heads):
    T, D = h.shape
    fw = yf.shape[1]
    rw = o_f.shape[1]
    E = wr_t.shape[0]
    tm = min(OUT_PROJ_ROWS, T)
    gate_col = ur.shape[1] // rw - 1
    row = lambda w: pl.BlockSpec((tm, w), lambda i: (i, 0))
    full = lambda a: pl.BlockSpec(a.shape, lambda i: (0,) * a.ndim)
    tok = lambda: pl.BlockSpec((TOP_K, tm), lambda i: (0, i))
    return pl.pallas_call(
        functools.partial(_out_proj_kernel, n_heads=n_heads, top_k=TOP_K),
        grid=(T // tm,),
        in_specs=[row(fw), row(rw), row(rw), pl.BlockSpec((tm, rw), lambda i: (i, gate_col)), full(gn_g), row(D),
                  full(wo_bf), full(ln_g), full(ln_b), full(wr_t), full(br)],
        out_specs=[row(D), row(D // 2), tok(), tok(), tok(), pl.BlockSpec((E, LANES), lambda i: (0, 0))],
        out_shape=[
            jax.ShapeDtypeStruct((T, D), F32),
            jax.ShapeDtypeStruct((T, D // 2), U32),
            jax.ShapeDtypeStruct((TOP_K, T), I32),
            jax.ShapeDtypeStruct((TOP_K, T), F32),
            jax.ShapeDtypeStruct((TOP_K, T), I32),
            jax.ShapeDtypeStruct((E, LANES), I32),
        ],
        scratch_shapes=[pltpu.VMEM((E, 1), F32)],
        compiler_params=_params("arbitrary"),
        name="out_proj",
    )(yf, o_f, o_b, ur, gn_g, h, wo_bf, ln_g, ln_b, wr_t, br)


def _dst_kernel(start_ref, topi_ref, rank_ref, dst_ref, *, n_experts):
    topi = topi_ref[...]
    dst = rank_ref[...]
    for e in range(n_experts):
        dst = dst + jnp.where(topi == e, start_ref[e], 0)
    dst_ref[...] = dst


def _dst_rows(pad_start, topi, rank):
    spec = pl.BlockSpec(topi.shape, lambda i, *_: (0, 0))
    return pl.pallas_call(
        functools.partial(_dst_kernel, n_experts=pad_start.shape[0]),
        grid_spec=pltpu.PrefetchScalarGridSpec(num_scalar_prefetch=1, grid=(1,), in_specs=[spec, spec], out_specs=spec),
        out_shape=jax.ShapeDtypeStruct(topi.shape, I32),
        compiler_params=_params("arbitrary"),
        name="dst",
    )(pad_start, topi, rank)


def _per_block_slots(dst, tb):
    K, T = dst.shape
    return dst.reshape(K, T // tb, tb).transpose(1, 0, 2).reshape(T // tb, 1, K * tb)


def _dispatch_kernel(dst_ref, h1p_ref, xs_in_ref, xs_ref, sem, *, top_k):
    del xs_in_ref
    tb = h1p_ref.shape[0]

    def row_copy(r, k):
        return pltpu.make_async_copy(h1p_ref.at[pl.ds(r, 1)], xs_ref.at[pl.ds(dst_ref[0, k * tb + r], 1)], sem)

    def start(r, carry):
        for k in range(top_k):
            row_copy(r, k).start()
        return carry

    def wait(r, carry):
        for k in range(top_k):
            row_copy(r, k).wait()
        return carry

    lax.fori_loop(0, tb, start, 0)
    lax.fori_loop(0, tb, wait, 0)


def _dispatch(dst, h1p, n_rows):
    T, W = h1p.shape
    tb = min(DISPATCH_ROWS, T)
    slots = _per_block_slots(dst, tb)
    xs0 = jnp.zeros((n_rows, W), U32)
    return pl.pallas_call(
        functools.partial(_dispatch_kernel, top_k=dst.shape[0]),
        grid=(T // tb,),
        in_specs=[
            pl.BlockSpec((None, 1, slots.shape[2]), lambda i: (i, 0, 0), memory_space=pltpu.SMEM),
            pl.BlockSpec((tb, W), lambda i: (i, 0)),
            pl.BlockSpec(memory_space=pl.ANY),
        ],
        out_specs=pl.BlockSpec(memory_space=pl.ANY),
        out_shape=jax.ShapeDtypeStruct((n_rows, W), U32),
        scratch_shapes=[pltpu.SemaphoreType.DMA(())],
        input_output_aliases={2: 0},
        compiler_params=_params("arbitrary"),
        name="dispatch",
    )(slots, h1p, xs0)


def _experts_kernel(be_ref, nact_ref, xs_ref, wg_ref, wl_ref, bg_ref, bl_ref, wd_ref, bd_ref, ys_ref, xb_ref, acc_ref):
    del be_ref
    i = pl.program_id(0)
    j = pl.program_id(1)
    half = xs_ref.shape[1]

    @pl.when(i < nact_ref[0])
    def _():
        @pl.when(j == 0)
        def _():
            lo, hi = _unpack_bf16_pair(xs_ref[...])
            xb_ref[:, :half] = lo
            xb_ref[:, half:] = hi
            acc_ref[...] = jnp.broadcast_to(bd_ref[...], acc_ref.shape)

        x = xb_ref[...]
        glu = jnp.dot(x, wg_ref[...].astype(BF16), preferred_element_type=F32) + bg_ref[...]
        lin = jnp.dot(x, wl_ref[...].astype(BF16), preferred_element_type=F32) + bl_ref[...]
        glu = jnp.minimum(glu, SWIGLU_LIMIT)
        lin = jnp.clip(lin, -SWIGLU_LIMIT, SWIGLU_LIMIT)
        act = glu / (1.0 + jnp.exp(-SWIGLU_ALPHA * glu)) * (lin + 1.0)
        acc_ref[...] += jnp.dot(act.astype(BF16), wd_ref[...].astype(BF16), preferred_element_type=F32)

        @pl.when(j == pl.num_programs(1) - 1)
        def _():
            ys_ref[...] = acc_ref[...]

    @pl.when((i >= nact_ref[0]) & (j == pl.num_programs(1) - 1))
    def _():
        ys_ref[...] = jnp.zeros_like(ys_ref)


def _experts(block_expert, n_active, xs, w_up, b_up, w_down, b_down):
    R, half = xs.shape
    E, D, ff2 = w_up.shape
    ff = ff2 // 2
    tm = EXPERT_ROWS
    tf = min(EXPERT_FF_COLS, ff)
    n_f = ff // tf
    row = lambda i, na: jnp.minimum(i, na[0] - 1)
    col = lambda i, j, na: jnp.where(i < na[0], j, n_f - 1)
    return pl.pallas_call(
        _experts_kernel,
        grid_spec=pltpu.PrefetchScalarGridSpec(
            num_scalar_prefetch=2,
            grid=(R // tm, n_f),
            in_specs=[
                pl.BlockSpec((tm, half), lambda i, j, be, na: (row(i, na), 0)),
                pl.BlockSpec((None, D, tf), lambda i, j, be, na: (be[row(i, na)], 0, col(i, j, na))),
                pl.BlockSpec((None, D, tf), lambda i, j, be, na: (be[row(i, na)], 0, n_f + col(i, j, na))),
                pl.BlockSpec((None, 1, tf), lambda i, j, be, na: (be[row(i, na)], 0, col(i, j, na))),
                pl.BlockSpec((None, 1, tf), lambda i, j, be, na: (be[row(i, na)], 0, n_f + col(i, j, na))),
                pl.BlockSpec((None, tf, D), lambda i, j, be, na: (be[row(i, na)], col(i, j, na), 0)),
                pl.BlockSpec((None, 1, D), lambda i, j, be, na: (be[row(i, na)], 0, 0)),
            ],
            out_specs=pl.BlockSpec((tm, D), lambda i, j, be, na: (i, 0)),
            scratch_shapes=[pltpu.VMEM((tm, D), BF16), pltpu.VMEM((tm, D), F32)],
        ),
        out_shape=jax.ShapeDtypeStruct((R, D), F32),
        compiler_params=_params("arbitrary", "arbitrary"),
        name="experts",
    )(block_expert, n_active, xs, w_up, w_up, b_up.reshape(E, 1, ff2), b_up.reshape(E, 1, ff2), w_down,
      b_down.reshape(E, 1, D))


def _combine_kernel(dst_ref, gate_ref, h1_ref, lng_ref, lnb_ref, ys_ref, out_ref, buf_ref, sem, *, top_k):
    tb = h1_ref.shape[0]

    def row_copy(r, k):
        return pltpu.make_async_copy(ys_ref.at[pl.ds(dst_ref[0, k * tb + r], 1)], buf_ref.at[k, pl.ds(r, 1)], sem)

    def start(r, carry):
        for k in range(top_k):
            row_copy(r, k).start()
        return carry

    def wait(r, carry):
        for k in range(top_k):
            row_copy(r, k).wait()
        return carry

    lax.fori_loop(0, tb, start, 0)
    lax.fori_loop(0, tb, wait, 0)

    diag = lax.broadcasted_iota(I32, (tb, tb), 0) == lax.broadcasted_iota(I32, (tb, tb), 1)
    ffn = jnp.zeros(h1_ref.shape, F32)
    for k in range(top_k):
        gate = jnp.sum(jnp.where(diag, gate_ref[k:k + 1, :], 0.0), axis=1, keepdims=True)
        ffn = ffn + buf_ref[k] * gate
    out_ref[...] = _layer_norm(DEEPNORM_ALPHA * h1_ref[...] + ffn, lng_ref[...], lnb_ref[...])


def _combine(dst, gates, h1, ln_g, ln_b, ys):
    T, D = h1.shape
    K = dst.shape[0]
    tb = min(COMBINE_ROWS, T)
    slots = _per_block_slots(dst, tb)
    return pl.pallas_call(
        functools.partial(_combine_kernel, top_k=K),
        grid=(T // tb,),
        in_specs=[
            pl.BlockSpec((None, 1, K * tb), lambda i: (i, 0, 0), memory_space=pltpu.SMEM),
            pl.BlockSpec((K, tb), lambda i: (0, i)),
            pl.BlockSpec((tb, D), lambda i: (i, 0)),
            pl.BlockSpec((1, D), lambda i: (0, 0)),
            pl.BlockSpec((1, D), lambda i: (0, 0)),
            pl.BlockSpec(memory_space=pl.ANY),
        ],
        out_specs=pl.BlockSpec((tb, D), lambda i: (i, 0)),
        out_shape=jax.ShapeDtypeStruct((T, D), F32),
        scratch_shapes=[pltpu.VMEM((K, tb, D), F32), pltpu.SemaphoreType.DMA(())],
        compiler_params=_params("arbitrary"),
        name="combine",
    )(slots, gates, h1, ln_g, ln_b, ys)


def _expert_schedule(counts, tm, n_blocks):
    padded = (counts + tm - 1) // tm * tm
    pad_end = jnp.cumsum(padded)
    pad_start = pad_end - padded
    n_experts = counts.shape[0]
    block_expert = jnp.minimum(
        jnp.searchsorted(pad_end, jnp.arange(n_blocks, dtype=I32) * tm, side="right"), n_experts - 1).astype(I32)
    n_active = (pad_end[-1] // tm).astype(I32).reshape(1)
    return pad_start.astype(I32), block_expert, n_active


def kernel(x, ln0_g, ln0_b, w_in, ret_decay_fwd, ret_decay_bwd, ret_gn_g, w_out, ln1_g, ln1_b, w_router, b_router,
           w_up, b_up, w_down, b_down, ln2_g, ln2_b):
    B, S, D = x.shape
    assert B == 1 and w_in.shape[0] == DEPTH
    T = S
    fw = D // 2
    rw = D - fw
    gw = fw // N_FOURIER_GROUPS
    head_dim = rw // N_RET_HEADS
    E = w_router.shape[-1]
    row = lambda a: a.reshape(1, -1)

    half = head_dim // 2
    inv = ROPE_BASE ** (-jnp.arange(half, dtype=F32) / half)
    ang = jnp.arange(T, dtype=jnp.int32).astype(F32)[:, None] * inv[None, :]

    h, uf, ur = _in_proj(x.reshape(T, D), row(ln0_g), row(ln0_b), w_in[0].astype(BF16), jnp.cos(ang), jnp.sin(ang),
                         head_dim)
    yf = _fourier_mix(uf, gw)
    lg_fwd = -jnp.exp(ret_decay_fwd[0].astype(F32))
    lg_bwd = -jnp.exp(ret_decay_bwd[0].astype(F32))
    o_f, o_b = _retention(ur, lg_fwd, lg_bwd, N_RET_HEADS, head_dim)
    h1, h1p, topi, gates, rank, cnt = _out_proj(
        yf, o_f, o_b, ur, row(ret_gn_g[0]), h, w_out[0].astype(BF16), row(ln1_g[0]), row(ln1_b[0]),
        w_router[0].T, b_router[0].reshape(E, 1), N_RET_HEADS)

    n_rows = T * TOP_K + E * EXPERT_ROWS
    pad_start, block_expert, n_active = _expert_schedule(cnt[:, 0], EXPERT_ROWS, n_rows // EXPERT_ROWS)
    dst = _dst_rows(pad_start, topi, rank)
    xs = _dispatch(dst, h1p, n_rows)
    ys = _experts(block_expert, n_active, xs, w_up.reshape(w_up.shape[1:]), b_up[0],
                  w_down.reshape(w_down.shape[1:]), b_down[0])
    out = _combine(dst, gates, h1, row(ln2_g[0]), row(ln2_b[0]), ys)
    return out.reshape(B, S, D)
```

```python
import functools

import numpy as np
import jax
import jax.numpy as jnp
from jax import lax
from jax.experimental import pallas as pl
from jax.experimental.pallas import tpu as pltpu

F32 = jnp.float32
BF16 = jnp.bfloat16
I32 = jnp.int32
U32 = jnp.uint32

N_FOURIER_GROUPS = 8
N_RET_HEADS = 4
ROPE_BASE = 10000.0
TOP_K = 4
SWIGLU_LIMIT = 7.0
SWIGLU_ALPHA = 1.702
LN_EPS = 1e-5
GN_EPS = 1e-6
DEPTH = 1
DEEPNORM_ALPHA = (2.0 * DEPTH) ** 0.25

V7X_VMEM_BYTES = 64 * 2**20
VMEM_LIMIT_BYTES = V7X_VMEM_BYTES * 3 // 4
LANES = 128

IN_PROJ_ROWS = 512
FOURIER_COLS_PER_STEP = 4
RET_CHUNK = 256
OUT_PROJ_ROWS = 256
DISPATCH_ROWS = 256
EXPERT_SUB_ROWS = 256
EXPERT_GROUP_SUBS = 9
EXPERT_FF_COLS = 256
EXPERT_VMEM_LIMIT_BYTES = V7X_VMEM_BYTES * 7 // 8
COMBINE_ROWS = 128


def _params(*semantics):
    return pltpu.CompilerParams(dimension_semantics=semantics, vmem_limit_bytes=VMEM_LIMIT_BYTES)


def _layer_norm(x, g, b):
    mu = jnp.mean(x, axis=-1, keepdims=True)
    xc = x - mu
    var = jnp.mean(xc * xc, axis=-1, keepdims=True)
    return xc * lax.rsqrt(var + LN_EPS) * g + b


def _in_proj_kernel(x_ref, g_ref, b_ref, w_ref, cos_ref, sin_ref, h_ref, uf_ref, ur_ref, hb_ref, *, head_dim):
    j = pl.program_id(1)

    @pl.when(j == 0)
    def _():
        h = _layer_norm(x_ref[...], g_ref[...], b_ref[...])
        h_ref[...] = h
        hb_ref[...] = h.astype(BF16)

    acc = jnp.dot(hb_ref[...], w_ref[...], preferred_element_type=F32)

    @pl.when(j == 0)
    def _():
        uf_ref[...] = acc.astype(BF16)

    @pl.when((j == 1) | (j == 2))
    def _():
        cos = cos_ref[...]
        sin = sin_ref[...]
        half = head_dim // 2
        for lo in range(0, acc.shape[1], head_dim):
            x1 = acc[:, lo:lo + half]
            x2 = acc[:, lo + half:lo + head_dim]
            ur_ref[:, lo:lo + half] = (x1 * cos - x2 * sin).astype(BF16)
            ur_ref[:, lo + half:lo + head_dim] = (x2 * cos + x1 * sin).astype(BF16)

    @pl.when(j >= 3)
    def _():
        ur_ref[...] = acc.astype(BF16)


def _in_proj(x2, ln_g, ln_b, w_bf, cos, sin, head_dim):
    T, D = x2.shape
    tn = D // 2
    n_col = w_bf.shape[1] // tn
    assert n_col == 5 and w_bf.shape[1] == 5 * tn
    tm = min(IN_PROJ_ROWS, T)
    half = head_dim // 2
    return pl.pallas_call(
        functools.partial(_in_proj_kernel, head_dim=head_dim),
        grid=(T // tm, n_col),
        in_specs=[
            pl.BlockSpec((tm, D), lambda i, j: (i, 0)),
            pl.BlockSpec((1, D), lambda i, j: (0, 0)),
            pl.BlockSpec((1, D), lambda i, j: (0, 0)),
            pl.BlockSpec((D, tn), lambda i, j: (0, j)),
            pl.BlockSpec((tm, half), lambda i, j: (i, 0)),
            pl.BlockSpec((tm, half), lambda i, j: (i, 0)),
        ],
        out_specs=[
            pl.BlockSpec((tm, D), lambda i, j: (i, 0)),
            pl.BlockSpec((tm, tn), lambda i, j: (i, 0)),
            pl.BlockSpec((tm, tn), lambda i, j: (i, jnp.maximum(j - 1, 0))),
        ],
        out_shape=[
            jax.ShapeDtypeStruct((T, D), F32),
            jax.ShapeDtypeStruct((T, tn), BF16),
            jax.ShapeDtypeStruct((T, 4 * tn), BF16),
        ],
        scratch_shapes=[pltpu.VMEM((tm, D), BF16)],
        compiler_params=_params("parallel", "arbitrary"),
        name="in_proj",
    )(x2, ln_g, ln_b, w_bf, cos, sin)


def _fourier_tables(n1, n2, gw):
    def cos_sin(n, r, c):
        ang = 2.0 * np.pi * ((np.outer(r, c) % n).astype(np.float64)) / n
        return np.cos(ang), np.sin(ang)

    cg, sg = cos_sin(gw, np.arange(gw), np.arange(gw))
    c1, s1 = cos_sin(n1, np.arange(n1), np.arange(n1))
    c3, s3 = cos_sin(n2, np.arange(n2), np.arange(n2))
    tc, ts = cos_sin(n1 * n2, np.arange(n1), np.arange(n2))
    cs = np.concatenate([cg, sg], axis=1)
    m1 = np.block([[c1, -s1], [-s1, -c1]])
    m3 = np.concatenate([c3, s3], axis=1)
    as32 = lambda a: jnp.asarray(a.astype(np.float32))
    return as32(cs).astype(BF16), as32(m1).astype(BF16), as32(m3).astype(BF16), as32(tc), as32(ts)


def _fourier_a_kernel(uf_ref, cs_ref, m1_ref, twc_ref, tws_ref, zz_ref, *, nb, gw):
    bb = pl.program_id(0)
    n1 = uf_ref.shape[0]
    width = uf_ref.shape[1] // nb
    cs = cs_ref[...]
    m1 = m1_ref[...]
    lane = lax.broadcasted_iota(I32, twc_ref.shape, 1)
    for t in range(nb):
        x = uf_ref[:, t * width:(t + 1) * width]
        a_parts, b_parts = [], []
        for lo in range(0, width, gw):
            ab = jnp.dot(x[:, lo:lo + gw], cs, preferred_element_type=F32)
            a_parts.append(ab[:, :gw])
            b_parts.append(ab[:, gw:])
        v = jnp.concatenate([jnp.concatenate(a_parts, axis=1), jnp.concatenate(b_parts, axis=1)], axis=0)
        z = jnp.dot(m1, v.astype(BF16), preferred_element_type=F32)
        zr, zi = z[:n1], z[n1:]
        sel = lane == bb * nb + t
        tc = jnp.sum(jnp.where(sel, twc_ref[...], 0.0), axis=1, keepdims=True)
        ts = jnp.sum(jnp.where(sel, tws_ref[...], 0.0), axis=1, keepdims=True)
        base = t * 2 * width
        zz_ref[:, base:base + width] = (zr * tc + zi * ts).astype(BF16)
        zz_ref[:, base + width:base + 2 * width] = (zi * tc - zr * ts).astype(BF16)


def _fourier_b_kernel(zz_ref, m3_ref, yf_ref, *, scale):
    nc = zz_ref.shape[0]
    width = zz_ref.shape[2] // 2
    m3 = m3_ref[...]
    for t in range(nc):
        z = zz_ref[t]
        stacked = jnp.concatenate([z[:, :width], z[:, width:]], axis=0)
        x = jnp.dot(m3, stacked, preferred_element_type=F32)
        yf_ref[:, t * width:(t + 1) * width] = (x * scale).astype(BF16)


def _fourier_mix(uf, gw):
    T, F = uf.shape
    n1 = min(LANES, T)
    n2 = T // n1
    assert n1 * n2 == T
    cs, m1, m3, twc, tws = _fourier_tables(n1, n2, gw)
    nb = min(FOURIER_COLS_PER_STEP, n2)
    nc = min(FOURIER_COLS_PER_STEP, n1)
    zz = pl.pallas_call(
        functools.partial(_fourier_a_kernel, nb=nb, gw=gw),
        grid=(n2 // nb,),
        in_specs=[
            pl.BlockSpec((n1, nb * F), lambda b: (0, b)),
            pl.BlockSpec(cs.shape, lambda b: (0, 0)),
            pl.BlockSpec(m1.shape, lambda b: (0, 0)),
            pl.BlockSpec(twc.shape, lambda b: (0, 0)),
            pl.BlockSpec(tws.shape, lambda b: (0, 0)),
        ],
        out_specs=pl.BlockSpec((n1, nb * 2 * F), lambda b: (0, b)),
        out_shape=jax.ShapeDtypeStruct((n1, n2 * 2 * F), BF16),
        compiler_params=_params("parallel"),
        name="fourier_a",
    )(uf.reshape(n1, n2 * F), cs, m1, twc, tws)
    yf = pl.pallas_call(
        functools.partial(_fourier_b_kernel, scale=float(1.0 / np.sqrt(T * gw))),
        grid=(n1 // nc,),
        in_specs=[
            pl.BlockSpec((nc, n2, 2 * F), lambda c: (c, 0, 0)),
            pl.BlockSpec(m3.shape, lambda c: (0, 0)),
        ],
        out_specs=pl.BlockSpec((n2, nc * F), lambda c: (0, c)),
        out_shape=jax.ShapeDtypeStruct((n2, n1 * F), BF16),
        compiler_params=_params("parallel"),
        name="fourier_b",
    )(zz.reshape(n1, n2, 2 * F), m3)
    return yf.reshape(T, F)


_TAB_K_TO_END_F, _TAB_Q_FROM_START_F, _TAB_K_TO_END_B, _TAB_Q_FROM_START_B, _TAB_CHUNK_F, _TAB_CHUNK_B = range(6)


def _retention_kernel(lgf_ref, lgb_ref, qf_ref, kf_ref, vf_ref, qb_ref, kb_ref, vb_ref, of_ref, ob_ref,
                      sf_ref, sb_ref, dmat_ref, tab_ref, *, kscale):
    head = pl.program_id(0)
    n = pl.program_id(1)
    C, hd = qf_ref.shape

    @pl.when(n == 0)
    def _():
        lgf = lgf_ref[head]
        lgb = lgb_ref[head]
        r = lax.broadcasted_iota(I32, (C, C), 0).astype(F32)
        c = lax.broadcasted_iota(I32, (C, C), 1).astype(F32)
        diff = r - c
        dmat_ref[...] = kscale * jnp.where(diff >= 0.0, jnp.exp(lgf * jnp.maximum(diff, 0.0)),
                                           jnp.exp(lgb * jnp.maximum(-diff, 0.0)))
        rr = lax.broadcasted_iota(I32, (C, hd), 0).astype(F32)
        tab_ref[_TAB_K_TO_END_F] = kscale * jnp.exp(lgf * (C - 1.0 - rr))
        tab_ref[_TAB_Q_FROM_START_F] = jnp.exp(lgf * (rr + 1.0))
        tab_ref[_TAB_K_TO_END_B] = kscale * jnp.exp(lgb * rr)
        tab_ref[_TAB_Q_FROM_START_B] = jnp.exp(lgb * (C - rr))
        tab_ref[_TAB_CHUNK_F] = jnp.exp(jnp.full((C, hd), lgf * C, F32))
        tab_ref[_TAB_CHUNK_B] = jnp.exp(jnp.full((C, hd), lgb * C, F32))
        sf_ref[...] = jnp.zeros_like(sf_ref)
        sb_ref[...] = jnp.zeros_like(sb_ref)

    nt = (((1,), (1,)), ((), ()))
    tn = (((0,), (0,)), ((), ()))

    q = qf_ref[...]
    k = kf_ref[...]
    v = vf_ref[...]
    scores = lax.dot_general(q, k, nt, preferred_element_type=F32) * dmat_ref[...]
    intra = jnp.dot(scores.astype(BF16), v, preferred_element_type=F32)
    cross = jnp.dot(q, sf_ref[...].astype(BF16), preferred_element_type=F32) * tab_ref[_TAB_Q_FROM_START_F]
    of_ref[...] = intra + cross
    v_dec = (v.astype(F32) * tab_ref[_TAB_K_TO_END_F]).astype(BF16)
    sf_ref[...] = sf_ref[...] * tab_ref[_TAB_CHUNK_F] + lax.dot_general(k, v_dec, tn, preferred_element_type=F32)

    q = qb_ref[...]
    k = kb_ref[...]
    v = vb_ref[...]
    ob_ref[...] = jnp.dot(q, sb_ref[...].astype(BF16), preferred_element_type=F32) * tab_ref[_TAB_Q_FROM_START_B]
    v_dec = (v.astype(F32) * tab_ref[_TAB_K_TO_END_B]).astype(BF16)
    sb_ref[...] = sb_ref[...] * tab_ref[_TAB_CHUNK_B] + lax.dot_general(k, v_dec, tn, preferred_element_type=F32)


def _retention(ur, lg_fwd, lg_bwd, n_heads, head_dim):
    T = ur.shape[0]
    C = min(RET_CHUNK, T)
    assert C == head_dim
    N = T // C
    H = n_heads
    fwd = lambda col: (lambda h, n, *_: (n, col * H + h))
    bwd = lambda col: (lambda h, n, *_: (N - 1 - n, col * H + h))
    blk = lambda imap: pl.BlockSpec((C, head_dim), imap)
    return pl.pallas_call(
        functools.partial(_retention_kernel, kscale=float(head_dim) ** -0.5),
        grid_spec=pltpu.PrefetchScalarGridSpec(
            num_scalar_prefetch=2,
            grid=(H, N),
            in_specs=[blk(fwd(0)), blk(fwd(1)), blk(fwd(2)), blk(bwd(0)), blk(bwd(1)), blk(bwd(2))],
            out_specs=[blk(lambda h, n, *_: (n, h)), blk(lambda h, n, *_: (N - 1 - n, h))],
            scratch_shapes=[
                pltpu.VMEM((head_dim, head_dim), F32),
                pltpu.VMEM((head_dim, head_dim), F32),
                pltpu.VMEM((C, C), F32),
                pltpu.VMEM((6, C, head_dim), F32),
            ],
        ),
        out_shape=[jax.ShapeDtypeStruct((T, H * head_dim), F32)] * 2,
        compiler_params=_params("arbitrary", "arbitrary"),
        name="retention",
    )(lg_fwd, lg_bwd, ur, ur, ur, ur, ur, ur)


def _pack_bf16_pair(lo, hi):
    lo_bits = lax.bitcast_convert_type(lo.astype(BF16).astype(F32), U32)
    hi_bits = lax.bitcast_convert_type(hi.astype(BF16).astype(F32), U32)
    return (hi_bits & jnp.uint32(0xFFFF0000)) | (lo_bits >> 16)


def _unpack_bf16_pair(words):
    lo = lax.bitcast_convert_type(words << 16, F32).astype(BF16)
    hi = lax.bitcast_convert_type(words & jnp.uint32(0xFFFF0000), F32).astype(BF16)
    return lo, hi


def _out_proj_kernel(yf_ref, of_ref, ob_ref, g_ref, gng_ref, h_ref, wo_ref, lng_ref, lnb_ref, wr_ref, br_ref,
                     h1_ref, h1p_ref, topi_ref, gate_ref, rank_ref, cnt_ref, carry_ref, *, n_heads, top_k):
    i = pl.program_id(0)
    tm, rw = of_ref.shape
    hd = rw // n_heads
    fw = yf_ref.shape[1]
    D = h_ref.shape[1]
    E = wr_ref.shape[0]

    @pl.when(i == 0)
    def _():
        carry_ref[...] = jnp.zeros_like(carry_ref)

    o = of_ref[...] + ob_ref[...]
    parts = []
    for lo in range(0, rw, hd):
        oh = o[:, lo:lo + hd]
        mu = jnp.mean(oh, axis=-1, keepdims=True)
        d = oh - mu
        var = jnp.mean(d * d, axis=-1, keepdims=True)
        parts.append(d * lax.rsqrt(var + GN_EPS))
    on = jnp.concatenate(parts, axis=-1) * gng_ref[...]
    g = g_ref[...].astype(F32)
    yr = (g / (1.0 + jnp.exp(-g))) * on

    mix = (jnp.dot(yf_ref[...], wo_ref[:fw, :], preferred_element_type=F32)
           + jnp.dot(yr.astype(BF16), wo_ref[fw:, :], preferred_element_type=F32))
    h1 = _layer_norm(DEEPNORM_ALPHA * h_ref[...] + mix, lng_ref[...], lnb_ref[...])
    h1_ref[...] = h1
    h1p_ref[...] = _pack_bf16_pair(h1[:, :D // 2], h1[:, D // 2:])

    logits = lax.dot_general(wr_ref[...], h1, (((1,), (1,)), ((), ())), precision=lax.Precision.HIGHEST,
                             preferred_element_type=F32) + br_ref[...]
    eidx = lax.broadcasted_iota(I32, (E, tm), 0)
    cur = logits
    sel_idx, sel_val = [], []
    for _ in range(top_k):
        m = jnp.max(cur, axis=0, keepdims=True)
        idx = jnp.min(jnp.where(cur == m, eidx, E), axis=0, keepdims=True)
        sel_idx.append(idx)
        sel_val.append(m)
        cur = jnp.where(eidx == idx, -jnp.inf, cur)
    ex = [jnp.exp(val - sel_val[0]) for val in sel_val]
    denom = ex[0]
    for e_r in ex[1:]:
        denom = denom + e_r
    topi_ref[...] = jnp.concatenate(sel_idx, axis=0)
    gate_ref[...] = jnp.concatenate([e_r / denom for e_r in ex], axis=0)

    member = jnp.zeros((E, tm), F32)
    for idx in sel_idx:
        member = member + jnp.where(eidx == idx, 1.0, 0.0)
    earlier = lax.broadcasted_iota(I32, (tm, tm), 0) < lax.broadcasted_iota(I32, (tm, tm), 1)
    prefix = jnp.dot(member.astype(BF16), jnp.where(earlier, 1.0, 0.0).astype(BF16), preferred_element_type=F32)
    rank_full = prefix + carry_ref[...]
    ranks = [jnp.sum(jnp.where(eidx == idx, rank_full, 0.0), axis=0, keepdims=True) for idx in sel_idx]
    rank_ref[...] = jnp.concatenate(ranks, axis=0).astype(I32)
    carry = carry_ref[...] + jnp.sum(member, axis=1, keepdims=True)
    carry_ref[...] = carry
    cnt_ref[...] = jnp.broadcast_to(carry, cnt_ref.shape).astype(I32)


def _out_proj(yf, o_f, o_b, ur, gn_g, h, wo_bf, ln_g, ln_b, wr_t, br, n_heads):
    T, D = h.shape
    fw = yf.shape[1]
    rw = o_f.shape[1]
    E = wr_t.shape[0]
    tm = min(OUT_PROJ_ROWS, T)
    gate_col = ur.shape[1] // rw - 1
    row = lambda w: pl.BlockSpec((tm, w), lambda i: (i, 0))
    full = lambda a: pl.BlockSpec(a.shape, lambda i: (0,) * a.ndim)
    tok = lambda: pl.BlockSpec((TOP_K, tm), lambda i: (0, i))
    return pl.pallas_call(
        functools.partial(_out_proj_kernel, n_heads=n_heads, top_k=TOP_K),
        grid=(T // tm,),
        in_specs=[row(fw), row(rw), row(rw), pl.BlockSpec((tm, rw), lambda i: (i, gate_col)), full(gn_g), row(D),
                  full(wo_bf), full(ln_g), full(ln_b), full(wr_t), full(br)],
        out_specs=[row(D), row(D // 2), tok(), tok(), tok(), pl.BlockSpec((E, LANES), lambda i: (0, 0))],
        out_shape=[
            jax.ShapeDtypeStruct((T, D), F32),
            jax.ShapeDtypeStruct((T, D // 2), U32),
            jax.ShapeDtypeStruct((TOP_K, T), I32),
            jax.ShapeDtypeStruct((TOP_K, T), F32),
            jax.ShapeDtypeStruct((TOP_K, T), I32),
            jax.ShapeDtypeStruct((E, LANES), I32),
        ],
        scratch_shapes=[pltpu.VMEM((E, 1), F32)],
        compiler_params=_params("arbitrary"),
        name="out_proj",
    )(yf, o_f, o_b, ur, gn_g, h, wo_bf, ln_g, ln_b, wr_t, br)


def _dst_kernel(start_ref, topi_ref, rank_ref, dst_ref, *, n_experts):
    topi = topi_ref[...]
    dst = rank_ref[...]
    for e in range(n_experts):
        dst = dst + jnp.where(topi == e, start_ref[e], 0)
    dst_ref[...] = dst


def _dst_rows(pad_start, topi, rank):
    spec = pl.BlockSpec(topi.shape, lambda i, *_: (0, 0))
    return pl.pallas_call(
        functools.partial(_dst_kernel, n_experts=pad_start.shape[0]),
        grid_spec=pltpu.PrefetchScalarGridSpec(num_scalar_prefetch=1, grid=(1,), in_specs=[spec, spec], out_specs=spec),
        out_shape=jax.ShapeDtypeStruct(topi.shape, I32),
        compiler_params=_params("arbitrary"),
        name="dst",
    )(pad_start, topi, rank)


def _per_block_slots(dst, tb):
    K, T = dst.shape
    return dst.reshape(K, T // tb, tb).transpose(1, 0, 2).reshape(T // tb, 1, K * tb)


def _dispatch_kernel(dst_ref, h1p_ref, xs_in_ref, xs_ref, sem, *, top_k):
    del xs_in_ref
    tb = h1p_ref.shape[0]

    def row_copy(r, k):
        return pltpu.make_async_copy(h1p_ref.at[pl.ds(r, 1)], xs_ref.at[pl.ds(dst_ref[0, k * tb + r], 1)], sem)

    def start(r, carry):
        for k in range(top_k):
            row_copy(r, k).start()
        return carry

    def wait(r, carry):
        for k in range(top_k):
            row_copy(r, k).wait()
        return carry

    lax.fori_loop(0, tb, start, 0)
    lax.fori_loop(0, tb, wait, 0)


def _dispatch(dst, h1p, n_rows):
    T, W = h1p.shape
    tb = min(DISPATCH_ROWS, T)
    slots = _per_block_slots(dst, tb)
    xs0 = jnp.zeros((n_rows, W), U32)
    return pl.pallas_call(
        functools.partial(_dispatch_kernel, top_k=dst.shape[0]),
        grid=(T // tb,),
        in_specs=[
            pl.BlockSpec((None, 1, slots.shape[2]), lambda i: (i, 0, 0), memory_space=pltpu.SMEM),
            pl.BlockSpec((tb, W), lambda i: (i, 0)),
            pl.BlockSpec(memory_space=pl.ANY),
        ],
        out_specs=pl.BlockSpec(memory_space=pl.ANY),
        out_shape=jax.ShapeDtypeStruct((n_rows, W), U32),
        scratch_shapes=[pltpu.SemaphoreType.DMA(())],
        input_output_aliases={2: 0},
        compiler_params=_params("arbitrary"),
        name="dispatch",
    )(slots, h1p, xs0)


def _experts_kernel(ge_ref, row0_ref, nsub_ref, meta_ref, wg_ref, wl_ref, bg_ref, bl_ref, wd_ref, bd_ref, xs_ref,
                    ys_ref, xbuf_ref, acc_ref, wgb_ref, wlb_ref, wdb_ref, stage_ref, xsem, ysem):
    del ge_ref
    s = pl.program_id(0)
    j = pl.program_id(1)
    last_j = pl.num_programs(1) - 1
    sub, half = stage_ref.shape[1:]

    def y_copy(ib, row):
        return pltpu.make_async_copy(acc_ref.at[ib], ys_ref.at[pl.ds(pl.multiple_of(row, sub), sub)], ysem)

    @pl.when(s < meta_ref[0])
    def _():
        n_sub = nsub_ref[s]
        row0 = row0_ref[s]
        wgb_ref[...] = wg_ref[...].astype(BF16)
        wlb_ref[...] = wl_ref[...].astype(BF16)
        wdb_ref[...] = wd_ref[...].astype(BF16)

        def x_copy(ib, slot):
            rows = pl.ds(pl.multiple_of(row0 + ib * sub, sub), sub)
            return pltpu.make_async_copy(xs_ref.at[rows], stage_ref.at[slot], xsem.at[slot])

        @pl.when(j == 0)
        def _():
            x_copy(0, 0).start()

        def body(ib, carry):
            @pl.when(j == 0)
            def _():
                slot = ib % 2
                x_copy(ib, slot).wait()

                @pl.when(ib + 1 < n_sub)
                def _():
                    x_copy(ib + 1, 1 - slot).start()

                lo, hi = _unpack_bf16_pair(stage_ref[slot])
                xbuf_ref[ib, :, :half] = lo
                xbuf_ref[ib, :, half:] = hi

            x = xbuf_ref[ib]
            glu = jnp.dot(x, wgb_ref[...], preferred_element_type=F32) + bg_ref[...]
            lin = jnp.dot(x, wlb_ref[...], preferred_element_type=F32) + bl_ref[...]
            glu = jnp.minimum(glu, SWIGLU_LIMIT)
            lin = jnp.clip(lin, -SWIGLU_LIMIT, SWIGLU_LIMIT)
            act = glu / (1.0 + jnp.exp(-SWIGLU_ALPHA * glu)) * (lin + 1.0)
            part = jnp.dot(act.astype(BF16), wdb_ref[...], preferred_element_type=F32)

            @pl.when(j == 0)
            def _():
                acc_ref[ib] = part + bd_ref[...]

            @pl.when(j > 0)
            def _():
                acc_ref[ib] += part

            @pl.when(j == last_j)
            def _():
                y_copy(ib, row0 + ib * sub).start()

            return carry

        lax.fori_loop(0, n_sub, body, 0)

        @pl.when(j == last_j)
        def _():
            def drain(ib, carry):
                y_copy(ib, row0 + ib * sub).wait()
                return carry

            lax.fori_loop(0, n_sub, drain, 0)

    @pl.when((s == pl.num_programs(0) - 1) & (j == last_j))
    def _():
        n_total = ys_ref.shape[0] // sub
        acc_ref[0] = jnp.zeros(acc_ref.shape[1:], F32)

        def fill(ib, carry):
            y_copy(0, ib * sub).start()
            return carry

        def drain(ib, carry):
            y_copy(0, ib * sub).wait()
            return carry

        lax.fori_loop(meta_ref[1], n_total, fill, 0)
        lax.fori_loop(meta_ref[1], n_total, drain, 0)


def _experts(group_expert, group_row0, group_nsub, meta, xs, w_up, b_up, w_down, b_down):
    R, half = xs.shape
    E, D, ff2 = w_up.shape
    ff = ff2 // 2
    sub = EXPERT_SUB_ROWS
    tf = min(EXPERT_FF_COLS, ff)
    n_f = ff // tf
    grp = lambda s, ge, meta: ge[jnp.minimum(s, meta[0] - 1)]
    col = lambda s, j, meta: jnp.where(s < meta[0], j, n_f - 1)
    return pl.pallas_call(
        _experts_kernel,
        grid_spec=pltpu.PrefetchScalarGridSpec(
            num_scalar_prefetch=4,
            grid=(group_expert.shape[0], n_f),
            in_specs=[
                pl.BlockSpec((None, D, tf), lambda s, j, ge, r0, ns, meta: (grp(s, ge, meta), 0, col(s, j, meta))),
                pl.BlockSpec((None, D, tf), lambda s, j, ge, r0, ns, meta: (grp(s, ge, meta), 0, n_f + col(s, j, meta))),
                pl.BlockSpec((None, 1, tf), lambda s, j, ge, r0, ns, meta: (grp(s, ge, meta), 0, col(s, j, meta))),
                pl.BlockSpec((None, 1, tf), lambda s, j, ge, r0, ns, meta: (grp(s, ge, meta), 0, n_f + col(s, j, meta))),
                pl.BlockSpec((None, tf, D), lambda s, j, ge, r0, ns, meta: (grp(s, ge, meta), col(s, j, meta), 0)),
                pl.BlockSpec((None, 1, D), lambda s, j, ge, r0, ns, meta: (grp(s, ge, meta), 0, 0)),
                pl.BlockSpec(memory_space=pl.ANY),
            ],
            out_specs=pl.BlockSpec(memory_space=pl.ANY),
            scratch_shapes=[
                pltpu.VMEM((EXPERT_GROUP_SUBS, sub, D), BF16),
                pltpu.VMEM((EXPERT_GROUP_SUBS, sub, D), F32),
                pltpu.VMEM((D, tf), BF16),
                pltpu.VMEM((D, tf), BF16),
                pltpu.VMEM((tf, D), BF16),
                pltpu.VMEM((2, sub, half), U32),
                pltpu.SemaphoreType.DMA((2,)),
                pltpu.SemaphoreType.DMA(()),
            ],
        ),
        out_shape=jax.ShapeDtypeStruct((R, D), F32),
        compiler_params=pltpu.CompilerParams(dimension_semantics=("arbitrary", "arbitrary"),
                                             vmem_limit_bytes=EXPERT_VMEM_LIMIT_BYTES),
        name="experts",
    )(group_expert, group_row0, group_nsub, meta, w_up, w_up, b_up.reshape(E, 1, ff2), b_up.reshape(E, 1, ff2),
      w_down, b_down.reshape(E, 1, D), xs)


def _combine_kernel(dst_ref, gate_ref, h1_ref, lng_ref, lnb_ref, ys_ref, out_ref, buf_ref, sem, *, top_k):
    tb = h1_ref.shape[0]

    def row_copy(r, k):
        return pltpu.make_async_copy(ys_ref.at[pl.ds(dst_ref[0, k * tb + r], 1)], buf_ref.at[k, pl.ds(r, 1)], sem)

    def start(r, carry):
        for k in range(top_k):
            row_copy(r, k).start()
        return carry

    def wait(r, carry):
        for k in range(top_k):
            row_copy(r, k).wait()
        return carry

    lax.fori_loop(0, tb, start, 0)
    lax.fori_loop(0, tb, wait, 0)

    diag = lax.broadcasted_iota(I32, (tb, tb), 0) == lax.broadcasted_iota(I32, (tb, tb), 1)
    ffn = jnp.zeros(h1_ref.shape, F32)
    for k in range(top_k):
        gate = jnp.sum(jnp.where(diag, gate_ref[k:k + 1, :], 0.0), axis=1, keepdims=True)
        ffn = ffn + buf_ref[k] * gate
    out_ref[...] = _layer_norm(DEEPNORM_ALPHA * h1_ref[...] + ffn, lng_ref[...], lnb_ref[...])


def _combine(dst, gates, h1, ln_g, ln_b, ys):
    T, D = h1.shape
    K = dst.shape[0]
    tb = min(COMBINE_ROWS, T)
    slots = _per_block_slots(dst, tb)
    return pl.pallas_call(
        functools.partial(_combine_kernel, top_k=K),
        grid=(T // tb,),
        in_specs=[
            pl.BlockSpec((None, 1, K * tb), lambda i: (i, 0, 0), memory_space=pltpu.SMEM),
            pl.BlockSpec((K, tb), lambda i: (0, i)),
            pl.BlockSpec((tb, D), lambda i: (i, 0)),
            pl.BlockSpec((1, D), lambda i: (0, 0)),
            pl.BlockSpec((1, D), lambda i: (0, 0)),
            pl.BlockSpec(memory_space=pl.ANY),
        ],
        out_specs=pl.BlockSpec((tb, D), lambda i: (i, 0)),
        out_shape=jax.ShapeDtypeStruct((T, D), F32),
        scratch_shapes=[pltpu.VMEM((K, tb, D), F32), pltpu.SemaphoreType.DMA(())],
        compiler_params=_params("arbitrary"),
        name="combine",
    )(slots, gates, h1, ln_g, ln_b, ys)


def _expert_schedule(counts, n_tokens_routed):
    sub, gb = EXPERT_SUB_ROWS, EXPERT_GROUP_SUBS
    n_experts = counts.shape[0]
    max_subs = n_tokens_routed // sub + n_experts
    max_groups = n_experts + max_subs // gb
    n_sub = (counts + sub - 1) // sub
    sub_end = jnp.cumsum(n_sub)
    sub_start = sub_end - n_sub
    n_grp = (n_sub + gb - 1) // gb
    grp_end = jnp.cumsum(n_grp)
    grp_start = grp_end - n_grp
    s = jnp.arange(max_groups, dtype=I32)
    expert = jnp.minimum(jnp.sum(grp_end[None, :] <= s[:, None], axis=1), n_experts - 1).astype(I32)
    onehot = (expert[:, None] == jnp.arange(n_experts, dtype=I32)[None, :]).astype(I32)
    pick = lambda v: jnp.sum(onehot * v[None, :], axis=1)
    local = s - pick(grp_start)
    row0 = (pick(sub_start) + local * gb) * sub
    nsub = jnp.clip(pick(n_sub) - local * gb, 0, gb)
    meta = jnp.stack([grp_end[-1], sub_end[-1]])
    to_i32 = lambda a: a.astype(I32)
    return to_i32(sub_start * sub), to_i32(expert), to_i32(row0), to_i32(nsub), to_i32(meta), max_subs * sub


def kernel(x, ln0_g, ln0_b, w_in, ret_decay_fwd, ret_decay_bwd, ret_gn_g, w_out, ln1_g, ln1_b, w_router, b_router,
           w_up, b_up, w_down, b_down, ln2_g, ln2_b):
    B, S, D = x.shape
    assert B == 1 and w_in.shape[0] == DEPTH
    T = S
    fw = D // 2
    rw = D - fw
    gw = fw // N_FOURIER_GROUPS
    head_dim = rw // N_RET_HEADS
    E = w_router.shape[-1]
    row = lambda a: a.reshape(1, -1)

    half = head_dim // 2
    inv = ROPE_BASE ** (-jnp.arange(half, dtype=F32) / half)
    ang = jnp.arange(T, dtype=jnp.int32).astype(F32)[:, None] * inv[None, :]

    h, uf, ur = _in_proj(x.reshape(T, D), row(ln0_g), row(ln0_b), w_in[0].astype(BF16), jnp.cos(ang), jnp.sin(ang),
                         head_dim)
    yf = _fourier_mix(uf, gw)
    lg_fwd = -jnp.exp(ret_decay_fwd[0].astype(F32))
    lg_bwd = -jnp.exp(ret_decay_bwd[0].astype(F32))
    o_f, o_b = _retention(ur, lg_fwd, lg_bwd, N_RET_HEADS, head_dim)
    h1, h1p, topi, gates, rank, cnt = _out_proj(
        yf, o_f, o_b, ur, row(ret_gn_g[0]), h, w_out[0].astype(BF16), row(ln1_g[0]), row(ln1_b[0]),
        w_router[0].T, b_router[0].reshape(E, 1), N_RET_HEADS)

    pad_start, group_expert, group_row0, group_nsub, meta, n_rows = _expert_schedule(cnt[:, 0], T * TOP_K)
    dst = _dst_rows(pad_start, topi, rank)
    xs = _dispatch(dst, h1p, n_rows)
    ys = _experts(group_expert, group_row0, group_nsub, meta, xs, w_up.reshape(w_up.shape[1:]), b_up[0],
                  w_down.reshape(w_down.shape[1:]), b_down[0])
    out = _combine(dst, gates, h1, row(ln2_g[0]), row(ln2_b[0]), ys)
    return out.reshape(B, S, D)
```

```python
import functools

import numpy as np
import jax
import jax.numpy as jnp
from jax import lax
from jax.experimental import pallas as pl
from jax.experimental.pallas import tpu as pltpu

F32 = jnp.float32
BF16 = jnp.bfloat16
I32 = jnp.int32
U32 = jnp.uint32

N_FOURIER_GROUPS = 8
N_RET_HEADS = 4
ROPE_BASE = 10000.0
TOP_K = 4
SWIGLU_LIMIT = 7.0
SWIGLU_ALPHA = 1.702
LN_EPS = 1e-5
GN_EPS = 1e-6
DEPTH = 1
DEEPNORM_ALPHA = (2.0 * DEPTH) ** 0.25

V7X_VMEM_BYTES = 64 * 2**20
VMEM_LIMIT_BYTES = V7X_VMEM_BYTES * 3 // 4
LANES = 128

IN_PROJ_ROWS = 512
IN_PROJ_PART_ROWS = 256
FOURIER_COLS_PER_STEP = 4
RET_CHUNK = 256
OUT_PROJ_ROWS = 512
OUT_PROJ_PART_ROWS = 256
ROUTER_ROWS = 1024
ROUTER_PART_ROWS = 256
DISPATCH_ROWS = 256
EXPERT_SUB_ROWS = 256
EXPERT_GROUP_SUBS = 9
EXPERT_FF_COLS = 256
EXPERT_VMEM_LIMIT_BYTES = V7X_VMEM_BYTES * 7 // 8
COMBINE_ROWS = 128
DMA_ISSUE_UNROLL = 8


def _params(*semantics):
    return pltpu.CompilerParams(dimension_semantics=semantics, vmem_limit_bytes=VMEM_LIMIT_BYTES)


def _layer_norm(x, g, b):
    mu = jnp.mean(x, axis=-1, keepdims=True)
    xc = x - mu
    var = jnp.mean(xc * xc, axis=-1, keepdims=True)
    return xc * lax.rsqrt(var + LN_EPS) * g + b


def _in_proj_kernel(x_ref, g_ref, b_ref, w_ref, cos_ref, sin_ref, h_ref, uf_ref, ur_ref, hb_ref, *, head_dim,
                    part_rows):
    j = pl.program_id(1)
    tm = x_ref.shape[0]
    parts = [slice(r0, r0 + part_rows) for r0 in range(0, tm, part_rows)]

    @pl.when(j == 0)
    def _():
        for rows in parts:
            h = _layer_norm(x_ref[rows, :], g_ref[...], b_ref[...])
            h_ref[rows, :] = h
            hb = h.astype(BF16)
            hb_ref[rows, :] = hb
            uf_ref[rows, :] = jnp.dot(hb, w_ref[...], preferred_element_type=F32).astype(BF16)

    @pl.when((j == 1) | (j == 2))
    def _():
        half = head_dim // 2
        for rows in parts:
            acc = jnp.dot(hb_ref[rows, :], w_ref[...], preferred_element_type=F32)
            cos = cos_ref[rows, :]
            sin = sin_ref[rows, :]
            for lo in range(0, acc.shape[1], head_dim):
                x1 = acc[:, lo:lo + half]
                x2 = acc[:, lo + half:lo + head_dim]
                ur_ref[rows, lo:lo + half] = (x1 * cos - x2 * sin).astype(BF16)
                ur_ref[rows, lo + half:lo + head_dim] = (x2 * cos + x1 * sin).astype(BF16)

    @pl.when(j >= 3)
    def _():
        for rows in parts:
            ur_ref[rows, :] = jnp.dot(hb_ref[rows, :], w_ref[...], preferred_element_type=F32).astype(BF16)


def _in_proj(x2, ln_g, ln_b, w_bf, cos, sin, head_dim):
    T, D = x2.shape
    tn = D // 2
    n_col = w_bf.shape[1] // tn
    assert n_col == 5 and w_bf.shape[1] == 5 * tn
    tm = min(IN_PROJ_ROWS, T)
    half = head_dim // 2
    return pl.pallas_call(
        functools.partial(_in_proj_kernel, head_dim=head_dim, part_rows=min(IN_PROJ_PART_ROWS, tm)),
        grid=(T // tm, n_col),
        in_specs=[
            pl.BlockSpec((tm, D), lambda i, j: (i, 0)),
            pl.BlockSpec((1, D), lambda i, j: (0, 0)),
            pl.BlockSpec((1, D), lambda i, j: (0, 0)),
            pl.BlockSpec((D, tn), lambda i, j: (0, j)),
            pl.BlockSpec((tm, half), lambda i, j: (i, 0)),
            pl.BlockSpec((tm, half), lambda i, j: (i, 0)),
        ],
        out_specs=[
            pl.BlockSpec((tm, D), lambda i, j: (i, 0)),
            pl.BlockSpec((tm, tn), lambda i, j: (i, 0)),
            pl.BlockSpec((tm, tn), lambda i, j: (i, jnp.maximum(j - 1, 0))),
        ],
        out_shape=[
            jax.ShapeDtypeStruct((T, D), F32),
            jax.ShapeDtypeStruct((T, tn), BF16),
            jax.ShapeDtypeStruct((T, 4 * tn), BF16),
        ],
        scratch_shapes=[pltpu.VMEM((tm, D), BF16)],
        compiler_params=_params("parallel", "arbitrary"),
        name="in_proj",
    )(x2, ln_g, ln_b, w_bf, cos, sin)


def _fourier_tables(n1, n2, gw):
    def cos_sin(n, r, c):
        ang = 2.0 * np.pi * ((np.outer(r, c) % n).astype(np.float64)) / n
        return np.cos(ang), np.sin(ang)

    cg, sg = cos_sin(gw, np.arange(gw), np.arange(gw))
    c1, s1 = cos_sin(n1, np.arange(n1), np.arange(n1))
    c3, s3 = cos_sin(n2, np.arange(n2), np.arange(n2))
    tc, ts = cos_sin(n1 * n2, np.arange(n1), np.arange(n2))
    cs = np.concatenate([cg, sg], axis=1)
    m1 = np.block([[c1, -s1], [-s1, -c1]])
    m3 = np.concatenate([c3, s3], axis=1)
    as32 = lambda a: jnp.asarray(a.astype(np.float32))
    return as32(cs).astype(BF16), as32(m1).astype(BF16), as32(m3).astype(BF16), as32(tc), as32(ts)


def _fourier_a_kernel(uf_ref, cs_ref, m1_ref, twc_ref, tws_ref, zz_ref, *, nb, gw):
    bb = pl.program_id(0)
    n1 = uf_ref.shape[0]
    width = uf_ref.shape[1] // nb
    cs = cs_ref[...]
    m1 = m1_ref[...]
    lane = lax.broadcasted_iota(I32, twc_ref.shape, 1)
    for t in range(nb):
        x = uf_ref[:, t * width:(t + 1) * width]
        a_parts, b_parts = [], []
        for lo in range(0, width, gw):
            ab = jnp.dot(x[:, lo:lo + gw], cs, preferred_element_type=F32)
            a_parts.append(ab[:, :gw])
            b_parts.append(ab[:, gw:])
        v = jnp.concatenate([jnp.concatenate(a_parts, axis=1), jnp.concatenate(b_parts, axis=1)], axis=0)
        z = jnp.dot(m1, v.astype(BF16), preferred_element_type=F32)
        zr, zi = z[:n1], z[n1:]
        sel = lane == bb * nb + t
        tc = jnp.sum(jnp.where(sel, twc_ref[...], 0.0), axis=1, keepdims=True)
        ts = jnp.sum(jnp.where(sel, tws_ref[...], 0.0), axis=1, keepdims=True)
        base = t * 2 * width
        zz_ref[:, base:base + width] = (zr * tc + zi * ts).astype(BF16)
        zz_ref[:, base + width:base + 2 * width] = (zi * tc - zr * ts).astype(BF16)


def _fourier_b_kernel(zz_ref, m3_ref, yf_ref, *, scale):
    nc = zz_ref.shape[0]
    width = zz_ref.shape[2] // 2
    m3 = m3_ref[...]
    for t in range(nc):
        z = zz_ref[t]
        stacked = jnp.concatenate([z[:, :width], z[:, width:]], axis=0)
        x = jnp.dot(m3, stacked, preferred_element_type=F32)
        yf_ref[:, t * width:(t + 1) * width] = (x * scale).astype(BF16)


def _fourier_mix(uf, gw):
    T, F = uf.shape
    n1 = min(LANES, T)
    n2 = T // n1
    assert n1 * n2 == T
    cs, m1, m3, twc, tws = _fourier_tables(n1, n2, gw)
    nb = min(FOURIER_COLS_PER_STEP, n2)
    nc = min(FOURIER_COLS_PER_STEP, n1)
    zz = pl.pallas_call(
        functools.partial(_fourier_a_kernel, nb=nb, gw=gw),
        grid=(n2 // nb,),
        in_specs=[
            pl.BlockSpec((n1, nb * F), lambda b: (0, b)),
            pl.BlockSpec(cs.shape, lambda b: (0, 0)),
            pl.BlockSpec(m1.shape, lambda b: (0, 0)),
            pl.BlockSpec(twc.shape, lambda b: (0, 0)),
            pl.BlockSpec(tws.shape, lambda b: (0, 0)),
        ],
        out_specs=pl.BlockSpec((n1, nb * 2 * F), lambda b: (0, b)),
        out_shape=jax.ShapeDtypeStruct((n1, n2 * 2 * F), BF16),
        compiler_params=_params("parallel"),
        name="fourier_a",
    )(uf.reshape(n1, n2 * F), cs, m1, twc, tws)
    yf = pl.pallas_call(
        functools.partial(_fourier_b_kernel, scale=float(1.0 / np.sqrt(T * gw))),
        grid=(n1 // nc,),
        in_specs=[
            pl.BlockSpec((nc, n2, 2 * F), lambda c: (c, 0, 0)),
            pl.BlockSpec(m3.shape, lambda c: (0, 0)),
        ],
        out_specs=pl.BlockSpec((n2, nc * F), lambda c: (0, c)),
        out_shape=jax.ShapeDtypeStruct((n2, n1 * F), BF16),
        compiler_params=_params("parallel"),
        name="fourier_b",
    )(zz.reshape(n1, n2, 2 * F), m3)
    return yf.reshape(T, F)


_TAB_K_TO_END_F, _TAB_Q_FROM_START_F, _TAB_K_TO_END_B, _TAB_Q_FROM_START_B, _TAB_CHUNK_F, _TAB_CHUNK_B = range(6)


def _retention_kernel(lgf_ref, lgb_ref, qf_ref, kf_ref, vf_ref, qb_ref, kb_ref, vb_ref, of_ref, ob_ref,
                      sf_ref, sb_ref, dmat_ref, tab_ref, *, kscale, n_heads):
    n = pl.program_id(0)
    C = qf_ref.shape[0]
    hd = qf_ref.shape[1] // n_heads

    @pl.when(n == 0)
    def _():
        r = lax.broadcasted_iota(I32, (C, C), 0).astype(F32)
        c = lax.broadcasted_iota(I32, (C, C), 1).astype(F32)
        diff = r - c
        rr = lax.broadcasted_iota(I32, (C, hd), 0).astype(F32)
        for head in range(n_heads):
            lgf = lgf_ref[head]
            lgb = lgb_ref[head]
            dmat_ref[head] = kscale * jnp.where(diff >= 0.0, jnp.exp(lgf * jnp.maximum(diff, 0.0)),
                                                jnp.exp(lgb * jnp.maximum(-diff, 0.0)))
            tab_ref[head, _TAB_K_TO_END_F] = kscale * jnp.exp(lgf * (C - 1.0 - rr))
            tab_ref[head, _TAB_Q_FROM_START_F] = jnp.exp(lgf * (rr + 1.0))
            tab_ref[head, _TAB_K_TO_END_B] = kscale * jnp.exp(lgb * rr)
            tab_ref[head, _TAB_Q_FROM_START_B] = jnp.exp(lgb * (C - rr))
            tab_ref[head, _TAB_CHUNK_F] = jnp.exp(jnp.full((C, hd), lgf * C, F32))
            tab_ref[head, _TAB_CHUNK_B] = jnp.exp(jnp.full((C, hd), lgb * C, F32))
        sf_ref[...] = jnp.zeros_like(sf_ref)
        sb_ref[...] = jnp.zeros_like(sb_ref)

    nt = (((1,), (1,)), ((), ()))
    tn = (((0,), (0,)), ((), ()))

    for head in range(n_heads):
        cols = slice(head * hd, (head + 1) * hd)
        q = qf_ref[:, cols]
        k = kf_ref[:, cols]
        v = vf_ref[:, cols]
        scores = lax.dot_general(q, k, nt, preferred_element_type=F32) * dmat_ref[head]
        intra = jnp.dot(scores.astype(BF16), v, preferred_element_type=F32)
        cross = (jnp.dot(q, sf_ref[head].astype(BF16), preferred_element_type=F32)
                 * tab_ref[head, _TAB_Q_FROM_START_F])
        of_ref[:, cols] = intra + cross
        v_dec = (v.astype(F32) * tab_ref[head, _TAB_K_TO_END_F]).astype(BF16)
        sf_ref[head] = (sf_ref[head] * tab_ref[head, _TAB_CHUNK_F]
                        + lax.dot_general(k, v_dec, tn, preferred_element_type=F32))

        q = qb_ref[:, cols]
        k = kb_ref[:, cols]
        v = vb_ref[:, cols]
        ob_ref[:, cols] = (jnp.dot(q, sb_ref[head].astype(BF16), preferred_element_type=F32)
                           * tab_ref[head, _TAB_Q_FROM_START_B])
        v_dec = (v.astype(F32) * tab_ref[head, _TAB_K_TO_END_B]).astype(BF16)
        sb_ref[head] = (sb_ref[head] * tab_ref[head, _TAB_CHUNK_B]
                        + lax.dot_general(k, v_dec, tn, preferred_element_type=F32))


def _retention(ur, lg_fwd, lg_bwd, n_heads, head_dim):
    T = ur.shape[0]
    C = min(RET_CHUNK, T)
    assert C == head_dim
    N = T // C
    H = n_heads
    rw = H * head_dim
    fwd = lambda col: (lambda n, *_: (n, col))
    bwd = lambda col: (lambda n, *_: (N - 1 - n, col))
    blk = lambda imap: pl.BlockSpec((C, rw), imap)
    return pl.pallas_call(
        functools.partial(_retention_kernel, kscale=float(head_dim) ** -0.5, n_heads=H),
        grid_spec=pltpu.PrefetchScalarGridSpec(
            num_scalar_prefetch=2,
            grid=(N,),
            in_specs=[blk(fwd(0)), blk(fwd(1)), blk(fwd(2)), blk(bwd(0)), blk(bwd(1)), blk(bwd(2))],
            out_specs=[blk(fwd(0)), blk(bwd(0))],
            scratch_shapes=[
                pltpu.VMEM((H, head_dim, head_dim), F32),
                pltpu.VMEM((H, head_dim, head_dim), F32),
                pltpu.VMEM((H, C, C), F32),
                pltpu.VMEM((H, 6, C, head_dim), F32),
            ],
        ),
        out_shape=[jax.ShapeDtypeStruct((T, rw), F32)] * 2,
        compiler_params=_params("arbitrary"),
        name="retention",
    )(lg_fwd, lg_bwd, ur, ur, ur, ur, ur, ur)


def _pack_bf16_pair(lo, hi):
    lo_bits = lax.bitcast_convert_type(lo.astype(BF16).astype(F32), U32)
    hi_bits = lax.bitcast_convert_type(hi.astype(BF16).astype(F32), U32)
    return (hi_bits & jnp.uint32(0xFFFF0000)) | (lo_bits >> 16)


def _unpack_bf16_pair(words):
    lo = lax.bitcast_convert_type(words << 16, F32).astype(BF16)
    hi = lax.bitcast_convert_type(words & jnp.uint32(0xFFFF0000), F32).astype(BF16)
    return lo, hi


def _out_proj_kernel(yf_ref, of_ref, ob_ref, g_ref, gng_ref, h_ref, wo_ref, lng_ref, lnb_ref, h1_ref, h1p_ref, *,
                     n_heads, part_rows):
    tm, rw = of_ref.shape
    hd = rw // n_heads
    fw = yf_ref.shape[1]
    D = h_ref.shape[1]
    for r0 in range(0, tm, part_rows):
        rows = slice(r0, r0 + part_rows)
        o = of_ref[rows, :] + ob_ref[rows, :]
        parts = []
        for lo in range(0, rw, hd):
            oh = o[:, lo:lo + hd]
            mu = jnp.mean(oh, axis=-1, keepdims=True)
            d = oh - mu
            var = jnp.mean(d * d, axis=-1, keepdims=True)
            parts.append(d * lax.rsqrt(var + GN_EPS))
        on = jnp.concatenate(parts, axis=-1) * gng_ref[...]
        g = g_ref[rows, :].astype(F32)
        yr = (g / (1.0 + jnp.exp(-g))) * on

        mix = (jnp.dot(yf_ref[rows, :], wo_ref[:fw, :], preferred_element_type=F32)
               + jnp.dot(yr.astype(BF16), wo_ref[fw:, :], preferred_element_type=F32))
        h1 = _layer_norm(DEEPNORM_ALPHA * h_ref[rows, :] + mix, lng_ref[...], lnb_ref[...])
        h1_ref[rows, :] = h1
        h1p_ref[rows, :] = _pack_bf16_pair(h1[:, :D // 2], h1[:, D // 2:])


def _out_proj(yf, o_f, o_b, ur, gn_g, h, wo_bf, ln_g, ln_b, n_heads):
    T, D = h.shape
    fw = yf.shape[1]
    rw = o_f.shape[1]
    tm = min(OUT_PROJ_ROWS, T)
    gate_col = ur.shape[1] // rw - 1
    row = lambda w: pl.BlockSpec((tm, w), lambda i: (i, 0))
    full = lambda a: pl.BlockSpec(a.shape, lambda i: (0,) * a.ndim)
    return pl.pallas_call(
        functools.partial(_out_proj_kernel, n_heads=n_heads, part_rows=min(OUT_PROJ_PART_ROWS, tm)),
        grid=(T // tm,),
        in_specs=[row(fw), row(rw), row(rw), pl.BlockSpec((tm, rw), lambda i: (i, gate_col)), full(gn_g), row(D),
                  full(wo_bf), full(ln_g), full(ln_b)],
        out_specs=[row(D), row(D // 2)],
        out_shape=[jax.ShapeDtypeStruct((T, D), F32), jax.ShapeDtypeStruct((T, D // 2), U32)],
        compiler_params=_params("parallel"),
        name="out_proj",
    )(yf, o_f, o_b, ur, gn_g, h, wo_bf, ln_g, ln_b)


def _router_kernel(h1_ref, wr_ref, wrl_ref, br_ref, topi_ref, gate_ref, rank_ref, cnt_ref, carry_ref, *, top_k,
                   part_rows):
    i = pl.program_id(0)
    tm = h1_ref.shape[0]
    E = wr_ref.shape[0]
    tp = part_rows

    @pl.when(i == 0)
    def _():
        carry_ref[...] = jnp.zeros_like(carry_ref)

    eidx = lax.broadcasted_iota(I32, (E, tp), 0)
    earlier = lax.broadcasted_iota(I32, (tp, tp), 0) < lax.broadcasted_iota(I32, (tp, tp), 1)
    earlier = jnp.where(earlier, 1.0, 0.0).astype(BF16)
    carry = carry_ref[...]

    for r0 in range(0, tm, tp):
        rows = slice(r0, r0 + tp)
        h = h1_ref[rows, :]
        h_hi = h.astype(BF16)
        h_lo = (h - h_hi.astype(F32)).astype(BF16)
        nt = (((1,), (1,)), ((), ()))
        logits = (lax.dot_general(wr_ref[...], h_hi, nt, preferred_element_type=F32)
                  + lax.dot_general(wr_ref[...], h_lo, nt, preferred_element_type=F32)
                  + lax.dot_general(wrl_ref[...], h_hi, nt, preferred_element_type=F32)) + br_ref[...]
        cur = logits
        sel_idx, sel_val = [], []
        for _ in range(top_k):
            m = jnp.max(cur, axis=0, keepdims=True)
            idx = jnp.min(jnp.where(cur == m, eidx, E), axis=0, keepdims=True)
            sel_idx.append(idx)
            sel_val.append(m)
            cur = jnp.where(eidx == idx, -jnp.inf, cur)
        ex = [jnp.exp(val - sel_val[0]) for val in sel_val]
        denom = ex[0]
        for e_r in ex[1:]:
            denom = denom + e_r
        topi_ref[:, rows] = jnp.concatenate(sel_idx, axis=0)
        gate_ref[:, rows] = jnp.concatenate([e_r / denom for e_r in ex], axis=0)

        member = jnp.zeros((E, tp), F32)
        for idx in sel_idx:
            member = member + jnp.where(eidx == idx, 1.0, 0.0)
        rank_full = jnp.dot(member.astype(BF16), earlier, preferred_element_type=F32) + carry
        ranks = [jnp.sum(jnp.where(eidx == idx, rank_full, 0.0), axis=0, keepdims=True) for idx in sel_idx]
        rank_ref[:, rows] = jnp.concatenate(ranks, axis=0).astype(I32)
        carry = carry + jnp.sum(member, axis=1, keepdims=True)

    carry_ref[...] = carry
    cnt_ref[...] = jnp.broadcast_to(carry, cnt_ref.shape).astype(I32)


def _router(h1, wr_t, br):
    T, D = h1.shape
    E = wr_t.shape[0]
    wr_hi = wr_t.astype(BF16)
    wr_lo = (wr_t - wr_hi.astype(F32)).astype(BF16)
    tm = min(ROUTER_ROWS, T)
    full = lambda a: pl.BlockSpec(a.shape, lambda i: (0,) * a.ndim)
    tok = lambda: pl.BlockSpec((TOP_K, tm), lambda i: (0, i))
    return pl.pallas_call(
        functools.partial(_router_kernel, top_k=TOP_K, part_rows=min(ROUTER_PART_ROWS, tm)),
        grid=(T // tm,),
        in_specs=[pl.BlockSpec((tm, D), lambda i: (i, 0)), full(wr_hi), full(wr_lo), full(br)],
        out_specs=[tok(), tok(), tok(), pl.BlockSpec((E, LANES), lambda i: (0, 0))],
        out_shape=[
            jax.ShapeDtypeStruct((TOP_K, T), I32),
            jax.ShapeDtypeStruct((TOP_K, T), F32),
            jax.ShapeDtypeStruct((TOP_K, T), I32),
            jax.ShapeDtypeStruct((E, LANES), I32),
        ],
        scratch_shapes=[pltpu.VMEM((E, 1), F32)],
        compiler_params=_params("arbitrary"),
        name="router",
    )(h1, wr_hi, wr_lo, br)


def _dst_kernel(start_ref, topi_ref, rank_ref, dst_ref, *, n_experts):
    topi = topi_ref[...]
    dst = rank_ref[...]
    for e in range(n_experts):
        dst = dst + jnp.where(topi == e, start_ref[e], 0)
    dst_ref[...] = dst


def _dst_rows(pad_start, topi, rank):
    spec = pl.BlockSpec(topi.shape, lambda i, *_: (0, 0))
    return pl.pallas_call(
        functools.partial(_dst_kernel, n_experts=pad_start.shape[0]),
        grid_spec=pltpu.PrefetchScalarGridSpec(num_scalar_prefetch=1, grid=(1,), in_specs=[spec, spec], out_specs=spec),
        out_shape=jax.ShapeDtypeStruct(topi.shape, I32),
        compiler_params=_params("arbitrary"),
        name="dst",
    )(pad_start, topi, rank)


def _per_block_slots(dst, tb):
    K, T = dst.shape
    return dst.reshape(K, T // tb, tb).transpose(1, 0, 2).reshape(T // tb, 1, K * tb)


def _dispatch_kernel(dst_ref, h1p_ref, xs_in_ref, xs_ref, sem, *, top_k):
    del xs_in_ref
    tb = h1p_ref.shape[0]

    def start(c, carry):
        for u in range(DMA_ISSUE_UNROLL):
            r = c * DMA_ISSUE_UNROLL + u
            for k in range(top_k):
                row = dst_ref[0, k * tb + r]
                pltpu.make_async_copy(h1p_ref.at[pl.ds(r, 1)], xs_ref.at[pl.ds(row, 1)], sem).start()
        return carry

    lax.fori_loop(0, tb // DMA_ISSUE_UNROLL, start, 0)
    for k in range(top_k):
        pltpu.make_async_copy(h1p_ref, xs_ref.at[pl.ds(0, tb)], sem).wait()


def _dispatch(dst, h1p, n_rows):
    T, W = h1p.shape
    tb = min(DISPATCH_ROWS, T)
    slots = _per_block_slots(dst, tb)
    xs0 = jnp.zeros((n_rows, W), U32)
    return pl.pallas_call(
        functools.partial(_dispatch_kernel, top_k=dst.shape[0]),
        grid=(T // tb,),
        in_specs=[
            pl.BlockSpec((None, 1, slots.shape[2]), lambda i: (i, 0, 0), memory_space=pltpu.SMEM),
            pl.BlockSpec((tb, W), lambda i: (i, 0)),
            pl.BlockSpec(memory_space=pl.ANY),
        ],
        out_specs=pl.BlockSpec(memory_space=pl.ANY),
        out_shape=jax.ShapeDtypeStruct((n_rows, W), U32),
        scratch_shapes=[pltpu.SemaphoreType.DMA(())],
        input_output_aliases={2: 0},
        compiler_params=_params("arbitrary"),
        name="dispatch",
    )(slots, h1p, xs0)


def _experts_kernel(ge_ref, row0_ref, nsub_ref, meta_ref, wg_ref, wl_ref, bg_ref, bl_ref, wd_ref, bd_ref, xs_ref,
                    ys_ref, xbuf_ref, acc_ref, wgb_ref, wlb_ref, wdb_ref, stage_ref, xsem, ysem):
    del ge_ref
    s = pl.program_id(0)
    j = pl.program_id(1)
    last_j = pl.num_programs(1) - 1
    sub, half = stage_ref.shape[1:]

    def y_copy(ib, row):
        return pltpu.make_async_copy(acc_ref.at[ib], ys_ref.at[pl.ds(pl.multiple_of(row, sub), sub)], ysem)

    @pl.when(s < meta_ref[0])
    def _():
        n_sub = nsub_ref[s]
        row0 = row0_ref[s]
        wgb_ref[...] = wg_ref[...].astype(BF16)
        wlb_ref[...] = wl_ref[...].astype(BF16)
        wdb_ref[...] = wd_ref[...].astype(BF16)

        def x_copy(ib, slot):
            rows = pl.ds(pl.multiple_of(row0 + ib * sub, sub), sub)
            return pltpu.make_async_copy(xs_ref.at[rows], stage_ref.at[slot], xsem.at[slot])

        @pl.when(j == 0)
        def _():
            x_copy(0, 0).start()

        def load_tokens(ib):
            slot = ib % 2
            x_copy(ib, slot).wait()

            @pl.when(ib + 1 < n_sub)
            def _():
                x_copy(ib + 1, 1 - slot).start()

            lo, hi = _unpack_bf16_pair(stage_ref[slot])
            xbuf_ref[ib, :, :half] = lo
            xbuf_ref[ib, :, half:] = hi
            acc_ref[ib] = jnp.broadcast_to(bd_ref[...], acc_ref.shape[1:])

        def accumulate(ib):
            x = xbuf_ref[ib]
            glu = jnp.dot(x, wgb_ref[...], preferred_element_type=F32) + bg_ref[...]
            lin = jnp.dot(x, wlb_ref[...], preferred_element_type=F32) + bl_ref[...]
            glu = jnp.minimum(glu, SWIGLU_LIMIT)
            lin = jnp.clip(lin, -SWIGLU_LIMIT, SWIGLU_LIMIT)
            act = glu / (1.0 + jnp.exp(-SWIGLU_ALPHA * glu)) * (lin + 1.0)
            acc_ref[ib] += jnp.dot(act.astype(BF16), wdb_ref[...], preferred_element_type=F32)

        def process(ibs):
            @pl.when(j == 0)
            def _():
                for ib in ibs:
                    load_tokens(ib)

            for ib in ibs:
                accumulate(ib)

            @pl.when(j == last_j)
            def _():
                for ib in ibs:
                    y_copy(ib, row0 + ib * sub).start()

        def pair(p, carry):
            process([2 * p, 2 * p + 1])
            return carry

        lax.fori_loop(0, n_sub // 2, pair, 0)

        @pl.when(n_sub % 2 == 1)
        def _():
            process([n_sub - 1])

        @pl.when(j == last_j)
        def _():
            def drain(ib, carry):
                y_copy(ib, row0 + ib * sub).wait()
                return carry

            lax.fori_loop(0, n_sub, drain, 0)

    @pl.when((s == pl.num_programs(0) - 1) & (j == last_j))
    def _():
        n_total = ys_ref.shape[0] // sub
        acc_ref[0] = jnp.zeros(acc_ref.shape[1:], F32)

        def fill(ib, carry):
            y_copy(0, ib * sub).start()
            return carry

        def drain(ib, carry):
            y_copy(0, ib * sub).wait()
            return carry

        lax.fori_loop(meta_ref[1], n_total, fill, 0)
        lax.fori_loop(meta_ref[1], n_total, drain, 0)


def _experts(group_expert, group_row0, group_nsub, meta, xs, w_up, b_up, w_down, b_down):
    R, half = xs.shape
    E, D, ff2 = w_up.shape
    ff = ff2 // 2
    sub = EXPERT_SUB_ROWS
    tf = min(EXPERT_FF_COLS, ff)
    n_f = ff // tf
    grp = lambda s, ge, meta: ge[jnp.minimum(s, meta[0] - 1)]
    col = lambda s, j, meta: jnp.where(s < meta[0], j, n_f - 1)
    return pl.pallas_call(
        _experts_kernel,
        grid_spec=pltpu.PrefetchScalarGridSpec(
            num_scalar_prefetch=4,
            grid=(group_expert.shape[0], n_f),
            in_specs=[
                pl.BlockSpec((None, D, tf), lambda s, j, ge, r0, ns, meta: (grp(s, ge, meta), 0, col(s, j, meta))),
                pl.BlockSpec((None, D, tf), lambda s, j, ge, r0, ns, meta: (grp(s, ge, meta), 0, n_f + col(s, j, meta))),
                pl.BlockSpec((None, 1, tf), lambda s, j, ge, r0, ns, meta: (grp(s, ge, meta), 0, col(s, j, meta))),
                pl.BlockSpec((None, 1, tf), lambda s, j, ge, r0, ns, meta: (grp(s, ge, meta), 0, n_f + col(s, j, meta))),
                pl.BlockSpec((None, tf, D), lambda s, j, ge, r0, ns, meta: (grp(s, ge, meta), col(s, j, meta), 0)),
                pl.BlockSpec((None, 1, D), lambda s, j, ge, r0, ns, meta: (grp(s, ge, meta), 0, 0)),
                pl.BlockSpec(memory_space=pl.ANY),
            ],
            out_specs=pl.BlockSpec(memory_space=pl.ANY),
            scratch_shapes=[
                pltpu.VMEM((EXPERT_GROUP_SUBS, sub, D), BF16),
                pltpu.VMEM((EXPERT_GROUP_SUBS, sub, D), F32),
                pltpu.VMEM((D, tf), BF16),
                pltpu.VMEM((D, tf), BF16),
                pltpu.VMEM((tf, D), BF16),
                pltpu.VMEM((2, sub, half), U32),
                pltpu.SemaphoreType.DMA((2,)),
                pltpu.SemaphoreType.DMA(()),
            ],
        ),
        out_shape=jax.ShapeDtypeStruct((R, D), F32),
        compiler_params=pltpu.CompilerParams(dimension_semantics=("arbitrary", "arbitrary"),
                                             vmem_limit_bytes=EXPERT_VMEM_LIMIT_BYTES),
        name="experts",
    )(group_expert, group_row0, group_nsub, meta, w_up, w_up, b_up.reshape(E, 1, ff2), b_up.reshape(E, 1, ff2),
      w_down, b_down.reshape(E, 1, D), xs)


def _combine_kernel(dst_ref, nxt_ref, gate_ref, h1_ref, lng_ref, lnb_ref, ys_ref, out_ref, buf_ref, sem, *, top_k):
    i = pl.program_id(0)
    tb = h1_ref.shape[0]
    slot = i % 2

    def fetch(slots_ref, into):
        def start(c, carry):
            for u in range(DMA_ISSUE_UNROLL):
                r = c * DMA_ISSUE_UNROLL + u
                for k in range(top_k):
                    row = slots_ref[0, k * tb + r]
                    pltpu.make_async_copy(ys_ref.at[pl.ds(row, 1)], buf_ref.at[into, k, pl.ds(r, 1)],
                                          sem.at[into]).start()
            return carry

        lax.fori_loop(0, tb // DMA_ISSUE_UNROLL, start, 0)

    @pl.when(i == 0)
    def _():
        fetch(dst_ref, 0)

    @pl.when(i + 1 < pl.num_programs(0))
    def _():
        fetch(nxt_ref, 1 - slot)

    for k in range(top_k):
        pltpu.make_async_copy(ys_ref.at[pl.ds(0, tb)], buf_ref.at[slot, k], sem.at[slot]).wait()

    diag = lax.broadcasted_iota(I32, (tb, tb), 0) == lax.broadcasted_iota(I32, (tb, tb), 1)
    ffn = jnp.zeros(h1_ref.shape, F32)
    for k in range(top_k):
        gate = jnp.sum(jnp.where(diag, gate_ref[k:k + 1, :], 0.0), axis=1, keepdims=True)
        ffn = ffn + buf_ref[slot, k] * gate
    out_ref[...] = _layer_norm(DEEPNORM_ALPHA * h1_ref[...] + ffn, lng_ref[...], lnb_ref[...])


def _combine(dst, gates, h1, ln_g, ln_b, ys):
    T, D = h1.shape
    K = dst.shape[0]
    tb = min(COMBINE_ROWS, T)
    slots = _per_block_slots(dst, tb)
    n_blocks = T // tb
    return pl.pallas_call(
        functools.partial(_combine_kernel, top_k=K),
        grid=(n_blocks,),
        in_specs=[
            pl.BlockSpec((None, 1, K * tb), lambda i: (i, 0, 0), memory_space=pltpu.SMEM),
            pl.BlockSpec((None, 1, K * tb), lambda i: (jnp.minimum(i + 1, n_blocks - 1), 0, 0),
                         memory_space=pltpu.SMEM),
            pl.BlockSpec((K, tb), lambda i: (0, i)),
            pl.BlockSpec((tb, D), lambda i: (i, 0)),
            pl.BlockSpec((1, D), lambda i: (0, 0)),
            pl.BlockSpec((1, D), lambda i: (0, 0)),
            pl.BlockSpec(memory_space=pl.ANY),
        ],
        out_specs=pl.BlockSpec((tb, D), lambda i: (i, 0)),
        out_shape=jax.ShapeDtypeStruct((T, D), F32),
        scratch_shapes=[pltpu.VMEM((2, K, tb, D), F32), pltpu.SemaphoreType.DMA((2,))],
        compiler_params=_params("arbitrary"),
        name="combine",
    )(slots, slots, gates, h1, ln_g, ln_b, ys)


def _expert_schedule(counts, n_tokens_routed):
    sub, gb = EXPERT_SUB_ROWS, EXPERT_GROUP_SUBS
    n_experts = counts.shape[0]
    max_subs = n_tokens_routed // sub + n_experts
    max_groups = n_experts + max_subs // gb
    n_sub = (counts + sub - 1) // sub
    sub_end = jnp.cumsum(n_sub)
    sub_start = sub_end - n_sub
    n_grp = (n_sub + gb - 1) // gb
    grp_end = jnp.cumsum(n_grp)
    grp_start = grp_end - n_grp
    s = jnp.arange(max_groups, dtype=I32)
    expert = jnp.minimum(jnp.sum(grp_end[None, :] <= s[:, None], axis=1), n_experts - 1).astype(I32)
    onehot = (expert[:, None] == jnp.arange(n_experts, dtype=I32)[None, :]).astype(I32)
    pick = lambda v: jnp.sum(onehot * v[None, :], axis=1)
    local = s - pick(grp_start)
    row0 = (pick(sub_start) + local * gb) * sub
    nsub = jnp.clip(pick(n_sub) - local * gb, 0, gb)
    meta = jnp.stack([grp_end[-1], sub_end[-1]])
    to_i32 = lambda a: a.astype(I32)
    return to_i32(sub_start * sub), to_i32(expert), to_i32(row0), to_i32(nsub), to_i32(meta), max_subs * sub


def kernel(x, ln0_g, ln0_b, w_in, ret_decay_fwd, ret_decay_bwd, ret_gn_g, w_out, ln1_g, ln1_b, w_router, b_router,
           w_up, b_up, w_down, b_down, ln2_g, ln2_b):
    B, S, D = x.shape
    assert B == 1 and w_in.shape[0] == DEPTH
    T = S
    fw = D // 2
    rw = D - fw
    gw = fw // N_FOURIER_GROUPS
    head_dim = rw // N_RET_HEADS
    E = w_router.shape[-1]
    row = lambda a: a.reshape(1, -1)

    half = head_dim // 2
    inv = ROPE_BASE ** (-jnp.arange(half, dtype=F32) / half)
    ang = jnp.arange(T, dtype=jnp.int32).astype(F32)[:, None] * inv[None, :]

    h, uf, ur = _in_proj(x.reshape(T, D), row(ln0_g), row(ln0_b), w_in[0].astype(BF16), jnp.cos(ang), jnp.sin(ang),
                         head_dim)
    yf = _fourier_mix(uf, gw)
    lg_fwd = -jnp.exp(ret_decay_fwd[0].astype(F32))
    lg_bwd = -jnp.exp(ret_decay_bwd[0].astype(F32))
    o_f, o_b = _retention(ur, lg_fwd, lg_bwd, N_RET_HEADS, head_dim)
    h1, h1p = _out_proj(yf, o_f, o_b, ur, row(ret_gn_g[0]), h, w_out[0].astype(BF16), row(ln1_g[0]), row(ln1_b[0]),
                        N_RET_HEADS)
    topi, gates, rank, cnt = _router(h1, w_router[0].T, b_router[0].reshape(E, 1))

    pad_start, group_expert, group_row0, group_nsub, meta, n_rows = _expert_schedule(cnt[:, 0], T * TOP_K)
    dst = _dst_rows(pad_start, topi, rank)
    xs = _dispatch(dst, h1p, n_rows)
    ys = _experts(group_expert, group_row0, group_nsub, meta, xs, w_up.reshape(w_up.shape[1:]), b_up[0],
                  w_down.reshape(w_down.shape[1:]), b_down[0])
    out = _combine(dst, gates, h1, row(ln2_g[0]), row(ln2_b[0]), ys)
    return out.reshape(B, S, D)
```

```python
import functools

import numpy as np
import jax
import jax.numpy as jnp
from jax import lax
from jax.experimental import pallas as pl
from jax.experimental.pallas import tpu as pltpu

F32 = jnp.float32
BF16 = jnp.bfloat16
I32 = jnp.int32
U32 = jnp.uint32

N_FOURIER_GROUPS = 8
N_RET_HEADS = 4
ROPE_BASE = 10000.0
TOP_K = 4
SWIGLU_LIMIT = 7.0
SWIGLU_ALPHA = 1.702
LN_EPS = 1e-5
GN_EPS = 1e-6
DEPTH = 1
DEEPNORM_ALPHA = (2.0 * DEPTH) ** 0.25

V7X_VMEM_BYTES = 64 * 2**20
VMEM_LIMIT_BYTES = V7X_VMEM_BYTES * 3 // 4
LANES = 128

IN_PROJ_ROWS = 512
IN_PROJ_PART_ROWS = 256
FOURIER_COLS_PER_STEP = 4
RET_CHUNK = 256
OUT_PROJ_ROWS = 512
OUT_PROJ_PART_ROWS = 256
ROUTER_ROWS = 1024
ROUTER_PART_ROWS = 256
DISPATCH_ROWS = 256
EXPERT_SUB_ROWS = 256
EXPERT_GROUP_SUBS = 9
EXPERT_FF_COLS = 256
EXPERT_VMEM_LIMIT_BYTES = V7X_VMEM_BYTES * 7 // 8
COMBINE_ROWS = 128
DMA_ISSUE_UNROLL = 8


def _params(*semantics):
    return pltpu.CompilerParams(dimension_semantics=semantics, vmem_limit_bytes=VMEM_LIMIT_BYTES)


def _layer_norm(x, g, b):
    mu = jnp.mean(x, axis=-1, keepdims=True)
    xc = x - mu
    var = jnp.mean(xc * xc, axis=-1, keepdims=True)
    return xc * lax.rsqrt(var + LN_EPS) * g + b


def _in_proj_kernel(x_ref, g_ref, b_ref, w_ref, cos_ref, sin_ref, h_ref, uf_ref, ur_ref, hb_ref, *, head_dim,
                    part_rows):
    j = pl.program_id(1)
    tm = x_ref.shape[0]
    parts = [slice(r0, r0 + part_rows) for r0 in range(0, tm, part_rows)]

    @pl.when(j == 0)
    def _():
        for rows in parts:
            h = _layer_norm(x_ref[rows, :], g_ref[...], b_ref[...])
            h_ref[rows, :] = h
            hb = h.astype(BF16)
            hb_ref[rows, :] = hb
            uf_ref[rows, :] = jnp.dot(hb, w_ref[...], preferred_element_type=F32).astype(BF16)

    @pl.when((j == 1) | (j == 2))
    def _():
        half = head_dim // 2
        for rows in parts:
            acc = jnp.dot(hb_ref[rows, :], w_ref[...], preferred_element_type=F32)
            cos = cos_ref[rows, :]
            sin = sin_ref[rows, :]
            for lo in range(0, acc.shape[1], head_dim):
                x1 = acc[:, lo:lo + half]
                x2 = acc[:, lo + half:lo + head_dim]
                ur_ref[rows, lo:lo + half] = (x1 * cos - x2 * sin).astype(BF16)
                ur_ref[rows, lo + half:lo + head_dim] = (x2 * cos + x1 * sin).astype(BF16)

    @pl.when(j >= 3)
    def _():
        for rows in parts:
            ur_ref[rows, :] = jnp.dot(hb_ref[rows, :], w_ref[...], preferred_element_type=F32).astype(BF16)


def _in_proj(x2, ln_g, ln_b, w_bf, cos, sin, head_dim):
    T, D = x2.shape
    tn = D // 2
    n_col = w_bf.shape[1] // tn
    assert n_col == 5 and w_bf.shape[1] == 5 * tn
    tm = min(IN_PROJ_ROWS, T)
    half = head_dim // 2
    return pl.pallas_call(
        functools.partial(_in_proj_kernel, head_dim=head_dim, part_rows=min(IN_PROJ_PART_ROWS, tm)),
        grid=(T // tm, n_col),
        in_specs=[
            pl.BlockSpec((tm, D), lambda i, j: (i, 0)),
            pl.BlockSpec((1, D), lambda i, j: (0, 0)),
            pl.BlockSpec((1, D), lambda i, j: (0, 0)),
            pl.BlockSpec((D, tn), lambda i, j: (0, j)),
            pl.BlockSpec((tm, half), lambda i, j: (i, 0)),
            pl.BlockSpec((tm, half), lambda i, j: (i, 0)),
        ],
        out_specs=[
            pl.BlockSpec((tm, D), lambda i, j: (i, 0)),
            pl.BlockSpec((tm, tn), lambda i, j: (i, 0)),
            pl.BlockSpec((tm, tn), lambda i, j: (i, jnp.maximum(j - 1, 0))),
        ],
        out_shape=[
            jax.ShapeDtypeStruct((T, D), F32),
            jax.ShapeDtypeStruct((T, tn), BF16),
            jax.ShapeDtypeStruct((T, 4 * tn), BF16),
        ],
        scratch_shapes=[pltpu.VMEM((tm, D), BF16)],
        compiler_params=_params("parallel", "arbitrary"),
        name="in_proj",
    )(x2, ln_g, ln_b, w_bf, cos, sin)


def _fourier_tables(n1, n2, gw):
    def cos_sin(n, r, c):
        ang = 2.0 * np.pi * ((np.outer(r, c) % n).astype(np.float64)) / n
        return np.cos(ang), np.sin(ang)

    cg, sg = cos_sin(gw, np.arange(gw), np.arange(gw))
    c1, s1 = cos_sin(n1, np.arange(n1), np.arange(n1))
    c3, s3 = cos_sin(n2, np.arange(n2), np.arange(n2))
    tc, ts = cos_sin(n1 * n2, np.arange(n1), np.arange(n2))
    cs = np.concatenate([cg, sg], axis=1)
    m1 = np.block([[c1, -s1], [-s1, -c1]])
    m3 = np.concatenate([c3, s3], axis=1)
    as32 = lambda a: jnp.asarray(a.astype(np.float32))
    return as32(cs).astype(BF16), as32(m1).astype(BF16), as32(m3).astype(BF16), as32(tc), as32(ts)


def _fourier_a_kernel(uf_ref, cs_ref, m1_ref, twc_ref, tws_ref, zz_ref, *, nb, gw):
    bb = pl.program_id(0)
    n1 = uf_ref.shape[0]
    width = uf_ref.shape[1] // nb
    cs = cs_ref[...]
    m1 = m1_ref[...]
    lane = lax.broadcasted_iota(I32, twc_ref.shape, 1)
    for t in range(nb):
        x = uf_ref[:, t * width:(t + 1) * width]
        a_parts, b_parts = [], []
        for lo in range(0, width, gw):
            ab = jnp.dot(x[:, lo:lo + gw], cs, preferred_element_type=F32)
            a_parts.append(ab[:, :gw])
            b_parts.append(ab[:, gw:])
        v = jnp.concatenate([jnp.concatenate(a_parts, axis=1), jnp.concatenate(b_parts, axis=1)], axis=0)
        z = jnp.dot(m1, v.astype(BF16), preferred_element_type=F32)
        zr, zi = z[:n1], z[n1:]
        sel = lane == bb * nb + t
        tc = jnp.sum(jnp.where(sel, twc_ref[...], 0.0), axis=1, keepdims=True)
        ts = jnp.sum(jnp.where(sel, tws_ref[...], 0.0), axis=1, keepdims=True)
        base = t * 2 * width
        zz_ref[:, base:base + width] = (zr * tc + zi * ts).astype(BF16)
        zz_ref[:, base + width:base + 2 * width] = (zi * tc - zr * ts).astype(BF16)


def _fourier_b_kernel(zz_ref, m3_ref, yf_ref, *, scale):
    nc = zz_ref.shape[0]
    width = zz_ref.shape[2] // 2
    m3 = m3_ref[...]
    for t in range(nc):
        z = zz_ref[t]
        stacked = jnp.concatenate([z[:, :width], z[:, width:]], axis=0)
        x = jnp.dot(m3, stacked, preferred_element_type=F32)
        yf_ref[:, t * width:(t + 1) * width] = (x * scale).astype(BF16)


def _fourier_mix(uf, gw):
    T, F = uf.shape
    n1 = min(LANES, T)
    n2 = T // n1
    assert n1 * n2 == T
    cs, m1, m3, twc, tws = _fourier_tables(n1, n2, gw)
    nb = min(FOURIER_COLS_PER_STEP, n2)
    nc = min(FOURIER_COLS_PER_STEP, n1)
    zz = pl.pallas_call(
        functools.partial(_fourier_a_kernel, nb=nb, gw=gw),
        grid=(n2 // nb,),
        in_specs=[
            pl.BlockSpec((n1, nb * F), lambda b: (0, b)),
            pl.BlockSpec(cs.shape, lambda b: (0, 0)),
            pl.BlockSpec(m1.shape, lambda b: (0, 0)),
            pl.BlockSpec(twc.shape, lambda b: (0, 0)),
            pl.BlockSpec(tws.shape, lambda b: (0, 0)),
        ],
        out_specs=pl.BlockSpec((n1, nb * 2 * F), lambda b: (0, b)),
        out_shape=jax.ShapeDtypeStruct((n1, n2 * 2 * F), BF16),
        compiler_params=_params("parallel"),
        name="fourier_a",
    )(uf.reshape(n1, n2 * F), cs, m1, twc, tws)
    yf = pl.pallas_call(
        functools.partial(_fourier_b_kernel, scale=float(1.0 / np.sqrt(T * gw))),
        grid=(n1 // nc,),
        in_specs=[
            pl.BlockSpec((nc, n2, 2 * F), lambda c: (c, 0, 0)),
            pl.BlockSpec(m3.shape, lambda c: (0, 0)),
        ],
        out_specs=pl.BlockSpec((n2, nc * F), lambda c: (0, c)),
        out_shape=jax.ShapeDtypeStruct((n2, n1 * F), BF16),
        compiler_params=_params("parallel"),
        name="fourier_b",
    )(zz.reshape(n1, n2, 2 * F), m3)
    return yf.reshape(T, F)


_TAB_K_TO_END_F, _TAB_Q_FROM_START_F, _TAB_K_TO_END_B, _TAB_Q_FROM_START_B, _TAB_CHUNK_F, _TAB_CHUNK_B = range(6)


def _retention_kernel(lgf_ref, lgb_ref, qf_ref, kf_ref, vf_ref, qb_ref, kb_ref, vb_ref, of_ref, ob_ref,
                      sf_ref, sb_ref, dmat_ref, tab_ref, *, kscale, n_heads):
    n = pl.program_id(0)
    C = qf_ref.shape[0]
    hd = qf_ref.shape[1] // n_heads

    @pl.when(n == 0)
    def _():
        r = lax.broadcasted_iota(I32, (C, C), 0).astype(F32)
        c = lax.broadcasted_iota(I32, (C, C), 1).astype(F32)
        diff = r - c
        rr = lax.broadcasted_iota(I32, (C, hd), 0).astype(F32)
        for head in range(n_heads):
            lgf = lgf_ref[head]
            lgb = lgb_ref[head]
            dmat_ref[head] = kscale * jnp.where(diff >= 0.0, jnp.exp(lgf * jnp.maximum(diff, 0.0)),
                                                jnp.exp(lgb * jnp.maximum(-diff, 0.0)))
            tab_ref[head, _TAB_K_TO_END_F] = kscale * jnp.exp(lgf * (C - 1.0 - rr))
            tab_ref[head, _TAB_Q_FROM_START_F] = jnp.exp(lgf * (rr + 1.0))
            tab_ref[head, _TAB_K_TO_END_B] = kscale * jnp.exp(lgb * rr)
            tab_ref[head, _TAB_Q_FROM_START_B] = jnp.exp(lgb * (C - rr))
            tab_ref[head, _TAB_CHUNK_F] = jnp.exp(jnp.full((C, hd), lgf * C, F32))
            tab_ref[head, _TAB_CHUNK_B] = jnp.exp(jnp.full((C, hd), lgb * C, F32))
        sf_ref[...] = jnp.zeros_like(sf_ref)
        sb_ref[...] = jnp.zeros_like(sb_ref)

    nt = (((1,), (1,)), ((), ()))
    tn = (((0,), (0,)), ((), ()))

    for head in range(n_heads):
        cols = slice(head * hd, (head + 1) * hd)
        q = qf_ref[:, cols]
        k = kf_ref[:, cols]
        v = vf_ref[:, cols]
        scores = lax.dot_general(q, k, nt, preferred_element_type=F32) * dmat_ref[head]
        intra = jnp.dot(scores.astype(BF16), v, preferred_element_type=F32)
        cross = (jnp.dot(q, sf_ref[head].astype(BF16), preferred_element_type=F32)
                 * tab_ref[head, _TAB_Q_FROM_START_F])
        of_ref[:, cols] = intra + cross
        v_dec = (v.astype(F32) * tab_ref[head, _TAB_K_TO_END_F]).astype(BF16)
        sf_ref[head] = (sf_ref[head] * tab_ref[head, _TAB_CHUNK_F]
                        + lax.dot_general(k, v_dec, tn, preferred_element_type=F32))

        q = qb_ref[:, cols]
        k = kb_ref[:, cols]
        v = vb_ref[:, cols]
        ob_ref[:, cols] = (jnp.dot(q, sb_ref[head].astype(BF16), preferred_element_type=F32)
                           * tab_ref[head, _TAB_Q_FROM_START_B])
        v_dec = (v.astype(F32) * tab_ref[head, _TAB_K_TO_END_B]).astype(BF16)
        sb_ref[head] = (sb_ref[head] * tab_ref[head, _TAB_CHUNK_B]
                        + lax.dot_general(k, v_dec, tn, preferred_element_type=F32))


def _retention(ur, lg_fwd, lg_bwd, n_heads, head_dim):
    T = ur.shape[0]
    C = min(RET_CHUNK, T)
    assert C == head_dim
    N = T // C
    H = n_heads
    rw = H * head_dim
    fwd = lambda col: (lambda n, *_: (n, col))
    bwd = lambda col: (lambda n, *_: (N - 1 - n, col))
    blk = lambda imap: pl.BlockSpec((C, rw), imap)
    return pl.pallas_call(
        functools.partial(_retention_kernel, kscale=float(head_dim) ** -0.5, n_heads=H),
        grid_spec=pltpu.PrefetchScalarGridSpec(
            num_scalar_prefetch=2,
            grid=(N,),
            in_specs=[blk(fwd(0)), blk(fwd(1)), blk(fwd(2)), blk(bwd(0)), blk(bwd(1)), blk(bwd(2))],
            out_specs=[blk(fwd(0)), blk(bwd(0))],
            scratch_shapes=[
                pltpu.VMEM((H, head_dim, head_dim), F32),
                pltpu.VMEM((H, head_dim, head_dim), F32),
                pltpu.VMEM((H, C, C), F32),
                pltpu.VMEM((H, 6, C, head_dim), F32),
            ],
        ),
        out_shape=[jax.ShapeDtypeStruct((T, rw), F32)] * 2,
        compiler_params=_params("arbitrary"),
        name="retention",
    )(lg_fwd, lg_bwd, ur, ur, ur, ur, ur, ur)


def _pack_bf16_pair(lo, hi):
    lo_bits = lax.bitcast_convert_type(lo.astype(BF16).astype(F32), U32)
    hi_bits = lax.bitcast_convert_type(hi.astype(BF16).astype(F32), U32)
    return (hi_bits & jnp.uint32(0xFFFF0000)) | (lo_bits >> 16)


def _unpack_bf16_pair(words):
    lo = lax.bitcast_convert_type(words << 16, F32).astype(BF16)
    hi = lax.bitcast_convert_type(words & jnp.uint32(0xFFFF0000), F32).astype(BF16)
    return lo, hi


def _out_proj_kernel(yf_ref, of_ref, ob_ref, g_ref, gng_ref, h_ref, wo_ref, lng_ref, lnb_ref, h1_ref, h1p_ref, *,
                     n_heads, part_rows):
    tm, rw = of_ref.shape
    hd = rw // n_heads
    fw = yf_ref.shape[1]
    D = h_ref.shape[1]
    for r0 in range(0, tm, part_rows):
        rows = slice(r0, r0 + part_rows)
        o = of_ref[rows, :] + ob_ref[rows, :]
        parts = []
        for lo in range(0, rw, hd):
            oh = o[:, lo:lo + hd]
            mu = jnp.mean(oh, axis=-1, keepdims=True)
            d = oh - mu
            var = jnp.mean(d * d, axis=-1, keepdims=True)
            parts.append(d * lax.rsqrt(var + GN_EPS))
        on = jnp.concatenate(parts, axis=-1) * gng_ref[...]
        g = g_ref[rows, :].astype(F32)
        yr = (g / (1.0 + jnp.exp(-g))) * on

        mix = (jnp.dot(yf_ref[rows, :], wo_ref[:fw, :], preferred_element_type=F32)
               + jnp.dot(yr.astype(BF16), wo_ref[fw:, :], preferred_element_type=F32))
        h1 = _layer_norm(DEEPNORM_ALPHA * h_ref[rows, :] + mix, lng_ref[...], lnb_ref[...])
        h1_ref[rows, :] = h1
        h1p_ref[rows, :] = _pack_bf16_pair(h1[:, :D // 2], h1[:, D // 2:])


def _out_proj(yf, o_f, o_b, ur, gn_g, h, wo_bf, ln_g, ln_b, n_heads):
    T, D = h.shape
    fw = yf.shape[1]
    rw = o_f.shape[1]
    tm = min(OUT_PROJ_ROWS, T)
    gate_col = ur.shape[1] // rw - 1
    row = lambda w: pl.BlockSpec((tm, w), lambda i: (i, 0))
    full = lambda a: pl.BlockSpec(a.shape, lambda i: (0,) * a.ndim)
    return pl.pallas_call(
        functools.partial(_out_proj_kernel, n_heads=n_heads, part_rows=min(OUT_PROJ_PART_ROWS, tm)),
        grid=(T // tm,),
        in_specs=[row(fw), row(rw), row(rw), pl.BlockSpec((tm, rw), lambda i: (i, gate_col)), full(gn_g), row(D),
                  full(wo_bf), full(ln_g), full(ln_b)],
        out_specs=[row(D), row(D // 2)],
        out_shape=[jax.ShapeDtypeStruct((T, D), F32), jax.ShapeDtypeStruct((T, D // 2), U32)],
        compiler_params=_params("parallel"),
        name="out_proj",
    )(yf, o_f, o_b, ur, gn_g, h, wo_bf, ln_g, ln_b)


def _router_kernel(h1_ref, wr_ref, wrl_ref, br_ref, topi_ref, gate_ref, rank_ref, cnt_ref, carry_ref, *, top_k,
                   part_rows):
    i = pl.program_id(0)
    tm = h1_ref.shape[0]
    E = wr_ref.shape[0]
    tp = part_rows

    @pl.when(i == 0)
    def _():
        carry_ref[...] = jnp.zeros_like(carry_ref)

    eidx = lax.broadcasted_iota(I32, (E, tp), 0)
    earlier = lax.broadcasted_iota(I32, (tp, tp), 0) < lax.broadcasted_iota(I32, (tp, tp), 1)
    earlier = jnp.where(earlier, 1.0, 0.0).astype(BF16)
    carry = carry_ref[...]

    for r0 in range(0, tm, tp):
        rows = slice(r0, r0 + tp)
        h = h1_ref[rows, :]
        h_hi = h.astype(BF16)
        h_lo = (h - h_hi.astype(F32)).astype(BF16)
        nt = (((1,), (1,)), ((), ()))
        logits = (lax.dot_general(wr_ref[...], h_hi, nt, preferred_element_type=F32)
                  + lax.dot_general(wr_ref[...], h_lo, nt, preferred_element_type=F32)
                  + lax.dot_general(wrl_ref[...], h_hi, nt, preferred_element_type=F32)) + br_ref[...]
        cur = logits
        sel_idx, sel_val = [], []
        for _ in range(top_k):
            m = jnp.max(cur, axis=0, keepdims=True)
            idx = jnp.min(jnp.where(cur == m, eidx, E), axis=0, keepdims=True)
            sel_idx.append(idx)
            sel_val.append(m)
            cur = jnp.where(eidx == idx, -jnp.inf, cur)
        ex = [jnp.exp(val - sel_val[0]) for val in sel_val]
        denom = ex[0]
        for e_r in ex[1:]:
            denom = denom + e_r
        topi_ref[:, rows] = jnp.concatenate(sel_idx, axis=0)
        gate_ref[:, rows] = jnp.concatenate([e_r / denom for e_r in ex], axis=0)

        member = jnp.zeros((E, tp), F32)
        for idx in sel_idx:
            member = member + jnp.where(eidx == idx, 1.0, 0.0)
        rank_full = jnp.dot(member.astype(BF16), earlier, preferred_element_type=F32) + carry
        ranks = [jnp.sum(jnp.where(eidx == idx, rank_full, 0.0), axis=0, keepdims=True) for idx in sel_idx]
        rank_ref[:, rows] = jnp.concatenate(ranks, axis=0).astype(I32)
        carry = carry + jnp.sum(member, axis=1, keepdims=True)

    carry_ref[...] = carry
    cnt_ref[...] = jnp.broadcast_to(carry, cnt_ref.shape).astype(I32)


def _router(h1, wr_t, br):
    T, D = h1.shape
    E = wr_t.shape[0]
    wr_hi = wr_t.astype(BF16)
    wr_lo = (wr_t - wr_hi.astype(F32)).astype(BF16)
    tm = min(ROUTER_ROWS, T)
    full = lambda a: pl.BlockSpec(a.shape, lambda i: (0,) * a.ndim)
    tok = lambda: pl.BlockSpec((TOP_K, tm), lambda i: (0, i))
    return pl.pallas_call(
        functools.partial(_router_kernel, top_k=TOP_K, part_rows=min(ROUTER_PART_ROWS, tm)),
        grid=(T // tm,),
        in_specs=[pl.BlockSpec((tm, D), lambda i: (i, 0)), full(wr_hi), full(wr_lo), full(br)],
        out_specs=[tok(), tok(), tok(), pl.BlockSpec((E, LANES), lambda i: (0, 0))],
        out_shape=[
            jax.ShapeDtypeStruct((TOP_K, T), I32),
            jax.ShapeDtypeStruct((TOP_K, T), F32),
            jax.ShapeDtypeStruct((TOP_K, T), I32),
            jax.ShapeDtypeStruct((E, LANES), I32),
        ],
        scratch_shapes=[pltpu.VMEM((E, 1), F32)],
        compiler_params=_params("arbitrary"),
        name="router",
    )(h1, wr_hi, wr_lo, br)


def _dst_kernel(start_ref, topi_ref, rank_ref, dst_ref, *, n_experts):
    topi = topi_ref[...]
    dst = rank_ref[...]
    for e in range(n_experts):
        dst = dst + jnp.where(topi == e, start_ref[e], 0)
    dst_ref[...] = dst


def _dst_rows(pad_start, topi, rank):
    spec = pl.BlockSpec(topi.shape, lambda i, *_: (0, 0))
    return pl.pallas_call(
        functools.partial(_dst_kernel, n_experts=pad_start.shape[0]),
        grid_spec=pltpu.PrefetchScalarGridSpec(num_scalar_prefetch=1, grid=(1,), in_specs=[spec, spec], out_specs=spec),
        out_shape=jax.ShapeDtypeStruct(topi.shape, I32),
        compiler_params=_params("arbitrary"),
        name="dst",
    )(pad_start, topi, rank)


def _per_block_slots(dst, tb):
    K, T = dst.shape
    return dst.reshape(K, T // tb, tb).transpose(1, 0, 2).reshape(T // tb, 1, K * tb)


def _dispatch_kernel(dst_ref, h1p_ref, xs_in_ref, xs_ref, sem, *, top_k):
    del xs_in_ref
    tb = h1p_ref.shape[0]

    def start(c, carry):
        for u in range(DMA_ISSUE_UNROLL):
            r = c * DMA_ISSUE_UNROLL + u
            for k in range(top_k):
                row = dst_ref[0, k * tb + r]
                pltpu.make_async_copy(h1p_ref.at[pl.ds(r, 1)], xs_ref.at[pl.ds(row, 1)], sem).start()
        return carry

    lax.fori_loop(0, tb // DMA_ISSUE_UNROLL, start, 0)
    for k in range(top_k):
        pltpu.make_async_copy(h1p_ref, xs_ref.at[pl.ds(0, tb)], sem).wait()


def _dispatch(dst, h1p, n_rows):
    T, W = h1p.shape
    tb = min(DISPATCH_ROWS, T)
    slots = _per_block_slots(dst, tb)
    xs0 = jnp.zeros((n_rows, W), U32)
    return pl.pallas_call(
        functools.partial(_dispatch_kernel, top_k=dst.shape[0]),
        grid=(T // tb,),
        in_specs=[
            pl.BlockSpec((None, 1, slots.shape[2]), lambda i: (i, 0, 0), memory_space=pltpu.SMEM),
            pl.BlockSpec((tb, W), lambda i: (i, 0)),
            pl.BlockSpec(memory_space=pl.ANY),
        ],
        out_specs=pl.BlockSpec(memory_space=pl.ANY),
        out_shape=jax.ShapeDtypeStruct((n_rows, W), U32),
        scratch_shapes=[pltpu.SemaphoreType.DMA(())],
        input_output_aliases={2: 0},
        compiler_params=_params("arbitrary"),
        name="dispatch",
    )(slots, h1p, xs0)


def _experts_kernel(ge_ref, row0_ref, nsub_ref, meta_ref, bu_ref, bd_ref, wup_ref, wdn_ref, xs_ref, ys_ref,
                    xbuf_ref, acc_ref, wgf_ref, wlf_ref, wdf_ref, wgb_ref, wlb_ref, wdb_ref, stage_ref,
                    wsem, xsem, ysem, *, n_f):
    s = pl.program_id(0)
    n_groups = meta_ref[0]
    sub, half = stage_ref.shape[1:]
    tf = wdf_ref.shape[0]
    ff = n_f * tf

    def y_copy(ib, row):
        return pltpu.make_async_copy(acc_ref.at[ib], ys_ref.at[pl.ds(pl.multiple_of(row, sub), sub)], ysem)

    def weight_copies(e, j):
        glu_cols = pl.ds(pl.multiple_of(j * tf, tf), tf)
        lin_cols = pl.ds(pl.multiple_of(ff + j * tf, tf), tf)
        return (pltpu.make_async_copy(wup_ref.at[e, :, glu_cols], wgf_ref, wsem.at[0]),
                pltpu.make_async_copy(wup_ref.at[e, :, lin_cols], wlf_ref, wsem.at[1]),
                pltpu.make_async_copy(wdn_ref.at[e, glu_cols, :], wdf_ref, wsem.at[2]))

    def cast_weights(slot):
        wgb_ref[slot] = wgf_ref[...].astype(BF16)
        wlb_ref[slot] = wlf_ref[...].astype(BF16)
        wdb_ref[slot] = wdf_ref[...].astype(BF16)

    @pl.when(s < n_groups)
    def _():
        e = ge_ref[s]
        e_after = ge_ref[jnp.minimum(s + 1, n_groups - 1)]
        n_sub = nsub_ref[s]
        row0 = row0_ref[s]

        def x_copy(ib, slot):
            rows = pl.ds(pl.multiple_of(row0 + ib * sub, sub), sub)
            return pltpu.make_async_copy(xs_ref.at[rows], stage_ref.at[slot], xsem.at[slot])

        x_copy(0, 0).start()

        @pl.when(s == 0)
        def _():
            for copy in weight_copies(e, 0):
                copy.start()
            for copy in weight_copies(e, 0):
                copy.wait()
            cast_weights(0)

        def item(j, carry):
            slot = j % 2
            is_last = j == n_f - 1
            next_j = jnp.where(is_last, 0, j + 1)
            next_e = jnp.where(is_last, e_after, e)
            for copy in weight_copies(next_e, next_j):
                copy.start()

            def load_tokens(ib):
                xslot = ib % 2
                x_copy(ib, xslot).wait()

                @pl.when(ib + 1 < n_sub)
                def _():
                    x_copy(ib + 1, 1 - xslot).start()

                lo, hi = _unpack_bf16_pair(stage_ref[xslot])
                xbuf_ref[ib, :, :half] = lo
                xbuf_ref[ib, :, half:] = hi
                acc_ref[ib] = jnp.broadcast_to(bd_ref[...], acc_ref.shape[1:])

            def accumulate(ib):
                x = xbuf_ref[ib]
                glu = jnp.dot(x, wgb_ref[slot], preferred_element_type=F32) + bu_ref[pl.ds(j, 1), :]
                lin = jnp.dot(x, wlb_ref[slot], preferred_element_type=F32) + bu_ref[pl.ds(n_f + j, 1), :]
                glu = jnp.minimum(glu, SWIGLU_LIMIT)
                lin = jnp.clip(lin, -SWIGLU_LIMIT, SWIGLU_LIMIT)
                act = glu / (1.0 + jnp.exp(-SWIGLU_ALPHA * glu)) * (lin + 1.0)
                acc_ref[ib] += jnp.dot(act.astype(BF16), wdb_ref[slot], preferred_element_type=F32)

            def process(ibs, with_next_weights):
                @pl.when(j == 0)
                def _():
                    for ib in ibs:
                        load_tokens(ib)

                if with_next_weights:
                    for copy in weight_copies(next_e, next_j):
                        copy.wait()
                    cast_weights(1 - slot)
                for ib in ibs:
                    accumulate(ib)

                @pl.when(is_last)
                def _():
                    for ib in ibs:
                        y_copy(ib, row0 + ib * sub).start()

            def pair(p, carry):
                process([2 * p, 2 * p + 1], False)
                return carry

            lax.fori_loop(0, (n_sub - 1) // 2, pair, 0)

            @pl.when(n_sub % 2 == 0)
            def _():
                process([n_sub - 2, n_sub - 1], True)

            @pl.when(n_sub % 2 == 1)
            def _():
                process([n_sub - 1], True)

            return carry

        lax.fori_loop(0, n_f, item, 0)

        def drain(ib, carry):
            y_copy(ib, row0 + ib * sub).wait()
            return carry

        lax.fori_loop(0, n_sub, drain, 0)

    @pl.when(s == pl.num_programs(0) - 1)
    def _():
        n_total = ys_ref.shape[0] // sub
        acc_ref[0] = jnp.zeros(acc_ref.shape[1:], F32)

        def fill(ib, carry):
            y_copy(0, ib * sub).start()
            return carry

        def drain(ib, carry):
            y_copy(0, ib * sub).wait()
            return carry

        lax.fori_loop(meta_ref[1], n_total, fill, 0)
        lax.fori_loop(meta_ref[1], n_total, drain, 0)


def _experts(group_expert, group_row0, group_nsub, meta, xs, w_up, b_up, w_down, b_down):
    R, half = xs.shape
    E, D, ff2 = w_up.shape
    ff = ff2 // 2
    sub = EXPERT_SUB_ROWS
    gb = EXPERT_GROUP_SUBS
    tf = min(EXPERT_FF_COLS, ff)
    n_f = ff // tf
    assert n_f % 2 == 0
    grp = lambda s, ge, r0, ns, meta: (ge[jnp.minimum(s, meta[0] - 1)], 0, 0)
    return pl.pallas_call(
        functools.partial(_experts_kernel, n_f=n_f),
        grid_spec=pltpu.PrefetchScalarGridSpec(
            num_scalar_prefetch=4,
            grid=(group_expert.shape[0],),
            in_specs=[
                pl.BlockSpec((None, 2 * n_f, tf), grp),
                pl.BlockSpec((None, 1, D), grp),
                pl.BlockSpec(memory_space=pl.ANY),
                pl.BlockSpec(memory_space=pl.ANY),
                pl.BlockSpec(memory_space=pl.ANY),
            ],
            out_specs=pl.BlockSpec(memory_space=pl.ANY),
            scratch_shapes=[
                pltpu.VMEM((gb, sub, D), BF16),
                pltpu.VMEM((gb, sub, D), F32),
                pltpu.VMEM((D, tf), F32),
                pltpu.VMEM((D, tf), F32),
                pltpu.VMEM((tf, D), F32),
                pltpu.VMEM((2, D, tf), BF16),
                pltpu.VMEM((2, D, tf), BF16),
                pltpu.VMEM((2, tf, D), BF16),
                pltpu.VMEM((2, sub, half), U32),
                pltpu.SemaphoreType.DMA((3,)),
                pltpu.SemaphoreType.DMA((2,)),
                pltpu.SemaphoreType.DMA(()),
            ],
        ),
        out_shape=jax.ShapeDtypeStruct((R, D), F32),
        compiler_params=pltpu.CompilerParams(dimension_semantics=("arbitrary",),
                                             vmem_limit_bytes=EXPERT_VMEM_LIMIT_BYTES),
        name="experts",
    )(group_expert, group_row0, group_nsub, meta, b_up.reshape(E, 2 * n_f, tf), b_down.reshape(E, 1, D),
      w_up, w_down, xs)


def _combine_kernel(dst_ref, nxt_ref, gate_ref, h1_ref, lng_ref, lnb_ref, ys_ref, out_ref, buf_ref, sem, *, top_k):
    i = pl.program_id(0)
    tb = h1_ref.shape[0]
    slot = i % 2

    def fetch(slots_ref, into):
        def start(c, carry):
            for u in range(DMA_ISSUE_UNROLL):
                r = c * DMA_ISSUE_UNROLL + u
                for k in range(top_k):
                    row = slots_ref[0, k * tb + r]
                    pltpu.make_async_copy(ys_ref.at[pl.ds(row, 1)], buf_ref.at[into, k, pl.ds(r, 1)],
                                          sem.at[into]).start()
            return carry

        lax.fori_loop(0, tb // DMA_ISSUE_UNROLL, start, 0)

    @pl.when(i == 0)
    def _():
        fetch(dst_ref, 0)

    @pl.when(i + 1 < pl.num_programs(0))
    def _():
        fetch(nxt_ref, 1 - slot)

    for k in range(top_k):
        pltpu.make_async_copy(ys_ref.at[pl.ds(0, tb)], buf_ref.at[slot, k], sem.at[slot]).wait()

    diag = lax.broadcasted_iota(I32, (tb, tb), 0) == lax.broadcasted_iota(I32, (tb, tb), 1)
    ffn = jnp.zeros(h1_ref.shape, F32)
    for k in range(top_k):
        gate = jnp.sum(jnp.where(diag, gate_ref[k:k + 1, :], 0.0), axis=1, keepdims=True)
        ffn = ffn + buf_ref[slot, k] * gate
    out_ref[...] = _layer_norm(DEEPNORM_ALPHA * h1_ref[...] + ffn, lng_ref[...], lnb_ref[...])


def _combine(dst, gates, h1, ln_g, ln_b, ys):
    T, D = h1.shape
    K = dst.shape[0]
    tb = min(COMBINE_ROWS, T)
    slots = _per_block_slots(dst, tb)
    n_blocks = T // tb
    return pl.pallas_call(
        functools.partial(_combine_kernel, top_k=K),
        grid=(n_blocks,),
        in_specs=[
            pl.BlockSpec((None, 1, K * tb), lambda i: (i, 0, 0), memory_space=pltpu.SMEM),
            pl.BlockSpec((None, 1, K * tb), lambda i: (jnp.minimum(i + 1, n_blocks - 1), 0, 0),
                         memory_space=pltpu.SMEM),
            pl.BlockSpec((K, tb), lambda i: (0, i)),
            pl.BlockSpec((tb, D), lambda i: (i, 0)),
            pl.BlockSpec((1, D), lambda i: (0, 0)),
            pl.BlockSpec((1, D), lambda i: (0, 0)),
            pl.BlockSpec(memory_space=pl.ANY),
        ],
        out_specs=pl.BlockSpec((tb, D), lambda i: (i, 0)),
        out_shape=jax.ShapeDtypeStruct((T, D), F32),
        scratch_shapes=[pltpu.VMEM((2, K, tb, D), F32), pltpu.SemaphoreType.DMA((2,))],
        compiler_params=_params("arbitrary"),
        name="combine",
    )(slots, slots, gates, h1, ln_g, ln_b, ys)


def _expert_schedule(counts, n_tokens_routed):
    sub, gb = EXPERT_SUB_ROWS, EXPERT_GROUP_SUBS
    n_experts = counts.shape[0]
    max_subs = n_tokens_routed // sub + n_experts
    max_groups = n_experts + max_subs // gb
    n_sub = (counts + sub - 1) // sub
    sub_end = jnp.cumsum(n_sub)
    sub_start = sub_end - n_sub
    n_grp = (n_sub + gb - 1) // gb
    grp_end = jnp.cumsum(n_grp)
    grp_start = grp_end - n_grp
    s = jnp.arange(max_groups, dtype=I32)
    expert = jnp.minimum(jnp.sum(grp_end[None, :] <= s[:, None], axis=1), n_experts - 1).astype(I32)
    onehot = (expert[:, None] == jnp.arange(n_experts, dtype=I32)[None, :]).astype(I32)
    pick = lambda v: jnp.sum(onehot * v[None, :], axis=1)
    local = s - pick(grp_start)
    row0 = (pick(sub_start) + local * gb) * sub
    nsub = jnp.clip(pick(n_sub) - local * gb, 0, gb)
    meta = jnp.stack([grp_end[-1], sub_end[-1]])
    to_i32 = lambda a: a.astype(I32)
    return to_i32(sub_start * sub), to_i32(expert), to_i32(row0), to_i32(nsub), to_i32(meta), max_subs * sub


def kernel(x, ln0_g, ln0_b, w_in, ret_decay_fwd, ret_decay_bwd, ret_gn_g, w_out, ln1_g, ln1_b, w_router, b_router,
           w_up, b_up, w_down, b_down, ln2_g, ln2_b):
    B, S, D = x.shape
    assert B == 1 and w_in.shape[0] == DEPTH
    T = S
    fw = D // 2
    rw = D - fw
    gw = fw // N_FOURIER_GROUPS
    head_dim = rw // N_RET_HEADS
    E = w_router.shape[-1]
    row = lambda a: a.reshape(1, -1)

    half = head_dim // 2
    inv = ROPE_BASE ** (-jnp.arange(half, dtype=F32) / half)
    ang = jnp.arange(T, dtype=jnp.int32).astype(F32)[:, None] * inv[None, :]

    h, uf, ur = _in_proj(x.reshape(T, D), row(ln0_g), row(ln0_b), w_in[0].astype(BF16), jnp.cos(ang), jnp.sin(ang),
                         head_dim)
    yf = _fourier_mix(uf, gw)
    lg_fwd = -jnp.exp(ret_decay_fwd[0].astype(F32))
    lg_bwd = -jnp.exp(ret_decay_bwd[0].astype(F32))
    o_f, o_b = _retention(ur, lg_fwd, lg_bwd, N_RET_HEADS, head_dim)
    h1, h1p = _out_proj(yf, o_f, o_b, ur, row(ret_gn_g[0]), h, w_out[0].astype(BF16), row(ln1_g[0]), row(ln1_b[0]),
                        N_RET_HEADS)
    topi, gates, rank, cnt = _router(h1, w_router[0].T, b_router[0].reshape(E, 1))

    pad_start, group_expert, group_row0, group_nsub, meta, n_rows = _expert_schedule(cnt[:, 0], T * TOP_K)
    dst = _dst_rows(pad_start, topi, rank)
    xs = _dispatch(dst, h1p, n_rows)
    ys = _experts(group_expert, group_row0, group_nsub, meta, xs, w_up.reshape(w_up.shape[1:]), b_up[0],
                  w_down.reshape(w_down.shape[1:]), b_down[0])
    out = _combine(dst, gates, h1, row(ln2_g[0]), row(ln2_b[0]), ys)
    return out.reshape(B, S, D)
```

```python
import functools

import numpy as np
import jax
import jax.numpy as jnp
from jax import lax
from jax.experimental import pallas as pl
from jax.experimental.pallas import tpu as pltpu

F32 = jnp.float32
BF16 = jnp.bfloat16
I32 = jnp.int32
U32 = jnp.uint32

N_FOURIER_GROUPS = 8
N_RET_HEADS = 4
ROPE_BASE = 10000.0
TOP_K = 4
SWIGLU_LIMIT = 7.0
SWIGLU_ALPHA = 1.702
LN_EPS = 1e-5
GN_EPS = 1e-6
DEPTH = 1
DEEPNORM_ALPHA = (2.0 * DEPTH) ** 0.25

V7X_VMEM_BYTES = 64 * 2**20
VMEM_LIMIT_BYTES = V7X_VMEM_BYTES * 3 // 4
LANES = 128

IN_PROJ_ROWS = 512
IN_PROJ_PART_ROWS = 256
FOURIER_COLS_PER_STEP = 4
RET_CHUNK = 256
OUT_PROJ_ROWS = 512
OUT_PROJ_PART_ROWS = 256
ROUTER_ROWS = 1024
ROUTER_PART_ROWS = 256
DISPATCH_ROWS = 256
EXPERT_SUB_ROWS = 256
EXPERT_GROUP_SUBS = 9
EXPERT_FF_COLS = 256
EXPERT_VMEM_LIMIT_BYTES = V7X_VMEM_BYTES * 7 // 8
COMBINE_ROWS = 128
DMA_ISSUE_UNROLL = 8


def _params(*semantics):
    return pltpu.CompilerParams(dimension_semantics=semantics, vmem_limit_bytes=VMEM_LIMIT_BYTES)


def _layer_norm(x, g, b):
    mu = jnp.mean(x, axis=-1, keepdims=True)
    xc = x - mu
    var = jnp.mean(xc * xc, axis=-1, keepdims=True)
    return xc * lax.rsqrt(var + LN_EPS) * g + b


def _in_proj_kernel(x_ref, g_ref, b_ref, w_ref, base_ref, cos_ref, sin_ref, h_ref, uf_ref, ur_ref, hb_ref, *,
                    head_dim, part_rows):
    j = pl.program_id(1)
    tm = x_ref.shape[0]
    parts = [slice(r0, r0 + part_rows) for r0 in range(0, tm, part_rows)]

    @pl.when(j == 0)
    def _():
        for rows in parts:
            h = _layer_norm(x_ref[rows, :], g_ref[...], b_ref[...])
            h_ref[rows, :] = h
            hb = h.astype(BF16)
            hb_ref[rows, :] = hb
            uf_ref[rows, :] = jnp.dot(hb, w_ref[...], preferred_element_type=F32).astype(BF16)

    @pl.when((j == 1) | (j == 2))
    def _():
        half = head_dim // 2
        cos0, sin0 = base_ref[0:1, :], base_ref[1:2, :]
        for rows in parts:
            acc = jnp.dot(hb_ref[rows, :], w_ref[...], preferred_element_type=F32)
            cos = cos0 * cos_ref[rows, :] - sin0 * sin_ref[rows, :]
            sin = sin0 * cos_ref[rows, :] + cos0 * sin_ref[rows, :]
            for lo in range(0, acc.shape[1], head_dim):
                x1 = acc[:, lo:lo + half]
                x2 = acc[:, lo + half:lo + head_dim]
                ur_ref[rows, lo:lo + half] = (x1 * cos - x2 * sin).astype(BF16)
                ur_ref[rows, lo + half:lo + head_dim] = (x2 * cos + x1 * sin).astype(BF16)

    @pl.when(j >= 3)
    def _():
        for rows in parts:
            ur_ref[rows, :] = jnp.dot(hb_ref[rows, :], w_ref[...], preferred_element_type=F32).astype(BF16)


def _in_proj(x2, ln_g, ln_b, w_bf, head_dim):
    T, D = x2.shape
    tn = D // 2
    n_col = w_bf.shape[1] // tn
    assert n_col == 5 and w_bf.shape[1] == 5 * tn
    tm = min(IN_PROJ_ROWS, T)
    half = head_dim // 2
    inv = ROPE_BASE ** (-jnp.arange(half, dtype=F32) / half)
    ang0 = (jnp.arange(T // tm, dtype=I32) * tm).astype(F32)[:, None] * inv[None, :]
    ang1 = jnp.arange(tm, dtype=I32).astype(F32)[:, None] * inv[None, :]
    base = jnp.stack([jnp.cos(ang0), jnp.sin(ang0)], axis=1)
    cos, sin = jnp.cos(ang1), jnp.sin(ang1)
    return pl.pallas_call(
        functools.partial(_in_proj_kernel, head_dim=head_dim, part_rows=min(IN_PROJ_PART_ROWS, tm)),
        grid=(T // tm, n_col),
        in_specs=[
            pl.BlockSpec((tm, D), lambda i, j: (i, 0)),
            pl.BlockSpec((1, D), lambda i, j: (0, 0)),
            pl.BlockSpec((1, D), lambda i, j: (0, 0)),
            pl.BlockSpec((D, tn), lambda i, j: (0, j)),
            pl.BlockSpec((None, 2, half), lambda i, j: (i, 0, 0)),
            pl.BlockSpec((tm, half), lambda i, j: (0, 0)),
            pl.BlockSpec((tm, half), lambda i, j: (0, 0)),
        ],
        out_specs=[
            pl.BlockSpec((tm, D), lambda i, j: (i, 0)),
            pl.BlockSpec((tm, tn), lambda i, j: (i, 0)),
            pl.BlockSpec((tm, tn), lambda i, j: (i, jnp.maximum(j - 1, 0))),
        ],
        out_shape=[
            jax.ShapeDtypeStruct((T, D), F32),
            jax.ShapeDtypeStruct((T, tn), BF16),
            jax.ShapeDtypeStruct((T, 4 * tn), BF16),
        ],
        scratch_shapes=[pltpu.VMEM((tm, D), BF16)],
        compiler_params=_params("parallel", "arbitrary"),
        name="in_proj",
    )(x2, ln_g, ln_b, w_bf, base, cos, sin)


def _fourier_tables(n1, n2, gw):
    def cos_sin(n, r, c):
        ang = 2.0 * np.pi * ((np.outer(r, c) % n).astype(np.float64)) / n
        return np.cos(ang), np.sin(ang)

    cg, sg = cos_sin(gw, np.arange(gw), np.arange(gw))
    c1, s1 = cos_sin(n1, np.arange(n1), np.arange(n1))
    c3, s3 = cos_sin(n2, np.arange(n2), np.arange(n2))
    tc, ts = cos_sin(n1 * n2, np.arange(n1), np.arange(n2))
    cs = np.concatenate([cg, sg], axis=1)
    m1 = np.block([[c1, -s1], [-s1, -c1]])
    m3 = np.concatenate([c3, s3], axis=1)
    as32 = lambda a: jnp.asarray(a.astype(np.float32))
    return as32(cs).astype(BF16), as32(m1).astype(BF16), as32(m3).astype(BF16), as32(tc), as32(ts)


def _fourier_a_kernel(uf_ref, cs_ref, m1_ref, twc_ref, tws_ref, zz_ref, *, nb, gw):
    bb = pl.program_id(0)
    n1 = uf_ref.shape[0]
    width = uf_ref.shape[1] // nb
    cs = cs_ref[...]
    m1 = m1_ref[...]
    lane = lax.broadcasted_iota(I32, twc_ref.shape, 1)
    for t in range(nb):
        x = uf_ref[:, t * width:(t + 1) * width]
        a_parts, b_parts = [], []
        for lo in range(0, width, gw):
            ab = jnp.dot(x[:, lo:lo + gw], cs, preferred_element_type=F32)
            a_parts.append(ab[:, :gw])
            b_parts.append(ab[:, gw:])
        v = jnp.concatenate([jnp.concatenate(a_parts, axis=1), jnp.concatenate(b_parts, axis=1)], axis=0)
        z = jnp.dot(m1, v.astype(BF16), preferred_element_type=F32)
        zr, zi = z[:n1], z[n1:]
        sel = lane == bb * nb + t
        tc = jnp.sum(jnp.where(sel, twc_ref[...], 0.0), axis=1, keepdims=True)
        ts = jnp.sum(jnp.where(sel, tws_ref[...], 0.0), axis=1, keepdims=True)
        base = t * 2 * width
        zz_ref[:, base:base + width] = (zr * tc + zi * ts).astype(BF16)
        zz_ref[:, base + width:base + 2 * width] = (zi * tc - zr * ts).astype(BF16)


def _fourier_b_kernel(zz_ref, m3_ref, yf_ref, *, scale):
    nc = zz_ref.shape[0]
    width = zz_ref.shape[2] // 2
    m3 = m3_ref[...]
    for t in range(nc):
        z = zz_ref[t]
        stacked = jnp.concatenate([z[:, :width], z[:, width:]], axis=0)
        x = jnp.dot(m3, stacked, preferred_element_type=F32)
        yf_ref[:, t * width:(t + 1) * width] = (x * scale).astype(BF16)


def _fourier_mix(uf, gw):
    T, F = uf.shape
    n1 = min(LANES, T)
    n2 = T // n1
    assert n1 * n2 == T
    cs, m1, m3, twc, tws = _fourier_tables(n1, n2, gw)
    nb = min(FOURIER_COLS_PER_STEP, n2)
    nc = min(FOURIER_COLS_PER_STEP, n1)
    zz = pl.pallas_call(
        functools.partial(_fourier_a_kernel, nb=nb, gw=gw),
        grid=(n2 // nb,),
        in_specs=[
            pl.BlockSpec((n1, nb * F), lambda b: (0, b)),
            pl.BlockSpec(cs.shape, lambda b: (0, 0)),
            pl.BlockSpec(m1.shape, lambda b: (0, 0)),
            pl.BlockSpec(twc.shape, lambda b: (0, 0)),
            pl.BlockSpec(tws.shape, lambda b: (0, 0)),
        ],
        out_specs=pl.BlockSpec((n1, nb * 2 * F), lambda b: (0, b)),
        out_shape=jax.ShapeDtypeStruct((n1, n2 * 2 * F), BF16),
        compiler_params=_params("parallel"),
        name="fourier_a",
    )(uf.reshape(n1, n2 * F), cs, m1, twc, tws)
    yf = pl.pallas_call(
        functools.partial(_fourier_b_kernel, scale=float(1.0 / np.sqrt(T * gw))),
        grid=(n1 // nc,),
        in_specs=[
            pl.BlockSpec((nc, n2, 2 * F), lambda c: (c, 0, 0)),
            pl.BlockSpec(m3.shape, lambda c: (0, 0)),
        ],
        out_specs=pl.BlockSpec((n2, nc * F), lambda c: (0, c)),
        out_shape=jax.ShapeDtypeStruct((n2, n1 * F), BF16),
        compiler_params=_params("parallel"),
        name="fourier_b",
    )(zz.reshape(n1, n2, 2 * F), m3)
    return yf.reshape(T, F)


_TAB_K_TO_END_F, _TAB_Q_FROM_START_F, _TAB_K_TO_END_B, _TAB_Q_FROM_START_B, _TAB_CHUNK_F, _TAB_CHUNK_B = range(6)


def _retention_kernel(lgf_ref, lgb_ref, qf_ref, kf_ref, vf_ref, qb_ref, kb_ref, vb_ref, of_ref, ob_ref,
                      sf_ref, sb_ref, dmat_ref, tab_ref, *, kscale, n_heads):
    n = pl.program_id(0)
    C = qf_ref.shape[0]
    hd = qf_ref.shape[1] // n_heads

    @pl.when(n == 0)
    def _():
        r = lax.broadcasted_iota(I32, (C, C), 0).astype(F32)
        c = lax.broadcasted_iota(I32, (C, C), 1).astype(F32)
        diff = r - c
        rr = lax.broadcasted_iota(I32, (C, hd), 0).astype(F32)
        for head in range(n_heads):
            lgf = lgf_ref[head]
            lgb = lgb_ref[head]
            dmat_ref[head] = kscale * jnp.where(diff >= 0.0, jnp.exp(lgf * jnp.maximum(diff, 0.0)),
                                                jnp.exp(lgb * jnp.maximum(-diff, 0.0)))
            tab_ref[head, _TAB_K_TO_END_F] = kscale * jnp.exp(lgf * (C - 1.0 - rr))
            tab_ref[head, _TAB_Q_FROM_START_F] = jnp.exp(lgf * (rr + 1.0))
            tab_ref[head, _TAB_K_TO_END_B] = kscale * jnp.exp(lgb * rr)
            tab_ref[head, _TAB_Q_FROM_START_B] = jnp.exp(lgb * (C - rr))
            tab_ref[head, _TAB_CHUNK_F] = jnp.exp(jnp.full((C, hd), lgf * C, F32))
            tab_ref[head, _TAB_CHUNK_B] = jnp.exp(jnp.full((C, hd), lgb * C, F32))
        sf_ref[...] = jnp.zeros_like(sf_ref)
        sb_ref[...] = jnp.zeros_like(sb_ref)

    nt = (((1,), (1,)), ((), ()))
    tn = (((0,), (0,)), ((), ()))

    for head in range(n_heads):
        cols = slice(head * hd, (head + 1) * hd)
        q = qf_ref[:, cols]
        k = kf_ref[:, cols]
        v = vf_ref[:, cols]
        scores = lax.dot_general(q, k, nt, preferred_element_type=F32) * dmat_ref[head]
        intra = jnp.dot(scores.astype(BF16), v, preferred_element_type=F32)
        cross = (jnp.dot(q, sf_ref[head].astype(BF16), preferred_element_type=F32)
                 * tab_ref[head, _TAB_Q_FROM_START_F])
        of_ref[:, cols] = intra + cross
        v_dec = (v.astype(F32) * tab_ref[head, _TAB_K_TO_END_F]).astype(BF16)
        sf_ref[head] = (sf_ref[head] * tab_ref[head, _TAB_CHUNK_F]
                        + lax.dot_general(k, v_dec, tn, preferred_element_type=F32))

        q = qb_ref[:, cols]
        k = kb_ref[:, cols]
        v = vb_ref[:, cols]
        ob_ref[:, cols] = (jnp.dot(q, sb_ref[head].astype(BF16), preferred_element_type=F32)
                           * tab_ref[head, _TAB_Q_FROM_START_B])
        v_dec = (v.astype(F32) * tab_ref[head, _TAB_K_TO_END_B]).astype(BF16)
        sb_ref[head] = (sb_ref[head] * tab_ref[head, _TAB_CHUNK_B]
                        + lax.dot_general(k, v_dec, tn, preferred_element_type=F32))


def _retention(ur, lg_fwd, lg_bwd, n_heads, head_dim):
    T = ur.shape[0]
    C = min(RET_CHUNK, T)
    assert C == head_dim
    N = T // C
    H = n_heads
    rw = H * head_dim
    fwd = lambda col: (lambda n, *_: (n, col))
    bwd = lambda col: (lambda n, *_: (N - 1 - n, col))
    blk = lambda imap: pl.BlockSpec((C, rw), imap)
    return pl.pallas_call(
        functools.partial(_retention_kernel, kscale=float(head_dim) ** -0.5, n_heads=H),
        grid_spec=pltpu.PrefetchScalarGridSpec(
            num_scalar_prefetch=2,
            grid=(N,),
            in_specs=[blk(fwd(0)), blk(fwd(1)), blk(fwd(2)), blk(bwd(0)), blk(bwd(1)), blk(bwd(2))],
            out_specs=[blk(fwd(0)), blk(bwd(0))],
            scratch_shapes=[
                pltpu.VMEM((H, head_dim, head_dim), F32),
                pltpu.VMEM((H, head_dim, head_dim), F32),
                pltpu.VMEM((H, C, C), F32),
                pltpu.VMEM((H, 6, C, head_dim), F32),
            ],
        ),
        out_shape=[jax.ShapeDtypeStruct((T, rw), F32)] * 2,
        compiler_params=_params("arbitrary"),
        name="retention",
    )(lg_fwd, lg_bwd, ur, ur, ur, ur, ur, ur)


def _pack_bf16_pair(lo, hi):
    lo_bits = lax.bitcast_convert_type(lo.astype(BF16).astype(F32), U32)
    hi_bits = lax.bitcast_convert_type(hi.astype(BF16).astype(F32), U32)
    return (hi_bits & jnp.uint32(0xFFFF0000)) | (lo_bits >> 16)


def _unpack_bf16_pair(words):
    lo = lax.bitcast_convert_type(words << 16, F32).astype(BF16)
    hi = lax.bitcast_convert_type(words & jnp.uint32(0xFFFF0000), F32).astype(BF16)
    return lo, hi


def _out_proj_kernel(yf_ref, of_ref, ob_ref, g_ref, gng_ref, h_ref, wo_ref, lng_ref, lnb_ref, h1_ref, h1p_ref, *,
                     n_heads, part_rows):
    tm, rw = of_ref.shape
    hd = rw // n_heads
    fw = yf_ref.shape[1]
    D = h_ref.shape[1]
    for r0 in range(0, tm, part_rows):
        rows = slice(r0, r0 + part_rows)
        o = of_ref[rows, :] + ob_ref[rows, :]
        parts = []
        for lo in range(0, rw, hd):
            oh = o[:, lo:lo + hd]
            mu = jnp.mean(oh, axis=-1, keepdims=True)
            d = oh - mu
            var = jnp.mean(d * d, axis=-1, keepdims=True)
            parts.append(d * lax.rsqrt(var + GN_EPS))
        on = jnp.concatenate(parts, axis=-1) * gng_ref[...]
        g = g_ref[rows, :].astype(F32)
        yr = (g / (1.0 + jnp.exp(-g))) * on

        mix = (jnp.dot(yf_ref[rows, :], wo_ref[:fw, :], preferred_element_type=F32)
               + jnp.dot(yr.astype(BF16), wo_ref[fw:, :], preferred_element_type=F32))
        h1 = _layer_norm(DEEPNORM_ALPHA * h_ref[rows, :] + mix, lng_ref[...], lnb_ref[...])
        h1_ref[rows, :] = h1
        h1p_ref[rows, :] = _pack_bf16_pair(h1[:, :D // 2], h1[:, D // 2:])


def _out_proj(yf, o_f, o_b, ur, gn_g, h, wo_bf, ln_g, ln_b, n_heads):
    T, D = h.shape
    fw = yf.shape[1]
    rw = o_f.shape[1]
    tm = min(OUT_PROJ_ROWS, T)
    gate_col = ur.shape[1] // rw - 1
    row = lambda w: pl.BlockSpec((tm, w), lambda i: (i, 0))
    full = lambda a: pl.BlockSpec(a.shape, lambda i: (0,) * a.ndim)
    return pl.pallas_call(
        functools.partial(_out_proj_kernel, n_heads=n_heads, part_rows=min(OUT_PROJ_PART_ROWS, tm)),
        grid=(T // tm,),
        in_specs=[row(fw), row(rw), row(rw), pl.BlockSpec((tm, rw), lambda i: (i, gate_col)), full(gn_g), row(D),
                  full(wo_bf), full(ln_g), full(ln_b)],
        out_specs=[row(D), row(D // 2)],
        out_shape=[jax.ShapeDtypeStruct((T, D), F32), jax.ShapeDtypeStruct((T, D // 2), U32)],
        compiler_params=_params("parallel"),
        name="out_proj",
    )(yf, o_f, o_b, ur, gn_g, h, wo_bf, ln_g, ln_b)


def _router_kernel(h1_ref, wr_ref, wrl_ref, br_ref, topi_ref, gate_ref, rank_ref, cnt_ref, carry_ref, *, top_k,
                   part_rows):
    i = pl.program_id(0)
    tm = h1_ref.shape[0]
    E = wr_ref.shape[0]
    tp = part_rows

    @pl.when(i == 0)
    def _():
        carry_ref[...] = jnp.zeros_like(carry_ref)

    eidx = lax.broadcasted_iota(I32, (E, tp), 0)
    earlier = lax.broadcasted_iota(I32, (tp, tp), 0) < lax.broadcasted_iota(I32, (tp, tp), 1)
    earlier = jnp.where(earlier, 1.0, 0.0).astype(BF16)
    carry = carry_ref[...]

    for r0 in range(0, tm, tp):
        rows = slice(r0, r0 + tp)
        h = h1_ref[rows, :]
        h_hi = h.astype(BF16)
        h_lo = (h - h_hi.astype(F32)).astype(BF16)
        nt = (((1,), (1,)), ((), ()))
        logits = (lax.dot_general(wr_ref[...], h_hi, nt, preferred_element_type=F32)
                  + lax.dot_general(wr_ref[...], h_lo, nt, preferred_element_type=F32)
                  + lax.dot_general(wrl_ref[...], h_hi, nt, preferred_element_type=F32)) + br_ref[...]
        cur = logits
        sel_idx, sel_val = [], []
        for _ in range(top_k):
            m = jnp.max(cur, axis=0, keepdims=True)
            idx = jnp.min(jnp.where(cur == m, eidx, E), axis=0, keepdims=True)
            sel_idx.append(idx)
            sel_val.append(m)
            cur = jnp.where(eidx == idx, -jnp.inf, cur)
        ex = [jnp.exp(val - sel_val[0]) for val in sel_val]
        denom = ex[0]
        for e_r in ex[1:]:
            denom = denom + e_r
        topi_ref[:, rows] = jnp.concatenate(sel_idx, axis=0)
        gate_ref[:, rows] = jnp.concatenate([e_r / denom for e_r in ex], axis=0)

        member = jnp.zeros((E, tp), F32)
        for idx in sel_idx:
            member = member + jnp.where(eidx == idx, 1.0, 0.0)
        rank_full = jnp.dot(member.astype(BF16), earlier, preferred_element_type=F32) + carry
        ranks = [jnp.sum(jnp.where(eidx == idx, rank_full, 0.0), axis=0, keepdims=True) for idx in sel_idx]
        rank_ref[:, rows] = jnp.concatenate(ranks, axis=0).astype(I32)
        carry = carry + jnp.sum(member, axis=1, keepdims=True)

    carry_ref[...] = carry
    cnt_ref[...] = jnp.broadcast_to(carry, cnt_ref.shape).astype(I32)


def _router(h1, wr_t, br):
    T, D = h1.shape
    E = wr_t.shape[0]
    wr_hi = wr_t.astype(BF16)
    wr_lo = (wr_t - wr_hi.astype(F32)).astype(BF16)
    tm = min(ROUTER_ROWS, T)
    full = lambda a: pl.BlockSpec(a.shape, lambda i: (0,) * a.ndim)
    tok = lambda: pl.BlockSpec((TOP_K, tm), lambda i: (0, i))
    return pl.pallas_call(
        functools.partial(_router_kernel, top_k=TOP_K, part_rows=min(ROUTER_PART_ROWS, tm)),
        grid=(T // tm,),
        in_specs=[pl.BlockSpec((tm, D), lambda i: (i, 0)), full(wr_hi), full(wr_lo), full(br)],
        out_specs=[tok(), tok(), tok(), pl.BlockSpec((E, LANES), lambda i: (0, 0))],
        out_shape=[
            jax.ShapeDtypeStruct((TOP_K, T), I32),
            jax.ShapeDtypeStruct((TOP_K, T), F32),
            jax.ShapeDtypeStruct((TOP_K, T), I32),
            jax.ShapeDtypeStruct((E, LANES), I32),
        ],
        scratch_shapes=[pltpu.VMEM((E, 1), F32)],
        compiler_params=_params("arbitrary"),
        name="router",
    )(h1, wr_hi, wr_lo, br)


def _dst_kernel(start_ref, topi_ref, rank_ref, dst_ref, *, n_experts):
    topi = topi_ref[...]
    dst = rank_ref[...]
    for e in range(n_experts):
        dst = dst + jnp.where(topi == e, start_ref[e], 0)
    dst_ref[...] = dst


def _dst_rows(pad_start, topi, rank):
    spec = pl.BlockSpec(topi.shape, lambda i, *_: (0, 0))
    return pl.pallas_call(
        functools.partial(_dst_kernel, n_experts=pad_start.shape[0]),
        grid_spec=pltpu.PrefetchScalarGridSpec(num_scalar_prefetch=1, grid=(1,), in_specs=[spec, spec], out_specs=spec),
        out_shape=jax.ShapeDtypeStruct(topi.shape, I32),
        compiler_params=_params("arbitrary"),
        name="dst",
    )(pad_start, topi, rank)


def _per_block_slots(dst, tb):
    K, T = dst.shape
    return dst.reshape(K, T // tb, tb).transpose(1, 0, 2).reshape(T // tb, 1, K * tb)


def _dispatch_kernel(cnt_ref, start_ref, meta_ref, dst_ref, h1p_ref, xs_ref, zero_ref, sem, zsem, *, top_k, sub):
    i = pl.program_id(0)
    tb = h1p_ref.shape[0]

    def padding_copies(e, act):
        pos = start_ref[e] + cnt_ref[e]
        n_single = (8 - pos % 8) % 8
        for u in range(7):
            @pl.when(u < n_single)
            def _():
                act(pltpu.make_async_copy(zero_ref.at[pl.ds(0, 1)], xs_ref.at[pl.ds(pos + u, 1)], zsem))

        pos8 = pos + n_single
        rem = start_ref[e] + (cnt_ref[e] + sub - 1) // sub * sub - pos8
        size = sub // 2
        while size >= 8:
            at = pl.multiple_of(pos8 + rem // (2 * size) * (2 * size), 8)

            @pl.when(rem % (2 * size) >= size)
            def _():
                act(pltpu.make_async_copy(zero_ref.at[pl.ds(0, size)], xs_ref.at[pl.ds(at, size)], zsem))

            size //= 2

    def tail_copy(sb):
        rows = pl.ds(pl.multiple_of(sb * sub, sub), sub)
        return pltpu.make_async_copy(zero_ref, xs_ref.at[rows], zsem)

    @pl.when(i == 0)
    def _():
        zero_ref[...] = jnp.zeros_like(zero_ref)
        n_experts = cnt_ref.shape[0]
        n_total = xs_ref.shape[0] // sub
        for act in (lambda copy: copy.start(), lambda copy: copy.wait()):
            def per_expert(e, carry, act=act):
                padding_copies(e, act)
                return carry

            def per_tail(sb, carry, act=act):
                act(tail_copy(sb))
                return carry

            lax.fori_loop(0, n_experts, per_expert, 0)
            lax.fori_loop(meta_ref[1], n_total, per_tail, 0)

    def start(c, carry):
        for u in range(DMA_ISSUE_UNROLL):
            r = c * DMA_ISSUE_UNROLL + u
            for k in range(top_k):
                row = dst_ref[0, k * tb + r]
                pltpu.make_async_copy(h1p_ref.at[pl.ds(r, 1)], xs_ref.at[pl.ds(row, 1)], sem).start()
        return carry

    lax.fori_loop(0, tb // DMA_ISSUE_UNROLL, start, 0)
    for k in range(top_k):
        pltpu.make_async_copy(h1p_ref, xs_ref.at[pl.ds(0, tb)], sem).wait()


def _dispatch(counts, pad_start, meta, dst, h1p, n_rows):
    T, W = h1p.shape
    tb = min(DISPATCH_ROWS, T)
    slots = _per_block_slots(dst, tb)
    return pl.pallas_call(
        functools.partial(_dispatch_kernel, top_k=dst.shape[0], sub=EXPERT_SUB_ROWS),
        grid_spec=pltpu.PrefetchScalarGridSpec(
            num_scalar_prefetch=3,
            grid=(T // tb,),
            in_specs=[
                pl.BlockSpec((None, 1, slots.shape[2]), lambda i, *_: (i, 0, 0), memory_space=pltpu.SMEM),
                pl.BlockSpec((tb, W), lambda i, *_: (i, 0)),
            ],
            out_specs=pl.BlockSpec(memory_space=pl.ANY),
            scratch_shapes=[pltpu.VMEM((EXPERT_SUB_ROWS, W), U32), pltpu.SemaphoreType.DMA(()),
                            pltpu.SemaphoreType.DMA(())],
        ),
        out_shape=jax.ShapeDtypeStruct((n_rows, W), U32),
        compiler_params=_params("arbitrary"),
        name="dispatch",
    )(counts, pad_start, meta, slots, h1p)


def _experts_kernel(ge_ref, row0_ref, nsub_ref, meta_ref, bu_ref, bd_ref, wup_ref, wdn_ref, xs_ref, ys_ref,
                    xbuf_ref, acc_ref, ybuf_ref, wgf_ref, wlf_ref, wdf_ref, wgb_ref, wlb_ref, wdb_ref, stage_ref,
                    wsem, xsem, ysem, *, n_f):
    s = pl.program_id(0)
    n_groups = meta_ref[0]
    sub, half = stage_ref.shape[1:]
    tf = wdf_ref.shape[0]
    ff = n_f * tf

    def y_copy(ib, row):
        return pltpu.make_async_copy(ybuf_ref.at[ib], ys_ref.at[pl.ds(pl.multiple_of(row, sub), sub)], ysem)

    def weight_copies(e, j):
        glu_cols = pl.ds(pl.multiple_of(j * tf, tf), tf)
        lin_cols = pl.ds(pl.multiple_of(ff + j * tf, tf), tf)
        return (pltpu.make_async_copy(wup_ref.at[e, :, glu_cols], wgf_ref, wsem.at[0]),
                pltpu.make_async_copy(wup_ref.at[e, :, lin_cols], wlf_ref, wsem.at[1]),
                pltpu.make_async_copy(wdn_ref.at[e, glu_cols, :], wdf_ref, wsem.at[2]))

    def cast_weights(slot):
        wgb_ref[slot] = wgf_ref[...].astype(BF16)
        wlb_ref[slot] = wlf_ref[...].astype(BF16)
        wdb_ref[slot] = wdf_ref[...].astype(BF16)

    @pl.when(s < n_groups)
    def _():
        e = ge_ref[s]
        e_after = ge_ref[jnp.minimum(s + 1, n_groups - 1)]
        n_sub = nsub_ref[s]
        row0 = row0_ref[s]

        def x_copy(ib, slot):
            rows = pl.ds(pl.multiple_of(row0 + ib * sub, sub), sub)
            return pltpu.make_async_copy(xs_ref.at[rows], stage_ref.at[slot], xsem.at[slot])

        x_copy(0, 0).start()

        @pl.when(s == 0)
        def _():
            for copy in weight_copies(e, 0):
                copy.start()
            for copy in weight_copies(e, 0):
                copy.wait()
            cast_weights(0)

        def item(j, carry):
            slot = j % 2
            is_last = j == n_f - 1
            next_j = jnp.where(is_last, 0, j + 1)
            next_e = jnp.where(is_last, e_after, e)
            for copy in weight_copies(next_e, next_j):
                copy.start()

            def load_tokens(ib):
                xslot = ib % 2
                x_copy(ib, xslot).wait()

                @pl.when(ib + 1 < n_sub)
                def _():
                    x_copy(ib + 1, 1 - xslot).start()

                lo, hi = _unpack_bf16_pair(stage_ref[xslot])
                xbuf_ref[ib, :, :half] = lo
                xbuf_ref[ib, :, half:] = hi
                acc_ref[ib] = jnp.broadcast_to(bd_ref[...], acc_ref.shape[1:])

            def accumulate(ib):
                x = xbuf_ref[ib]
                glu = jnp.dot(x, wgb_ref[slot], preferred_element_type=F32) + bu_ref[pl.ds(j, 1), :]
                lin = jnp.dot(x, wlb_ref[slot], preferred_element_type=F32) + bu_ref[pl.ds(n_f + j, 1), :]
                glu = jnp.minimum(glu, SWIGLU_LIMIT)
                lin = jnp.clip(lin, -SWIGLU_LIMIT, SWIGLU_LIMIT)
                act = glu / (1.0 + jnp.exp(-SWIGLU_ALPHA * glu)) * (lin + 1.0)
                acc_ref[ib] += jnp.dot(act.astype(BF16), wdb_ref[slot], preferred_element_type=F32)

            def process(ibs, with_next_weights):
                @pl.when(j == 0)
                def _():
                    for ib in ibs:
                        load_tokens(ib)

                if with_next_weights:
                    for copy in weight_copies(next_e, next_j):
                        copy.wait()
                    cast_weights(1 - slot)
                for ib in ibs:
                    accumulate(ib)

                @pl.when(is_last)
                def _():
                    for ib in ibs:
                        y = acc_ref[ib]
                        ybuf_ref[ib] = _pack_bf16_pair(y[:, :half], y[:, half:])
                        y_copy(ib, row0 + ib * sub).start()

            def pair(p, carry):
                process([2 * p, 2 * p + 1], False)
                return carry

            lax.fori_loop(0, (n_sub - 1) // 2, pair, 0)

            @pl.when(n_sub % 2 == 0)
            def _():
                process([n_sub - 2, n_sub - 1], True)

            @pl.when(n_sub % 2 == 1)
            def _():
                process([n_sub - 1], True)

            return carry

        lax.fori_loop(0, n_f, item, 0)

        def drain(ib, carry):
            y_copy(ib, row0 + ib * sub).wait()
            return carry

        lax.fori_loop(0, n_sub, drain, 0)

    @pl.when(s == pl.num_programs(0) - 1)
    def _():
        n_total = ys_ref.shape[0] // sub
        ybuf_ref[0] = jnp.zeros(ybuf_ref.shape[1:], U32)

        def fill(ib, carry):
            y_copy(0, ib * sub).start()
            return carry

        def drain(ib, carry):
            y_copy(0, ib * sub).wait()
            return carry

        lax.fori_loop(meta_ref[1], n_total, fill, 0)
        lax.fori_loop(meta_ref[1], n_total, drain, 0)


def _experts(group_expert, group_row0, group_nsub, meta, xs, w_up, b_up, w_down, b_down):
    R, half = xs.shape
    E, D, ff2 = w_up.shape
    ff = ff2 // 2
    sub = EXPERT_SUB_ROWS
    gb = EXPERT_GROUP_SUBS
    tf = min(EXPERT_FF_COLS, ff)
    n_f = ff // tf
    assert n_f % 2 == 0
    grp = lambda s, ge, r0, ns, meta: (ge[jnp.minimum(s, meta[0] - 1)], 0, 0)
    return pl.pallas_call(
        functools.partial(_experts_kernel, n_f=n_f),
        grid_spec=pltpu.PrefetchScalarGridSpec(
            num_scalar_prefetch=4,
            grid=(group_expert.shape[0],),
            in_specs=[
                pl.BlockSpec((None, 2 * n_f, tf), grp),
                pl.BlockSpec((None, 1, D), grp),
                pl.BlockSpec(memory_space=pl.ANY),
                pl.BlockSpec(memory_space=pl.ANY),
                pl.BlockSpec(memory_space=pl.ANY),
            ],
            out_specs=pl.BlockSpec(memory_space=pl.ANY),
            scratch_shapes=[
                pltpu.VMEM((gb, sub, D), BF16),
                pltpu.VMEM((gb, sub, D), F32),
                pltpu.VMEM((gb, sub, half), U32),
                pltpu.VMEM((D, tf), F32),
                pltpu.VMEM((D, tf), F32),
                pltpu.VMEM((tf, D), F32),
                pltpu.VMEM((2, D, tf), BF16),
                pltpu.VMEM((2, D, tf), BF16),
                pltpu.VMEM((2, tf, D), BF16),
                pltpu.VMEM((2, sub, half), U32),
                pltpu.SemaphoreType.DMA((3,)),
                pltpu.SemaphoreType.DMA((2,)),
                pltpu.SemaphoreType.DMA(()),
            ],
        ),
        out_shape=jax.ShapeDtypeStruct((R, half), U32),
        compiler_params=pltpu.CompilerParams(dimension_semantics=("arbitrary",),
                                             vmem_limit_bytes=EXPERT_VMEM_LIMIT_BYTES),
        name="experts",
    )(group_expert, group_row0, group_nsub, meta, b_up.reshape(E, 2 * n_f, tf), b_down.reshape(E, 1, D),
      w_up, w_down, xs)


def _combine_kernel(dst_ref, nxt_ref, gate_ref, h1_ref, lng_ref, lnb_ref, ys_ref, out_ref, buf_ref, sem, *, top_k):
    i = pl.program_id(0)
    tb = h1_ref.shape[0]
    slot = i % 2

    def fetch(slots_ref, into):
        def start(c, carry):
            for u in range(DMA_ISSUE_UNROLL):
                r = c * DMA_ISSUE_UNROLL + u
                for k in range(top_k):
                    row = slots_ref[0, k * tb + r]
                    pltpu.make_async_copy(ys_ref.at[pl.ds(row, 1)], buf_ref.at[into, k, pl.ds(r, 1)],
                                          sem.at[into]).start()
            return carry

        lax.fori_loop(0, tb // DMA_ISSUE_UNROLL, start, 0)

    @pl.when(i == 0)
    def _():
        fetch(dst_ref, 0)

    @pl.when(i + 1 < pl.num_programs(0))
    def _():
        fetch(nxt_ref, 1 - slot)

    for k in range(top_k):
        pltpu.make_async_copy(ys_ref.at[pl.ds(0, tb)], buf_ref.at[slot, k], sem.at[slot]).wait()

    diag = lax.broadcasted_iota(I32, (tb, tb), 0) == lax.broadcasted_iota(I32, (tb, tb), 1)
    half = buf_ref.shape[3]
    ffn_lo = jnp.zeros((tb, half), F32)
    ffn_hi = jnp.zeros((tb, half), F32)
    for k in range(top_k):
        gate = jnp.sum(jnp.where(diag, gate_ref[k:k + 1, :], 0.0), axis=1, keepdims=True)
        lo, hi = _unpack_bf16_pair(buf_ref[slot, k])
        ffn_lo = ffn_lo + lo.astype(F32) * gate
        ffn_hi = ffn_hi + hi.astype(F32) * gate
    ffn = jnp.concatenate([ffn_lo, ffn_hi], axis=1)
    out_ref[...] = _layer_norm(DEEPNORM_ALPHA * h1_ref[...] + ffn, lng_ref[...], lnb_ref[...])


def _combine(dst, gates, h1, ln_g, ln_b, ys):
    T, D = h1.shape
    K = dst.shape[0]
    tb = min(COMBINE_ROWS, T)
    slots = _per_block_slots(dst, tb)
    n_blocks = T // tb
    return pl.pallas_call(
        functools.partial(_combine_kernel, top_k=K),
        grid=(n_blocks,),
        in_specs=[
            pl.BlockSpec((None, 1, K * tb), lambda i: (i, 0, 0), memory_space=pltpu.SMEM),
            pl.BlockSpec((None, 1, K * tb), lambda i: (jnp.minimum(i + 1, n_blocks - 1), 0, 0),
                         memory_space=pltpu.SMEM),
            pl.BlockSpec((K, tb), lambda i: (0, i)),
            pl.BlockSpec((tb, D), lambda i: (i, 0)),
            pl.BlockSpec((1, D), lambda i: (0, 0)),
            pl.BlockSpec((1, D), lambda i: (0, 0)),
            pl.BlockSpec(memory_space=pl.ANY),
        ],
        out_specs=pl.BlockSpec((tb, D), lambda i: (i, 0)),
        out_shape=jax.ShapeDtypeStruct((T, D), F32),
        scratch_shapes=[pltpu.VMEM((2, K, tb, D // 2), U32), pltpu.SemaphoreType.DMA((2,))],
        compiler_params=_params("arbitrary"),
        name="combine",
    )(slots, slots, gates, h1, ln_g, ln_b, ys)


def _expert_schedule(counts, n_tokens_routed):
    sub, gb = EXPERT_SUB_ROWS, EXPERT_GROUP_SUBS
    n_experts = counts.shape[0]
    max_subs = n_tokens_routed // sub + n_experts
    max_groups = n_experts + max_subs // gb
    n_sub = (counts + sub - 1) // sub
    sub_end = jnp.cumsum(n_sub)
    sub_start = sub_end - n_sub
    n_grp = (n_sub + gb - 1) // gb
    grp_end = jnp.cumsum(n_grp)
    grp_start = grp_end - n_grp
    s = jnp.arange(max_groups, dtype=I32)
    expert = jnp.minimum(jnp.sum(grp_end[None, :] <= s[:, None], axis=1), n_experts - 1).astype(I32)
    onehot = (expert[:, None] == jnp.arange(n_experts, dtype=I32)[None, :]).astype(I32)
    pick = lambda v: jnp.sum(onehot * v[None, :], axis=1)
    local = s - pick(grp_start)
    row0 = (pick(sub_start) + local * gb) * sub
    nsub = jnp.clip(pick(n_sub) - local * gb, 0, gb)
    meta = jnp.stack([grp_end[-1], sub_end[-1]])
    to_i32 = lambda a: a.astype(I32)
    return to_i32(sub_start * sub), to_i32(expert), to_i32(row0), to_i32(nsub), to_i32(meta), max_subs * sub


def kernel(x, ln0_g, ln0_b, w_in, ret_decay_fwd, ret_decay_bwd, ret_gn_g, w_out, ln1_g, ln1_b, w_router, b_router,
           w_up, b_up, w_down, b_down, ln2_g, ln2_b):
    B, S, D = x.shape
    assert B == 1 and w_in.shape[0] == DEPTH
    T = S
    fw = D // 2
    rw = D - fw
    gw = fw // N_FOURIER_GROUPS
    head_dim = rw // N_RET_HEADS
    E = w_router.shape[-1]
    row = lambda a: a.reshape(1, -1)

    h, uf, ur = _in_proj(x.reshape(T, D), row(ln0_g), row(ln0_b), w_in[0].astype(BF16), head_dim)
    yf = _fourier_mix(uf, gw)
    lg_fwd = -jnp.exp(ret_decay_fwd[0].astype(F32))
    lg_bwd = -jnp.exp(ret_decay_bwd[0].astype(F32))
    o_f, o_b = _retention(ur, lg_fwd, lg_bwd, N_RET_HEADS, head_dim)
    h1, h1p = _out_proj(yf, o_f, o_b, ur, row(ret_gn_g[0]), h, w_out[0].astype(BF16), row(ln1_g[0]), row(ln1_b[0]),
                        N_RET_HEADS)
    topi, gates, rank, cnt = _router(h1, w_router[0].T, b_router[0].reshape(E, 1))

    counts = cnt[:, 0]
    pad_start, group_expert, group_row0, group_nsub, meta, n_rows = _expert_schedule(counts, T * TOP_K)
    dst = _dst_rows(pad_start, topi, rank)
    xs = _dispatch(counts, pad_start, meta, dst, h1p, n_rows)
    ys = _experts(group_expert, group_row0, group_nsub, meta, xs, w_up.reshape(w_up.shape[1:]), b_up[0],
                  w_down.reshape(w_down.shape[1:]), b_down[0])
    out = _combine(dst, gates, h1, row(ln2_g[0]), row(ln2_b[0]), ys)
    return out.reshape(B, S, D)
```

```python
import functools

import numpy as np
import jax
import jax.numpy as jnp
from jax import lax
from jax.experimental import pallas as pl
from jax.experimental.pallas import tpu as pltpu

F32 = jnp.float32
BF16 = jnp.bfloat16
I32 = jnp.int32
U32 = jnp.uint32

N_FOURIER_GROUPS = 8
N_RET_HEADS = 4
ROPE_BASE = 10000.0
TOP_K = 4
SWIGLU_LIMIT = 7.0
SWIGLU_ALPHA = 1.702
LN_EPS = 1e-5
GN_EPS = 1e-6
DEPTH = 1
DEEPNORM_ALPHA = (2.0 * DEPTH) ** 0.25

V7X_VMEM_BYTES = 64 * 2**20
VMEM_LIMIT_BYTES = V7X_VMEM_BYTES * 3 // 4
LANES = 128

IN_PROJ_ROWS = 512
IN_PROJ_PART_ROWS = 256
FOURIER_COLS_PER_STEP = 4
RET_CHUNK = 256
OUT_PROJ_ROWS = 512
OUT_PROJ_PART_ROWS = 256
ROUTER_ROWS = 1024
ROUTER_PART_ROWS = 256
DISPATCH_ROWS = 256
EXPERT_SUB_ROWS = 256
EXPERT_GROUP_SUBS = 9
EXPERT_FF_COLS = 256
EXPERT_VMEM_LIMIT_BYTES = V7X_VMEM_BYTES * 7 // 8
COMBINE_ROWS = 128
DMA_ISSUE_UNROLL = 8


def _params(*semantics):
    return pltpu.CompilerParams(dimension_semantics=semantics, vmem_limit_bytes=VMEM_LIMIT_BYTES)


def _layer_norm(x, g, b):
    mu = jnp.mean(x, axis=-1, keepdims=True)
    xc = x - mu
    var = jnp.mean(xc * xc, axis=-1, keepdims=True)
    return xc * lax.rsqrt(var + LN_EPS) * g + b


def _in_proj_kernel(x_ref, g_ref, b_ref, w_ref, base_ref, cos_ref, sin_ref, h_ref, uf_ref, ur_ref, hb_ref, *,
                    head_dim, part_rows):
    j = pl.program_id(1)
    tm = x_ref.shape[0]
    parts = [slice(r0, r0 + part_rows) for r0 in range(0, tm, part_rows)]

    @pl.when(j == 0)
    def _():
        for rows in parts:
            h = _layer_norm(x_ref[rows, :], g_ref[...], b_ref[...])
            h_ref[rows, :] = h
            hb = h.astype(BF16)
            hb_ref[rows, :] = hb
            uf_ref[rows, :] = jnp.dot(hb, w_ref[...], preferred_element_type=F32).astype(BF16)

    @pl.when((j == 1) | (j == 2))
    def _():
        half = head_dim // 2
        cos0, sin0 = base_ref[0:1, :], base_ref[1:2, :]
        for rows in parts:
            acc = jnp.dot(hb_ref[rows, :], w_ref[...], preferred_element_type=F32)
            cos = cos0 * cos_ref[rows, :] - sin0 * sin_ref[rows, :]
            sin = sin0 * cos_ref[rows, :] + cos0 * sin_ref[rows, :]
            for lo in range(0, acc.shape[1], head_dim):
                x1 = acc[:, lo:lo + half]
                x2 = acc[:, lo + half:lo + head_dim]
                ur_ref[rows, lo:lo + half] = (x1 * cos - x2 * sin).astype(BF16)
                ur_ref[rows, lo + half:lo + head_dim] = (x2 * cos + x1 * sin).astype(BF16)

    @pl.when(j >= 3)
    def _():
        for rows in parts:
            ur_ref[rows, :] = jnp.dot(hb_ref[rows, :], w_ref[...], preferred_element_type=F32).astype(BF16)


def _in_proj(x2, ln_g, ln_b, w_bf, head_dim):
    T, D = x2.shape
    tn = D // 2
    n_col = w_bf.shape[1] // tn
    assert n_col == 5 and w_bf.shape[1] == 5 * tn
    tm = min(IN_PROJ_ROWS, T)
    half = head_dim // 2
    inv = ROPE_BASE ** (-jnp.arange(half, dtype=F32) / half)
    ang0 = (jnp.arange(T // tm, dtype=I32) * tm).astype(F32)[:, None] * inv[None, :]
    ang1 = jnp.arange(tm, dtype=I32).astype(F32)[:, None] * inv[None, :]
    base = jnp.stack([jnp.cos(ang0), jnp.sin(ang0)], axis=1)
    cos, sin = jnp.cos(ang1), jnp.sin(ang1)
    return pl.pallas_call(
        functools.partial(_in_proj_kernel, head_dim=head_dim, part_rows=min(IN_PROJ_PART_ROWS, tm)),
        grid=(T // tm, n_col),
        in_specs=[
            pl.BlockSpec((tm, D), lambda i, j: (i, 0)),
            pl.BlockSpec((1, D), lambda i, j: (0, 0)),
            pl.BlockSpec((1, D), lambda i, j: (0, 0)),
            pl.BlockSpec((D, tn), lambda i, j: (0, j)),
            pl.BlockSpec((None, 2, half), lambda i, j: (i, 0, 0)),
            pl.BlockSpec((tm, half), lambda i, j: (0, 0)),
            pl.BlockSpec((tm, half), lambda i, j: (0, 0)),
        ],
        out_specs=[
            pl.BlockSpec((tm, D), lambda i, j: (i, 0)),
            pl.BlockSpec((tm, tn), lambda i, j: (i, 0)),
            pl.BlockSpec((tm, tn), lambda i, j: (i, jnp.maximum(j - 1, 0))),
        ],
        out_shape=[
            jax.ShapeDtypeStruct((T, D), F32),
            jax.ShapeDtypeStruct((T, tn), BF16),
            jax.ShapeDtypeStruct((T, 4 * tn), BF16),
        ],
        scratch_shapes=[pltpu.VMEM((tm, D), BF16)],
        compiler_params=_params("parallel", "arbitrary"),
        name="in_proj",
    )(x2, ln_g, ln_b, w_bf, base, cos, sin)


def _fourier_tables(n1, n2, gw):
    def cos_sin(n, r, c):
        ang = 2.0 * np.pi * ((np.outer(r, c) % n).astype(np.float64)) / n
        return np.cos(ang), np.sin(ang)

    cg, sg = cos_sin(gw, np.arange(gw), np.arange(gw))
    c1, s1 = cos_sin(n1, np.arange(n1), np.arange(n1))
    c3, s3 = cos_sin(n2, np.arange(n2), np.arange(n2))
    tc, ts = cos_sin(n1 * n2, np.arange(n1), np.arange(n2))
    cs = np.concatenate([cg, sg], axis=1)
    m1 = np.block([[c1, -s1], [-s1, -c1]])
    m3 = np.concatenate([c3, s3], axis=1)
    as32 = lambda a: jnp.asarray(a.astype(np.float32))
    return as32(cs).astype(BF16), as32(m1).astype(BF16), as32(m3).astype(BF16), as32(tc), as32(ts)


def _fourier_a_kernel(uf_ref, cs_ref, m1_ref, twc_ref, tws_ref, zz_ref, *, nb, gw):
    bb = pl.program_id(0)
    n1 = uf_ref.shape[0]
    width = uf_ref.shape[1] // nb
    cs = cs_ref[...]
    m1 = m1_ref[...]
    lane = lax.broadcasted_iota(I32, twc_ref.shape, 1)
    for t in range(nb):
        x = uf_ref[:, t * width:(t + 1) * width]
        a_parts, b_parts = [], []
        for lo in range(0, width, gw):
            ab = jnp.dot(x[:, lo:lo + gw], cs, preferred_element_type=F32)
            a_parts.append(ab[:, :gw])
            b_parts.append(ab[:, gw:])
        v = jnp.concatenate([jnp.concatenate(a_parts, axis=1), jnp.concatenate(b_parts, axis=1)], axis=0)
        z = jnp.dot(m1, v.astype(BF16), preferred_element_type=F32)
        zr, zi = z[:n1], z[n1:]
        sel = lane == bb * nb + t
        tc = jnp.sum(jnp.where(sel, twc_ref[...], 0.0), axis=1, keepdims=True)
        ts = jnp.sum(jnp.where(sel, tws_ref[...], 0.0), axis=1, keepdims=True)
        base = t * 2 * width
        zz_ref[:, base:base + width] = (zr * tc + zi * ts).astype(BF16)
        zz_ref[:, base + width:base + 2 * width] = (zi * tc - zr * ts).astype(BF16)


def _fourier_b_kernel(zz_ref, m3_ref, yf_ref, *, scale):
    nc = zz_ref.shape[0]
    width = zz_ref.shape[2] // 2
    m3 = m3_ref[...]
    for t in range(nc):
        z = zz_ref[t]
        stacked = jnp.concatenate([z[:, :width], z[:, width:]], axis=0)
        x = jnp.dot(m3, stacked, preferred_element_type=F32)
        yf_ref[:, t * width:(t + 1) * width] = (x * scale).astype(BF16)


def _fourier_mix(uf, gw):
    T, F = uf.shape
    n1 = min(LANES, T)
    n2 = T // n1
    assert n1 * n2 == T
    cs, m1, m3, twc, tws = _fourier_tables(n1, n2, gw)
    nb = min(FOURIER_COLS_PER_STEP, n2)
    nc = min(FOURIER_COLS_PER_STEP, n1)
    zz = pl.pallas_call(
        functools.partial(_fourier_a_kernel, nb=nb, gw=gw),
        grid=(n2 // nb,),
        in_specs=[
            pl.BlockSpec((n1, nb * F), lambda b: (0, b)),
            pl.BlockSpec(cs.shape, lambda b: (0, 0)),
            pl.BlockSpec(m1.shape, lambda b: (0, 0)),
            pl.BlockSpec(twc.shape, lambda b: (0, 0)),
            pl.BlockSpec(tws.shape, lambda b: (0, 0)),
        ],
        out_specs=pl.BlockSpec((n1, nb * 2 * F), lambda b: (0, b)),
        out_shape=jax.ShapeDtypeStruct((n1, n2 * 2 * F), BF16),
        compiler_params=_params("parallel"),
        name="fourier_a",
    )(uf.reshape(n1, n2 * F), cs, m1, twc, tws)
    yf = pl.pallas_call(
        functools.partial(_fourier_b_kernel, scale=float(1.0 / np.sqrt(T * gw))),
        grid=(n1 // nc,),
        in_specs=[
            pl.BlockSpec((nc, n2, 2 * F), lambda c: (c, 0, 0)),
            pl.BlockSpec(m3.shape, lambda c: (0, 0)),
        ],
        out_specs=pl.BlockSpec((n2, nc * F), lambda c: (0, c)),
        out_shape=jax.ShapeDtypeStruct((n2, n1 * F), BF16),
        compiler_params=_params("parallel"),
        name="fourier_b",
    )(zz.reshape(n1, n2, 2 * F), m3)
    return yf.reshape(T, F)


_TAB_K_TO_END_F, _TAB_Q_FROM_START_F, _TAB_K_TO_END_B, _TAB_Q_FROM_START_B, _TAB_CHUNK_F, _TAB_CHUNK_B = range(6)


def _retention_kernel(lgf_ref, lgb_ref, qf_ref, kf_ref, vf_ref, qb_ref, kb_ref, vb_ref, of_ref, ob_ref,
                      sf_ref, sb_ref, dmat_ref, tab_ref, *, kscale, n_heads):
    n = pl.program_id(0)
    C = qf_ref.shape[0]
    hd = qf_ref.shape[1] // n_heads

    @pl.when(n == 0)
    def _():
        r = lax.broadcasted_iota(I32, (C, C), 0).astype(F32)
        c = lax.broadcasted_iota(I32, (C, C), 1).astype(F32)
        diff = r - c
        rr = lax.broadcasted_iota(I32, (C, hd), 0).astype(F32)
        for head in range(n_heads):
            lgf = lgf_ref[head]
            lgb = lgb_ref[head]
            dmat_ref[head] = kscale * jnp.where(diff >= 0.0, jnp.exp(lgf * jnp.maximum(diff, 0.0)),
                                                jnp.exp(lgb * jnp.maximum(-diff, 0.0)))
            tab_ref[head, _TAB_K_TO_END_F] = kscale * jnp.exp(lgf * (C - 1.0 - rr))
            tab_ref[head, _TAB_Q_FROM_START_F] = jnp.exp(lgf * (rr + 1.0))
            tab_ref[head, _TAB_K_TO_END_B] = kscale * jnp.exp(lgb * rr)
            tab_ref[head, _TAB_Q_FROM_START_B] = jnp.exp(lgb * (C - rr))
            tab_ref[head, _TAB_CHUNK_F] = jnp.exp(jnp.full((C, hd), lgf * C, F32))
            tab_ref[head, _TAB_CHUNK_B] = jnp.exp(jnp.full((C, hd), lgb * C, F32))
        sf_ref[...] = jnp.zeros_like(sf_ref)
        sb_ref[...] = jnp.zeros_like(sb_ref)

    nt = (((1,), (1,)), ((), ()))
    tn = (((0,), (0,)), ((), ()))

    for head in range(n_heads):
        cols = slice(head * hd, (head + 1) * hd)
        q = qf_ref[:, cols]
        k = kf_ref[:, cols]
        v = vf_ref[:, cols]
        scores = lax.dot_general(q, k, nt, preferred_element_type=F32) * dmat_ref[head]
        intra = jnp.dot(scores.astype(BF16), v, preferred_element_type=F32)
        cross = (jnp.dot(q, sf_ref[head].astype(BF16), preferred_element_type=F32)
                 * tab_ref[head, _TAB_Q_FROM_START_F])
        of_ref[:, cols] = intra + cross
        v_dec = (v.astype(F32) * tab_ref[head, _TAB_K_TO_END_F]).astype(BF16)
        sf_ref[head] = (sf_ref[head] * tab_ref[head, _TAB_CHUNK_F]
                        + lax.dot_general(k, v_dec, tn, preferred_element_type=F32))

        q = qb_ref[:, cols]
        k = kb_ref[:, cols]
        v = vb_ref[:, cols]
        ob_ref[:, cols] = (jnp.dot(q, sb_ref[head].astype(BF16), preferred_element_type=F32)
                           * tab_ref[head, _TAB_Q_FROM_START_B])
        v_dec = (v.astype(F32) * tab_ref[head, _TAB_K_TO_END_B]).astype(BF16)
        sb_ref[head] = (sb_ref[head] * tab_ref[head, _TAB_CHUNK_B]
                        + lax.dot_general(k, v_dec, tn, preferred_element_type=F32))


def _retention(ur, lg_fwd, lg_bwd, n_heads, head_dim):
    T = ur.shape[0]
    C = min(RET_CHUNK, T)
    assert C == head_dim
    N = T // C
    H = n_heads
    rw = H * head_dim
    fwd = lambda col: (lambda n, *_: (n, col))
    bwd = lambda col: (lambda n, *_: (N - 1 - n, col))
    blk = lambda imap: pl.BlockSpec((C, rw), imap)
    return pl.pallas_call(
        functools.partial(_retention_kernel, kscale=float(head_dim) ** -0.5, n_heads=H),
        grid_spec=pltpu.PrefetchScalarGridSpec(
            num_scalar_prefetch=2,
            grid=(N,),
            in_specs=[blk(fwd(0)), blk(fwd(1)), blk(fwd(2)), blk(bwd(0)), blk(bwd(1)), blk(bwd(2))],
            out_specs=[blk(fwd(0)), blk(bwd(0))],
            scratch_shapes=[
                pltpu.VMEM((H, head_dim, head_dim), F32),
                pltpu.VMEM((H, head_dim, head_dim), F32),
                pltpu.VMEM((H, C, C), F32),
                pltpu.VMEM((H, 6, C, head_dim), F32),
            ],
        ),
        out_shape=[jax.ShapeDtypeStruct((T, rw), F32)] * 2,
        compiler_params=_params("arbitrary"),
        name="retention",
    )(lg_fwd, lg_bwd, ur, ur, ur, ur, ur, ur)


def _pack_bf16_pair(lo, hi):
    lo_bits = lax.bitcast_convert_type(lo.astype(BF16).astype(F32), U32)
    hi_bits = lax.bitcast_convert_type(hi.astype(BF16).astype(F32), U32)
    return (hi_bits & jnp.uint32(0xFFFF0000)) | (lo_bits >> 16)


def _unpack_bf16_pair(words):
    lo = lax.bitcast_convert_type(words << 16, F32).astype(BF16)
    hi = lax.bitcast_convert_type(words & jnp.uint32(0xFFFF0000), F32).astype(BF16)
    return lo, hi


def _out_proj_kernel(yf_ref, of_ref, ob_ref, g_ref, gng_ref, h_ref, wo_ref, lng_ref, lnb_ref, h1_ref, h1p_ref, *,
                     n_heads, part_rows):
    tm, rw = of_ref.shape
    hd = rw // n_heads
    fw = yf_ref.shape[1]
    D = h_ref.shape[1]
    for r0 in range(0, tm, part_rows):
        rows = slice(r0, r0 + part_rows)
        o = of_ref[rows, :] + ob_ref[rows, :]
        parts = []
        for lo in range(0, rw, hd):
            oh = o[:, lo:lo + hd]
            mu = jnp.mean(oh, axis=-1, keepdims=True)
            d = oh - mu
            var = jnp.mean(d * d, axis=-1, keepdims=True)
            parts.append(d * lax.rsqrt(var + GN_EPS))
        on = jnp.concatenate(parts, axis=-1) * gng_ref[...]
        g = g_ref[rows, :].astype(F32)
        yr = (g / (1.0 + jnp.exp(-g))) * on

        mix = (jnp.dot(yf_ref[rows, :], wo_ref[:fw, :], preferred_element_type=F32)
               + jnp.dot(yr.astype(BF16), wo_ref[fw:, :], preferred_element_type=F32))
        h1 = _layer_norm(DEEPNORM_ALPHA * h_ref[rows, :] + mix, lng_ref[...], lnb_ref[...])
        h1_ref[rows, :] = h1
        h1p_ref[rows, :] = _pack_bf16_pair(h1[:, :D // 2], h1[:, D // 2:])


def _out_proj(yf, o_f, o_b, ur, gn_g, h, wo_bf, ln_g, ln_b, n_heads):
    T, D = h.shape
    fw = yf.shape[1]
    rw = o_f.shape[1]
    tm = min(OUT_PROJ_ROWS, T)
    gate_col = ur.shape[1] // rw - 1
    row = lambda w: pl.BlockSpec((tm, w), lambda i: (i, 0))
    full = lambda a: pl.BlockSpec(a.shape, lambda i: (0,) * a.ndim)
    return pl.pallas_call(
        functools.partial(_out_proj_kernel, n_heads=n_heads, part_rows=min(OUT_PROJ_PART_ROWS, tm)),
        grid=(T // tm,),
        in_specs=[row(fw), row(rw), row(rw), pl.BlockSpec((tm, rw), lambda i: (i, gate_col)), full(gn_g), row(D),
                  full(wo_bf), full(ln_g), full(ln_b)],
        out_specs=[row(D), row(D // 2)],
        out_shape=[jax.ShapeDtypeStruct((T, D), F32), jax.ShapeDtypeStruct((T, D // 2), U32)],
        compiler_params=_params("parallel"),
        name="out_proj",
    )(yf, o_f, o_b, ur, gn_g, h, wo_bf, ln_g, ln_b)


def _router_kernel(h1_ref, wr_ref, wrl_ref, br_ref, topi_ref, gate_ref, rank_ref, cnt_ref, carry_ref, *, top_k,
                   part_rows):
    i = pl.program_id(0)
    tm = h1_ref.shape[0]
    E = wr_ref.shape[0]
    tp = part_rows

    @pl.when(i == 0)
    def _():
        carry_ref[...] = jnp.zeros_like(carry_ref)

    eidx = lax.broadcasted_iota(I32, (E, tp), 0)
    earlier = lax.broadcasted_iota(I32, (tp, tp), 0) < lax.broadcasted_iota(I32, (tp, tp), 1)
    earlier = jnp.where(earlier, 1.0, 0.0).astype(BF16)
    carry = carry_ref[...]

    for r0 in range(0, tm, tp):
        rows = slice(r0, r0 + tp)
        h = h1_ref[rows, :]
        h_hi = h.astype(BF16)
        h_lo = (h - h_hi.astype(F32)).astype(BF16)
        nt = (((1,), (1,)), ((), ()))
        logits = (lax.dot_general(wr_ref[...], h_hi, nt, preferred_element_type=F32)
                  + lax.dot_general(wr_ref[...], h_lo, nt, preferred_element_type=F32)
                  + lax.dot_general(wrl_ref[...], h_hi, nt, preferred_element_type=F32)) + br_ref[...]
        cur = logits
        sel_idx, sel_val = [], []
        for _ in range(top_k):
            m = jnp.max(cur, axis=0, keepdims=True)
            idx = jnp.min(jnp.where(cur == m, eidx, E), axis=0, keepdims=True)
            sel_idx.append(idx)
            sel_val.append(m)
            cur = jnp.where(eidx == idx, -jnp.inf, cur)
        ex = [jnp.exp(val - sel_val[0]) for val in sel_val]
        denom = ex[0]
        for e_r in ex[1:]:
            denom = denom + e_r
        topi_ref[:, rows] = jnp.concatenate(sel_idx, axis=0)
        gate_ref[:, rows] = jnp.concatenate([e_r / denom for e_r in ex], axis=0)

        member = jnp.zeros((E, tp), F32)
        for idx in sel_idx:
            member = member + jnp.where(eidx == idx, 1.0, 0.0)
        rank_full = jnp.dot(member.astype(BF16), earlier, preferred_element_type=F32) + carry
        ranks = [jnp.sum(jnp.where(eidx == idx, rank_full, 0.0), axis=0, keepdims=True) for idx in sel_idx]
        rank_ref[:, rows] = jnp.concatenate(ranks, axis=0).astype(I32)
        carry = carry + jnp.sum(member, axis=1, keepdims=True)

    carry_ref[...] = carry
    cnt_ref[...] = jnp.broadcast_to(carry, cnt_ref.shape).astype(I32)


def _router(h1, wr_t, br):
    T, D = h1.shape
    E = wr_t.shape[0]
    wr_hi = wr_t.astype(BF16)
    wr_lo = (wr_t - wr_hi.astype(F32)).astype(BF16)
    tm = min(ROUTER_ROWS, T)
    full = lambda a: pl.BlockSpec(a.shape, lambda i: (0,) * a.ndim)
    tok = lambda: pl.BlockSpec((TOP_K, tm), lambda i: (0, i))
    return pl.pallas_call(
        functools.partial(_router_kernel, top_k=TOP_K, part_rows=min(ROUTER_PART_ROWS, tm)),
        grid=(T // tm,),
        in_specs=[pl.BlockSpec((tm, D), lambda i: (i, 0)), full(wr_hi), full(wr_lo), full(br)],
        out_specs=[tok(), tok(), tok(), pl.BlockSpec((E, LANES), lambda i: (0, 0))],
        out_shape=[
            jax.ShapeDtypeStruct((TOP_K, T), I32),
            jax.ShapeDtypeStruct((TOP_K, T), F32),
            jax.ShapeDtypeStruct((TOP_K, T), I32),
            jax.ShapeDtypeStruct((E, LANES), I32),
        ],
        scratch_shapes=[pltpu.VMEM((E, 1), F32)],
        compiler_params=_params("arbitrary"),
        name="router",
    )(h1, wr_hi, wr_lo, br)


def _dst_kernel(start_ref, topi_ref, rank_ref, dst_ref, *, n_experts):
    topi = topi_ref[...]
    dst = rank_ref[...]
    for e in range(n_experts):
        dst = dst + jnp.where(topi == e, start_ref[e], 0)
    dst_ref[...] = dst


def _dst_rows(pad_start, topi, rank):
    spec = pl.BlockSpec(topi.shape, lambda i, *_: (0, 0))
    return pl.pallas_call(
        functools.partial(_dst_kernel, n_experts=pad_start.shape[0]),
        grid_spec=pltpu.PrefetchScalarGridSpec(num_scalar_prefetch=1, grid=(1,), in_specs=[spec, spec], out_specs=spec),
        out_shape=jax.ShapeDtypeStruct(topi.shape, I32),
        compiler_params=_params("arbitrary"),
        name="dst",
    )(pad_start, topi, rank)


def _per_block_slots(dst, tb):
    K, T = dst.shape
    return dst.reshape(K, T // tb, tb).transpose(1, 0, 2).reshape(T // tb, 1, K * tb)


def _dispatch_kernel(cnt_ref, start_ref, meta_ref, dst_ref, h1p_ref, xs_ref, zero_ref, sem, zsem, *, top_k, sub):
    i = pl.program_id(0)
    tb = h1p_ref.shape[0]

    def padding_copies(e, act):
        pos = start_ref[e] + cnt_ref[e]
        n_single = (8 - pos % 8) % 8
        for u in range(7):
            @pl.when(u < n_single)
            def _():
                act(pltpu.make_async_copy(zero_ref.at[pl.ds(0, 1)], xs_ref.at[pl.ds(pos + u, 1)], zsem))

        pos8 = pos + n_single
        rem = start_ref[e] + (cnt_ref[e] + sub - 1) // sub * sub - pos8
        size = sub // 2
        while size >= 8:
            at = pl.multiple_of(pos8 + rem // (2 * size) * (2 * size), 8)

            @pl.when(rem % (2 * size) >= size)
            def _():
                act(pltpu.make_async_copy(zero_ref.at[pl.ds(0, size)], xs_ref.at[pl.ds(at, size)], zsem))

            size //= 2

    def tail_copy(sb):
        rows = pl.ds(pl.multiple_of(sb * sub, sub), sub)
        return pltpu.make_async_copy(zero_ref, xs_ref.at[rows], zsem)

    @pl.when(i == 0)
    def _():
        zero_ref[...] = jnp.zeros_like(zero_ref)
        n_experts = cnt_ref.shape[0]
        n_total = xs_ref.shape[0] // sub
        for act in (lambda copy: copy.start(), lambda copy: copy.wait()):
            def per_expert(e, carry, act=act):
                padding_copies(e, act)
                return carry

            def per_tail(sb, carry, act=act):
                act(tail_copy(sb))
                return carry

            lax.fori_loop(0, n_experts, per_expert, 0)
            lax.fori_loop(meta_ref[1], n_total, per_tail, 0)

    def start(c, carry):
        for u in range(DMA_ISSUE_UNROLL):
            r = c * DMA_ISSUE_UNROLL + u
            for k in range(top_k):
                row = dst_ref[0, k * tb + r]
                pltpu.make_async_copy(h1p_ref.at[pl.ds(r, 1)], xs_ref.at[pl.ds(row, 1)], sem).start()
        return carry

    lax.fori_loop(0, tb // DMA_ISSUE_UNROLL, start, 0)
    for k in range(top_k):
        pltpu.make_async_copy(h1p_ref, xs_ref.at[pl.ds(0, tb)], sem).wait()


def _dispatch(counts, pad_start, meta, dst, h1p, n_rows):
    T, W = h1p.shape
    tb = min(DISPATCH_ROWS, T)
    slots = _per_block_slots(dst, tb)
    return pl.pallas_call(
        functools.partial(_dispatch_kernel, top_k=dst.shape[0], sub=EXPERT_SUB_ROWS),
        grid_spec=pltpu.PrefetchScalarGridSpec(
            num_scalar_prefetch=3,
            grid=(T // tb,),
            in_specs=[
                pl.BlockSpec((None, 1, slots.shape[2]), lambda i, *_: (i, 0, 0), memory_space=pltpu.SMEM),
                pl.BlockSpec((tb, W), lambda i, *_: (i, 0)),
            ],
            out_specs=pl.BlockSpec(memory_space=pl.ANY),
            scratch_shapes=[pltpu.VMEM((EXPERT_SUB_ROWS, W), U32), pltpu.SemaphoreType.DMA(()),
                            pltpu.SemaphoreType.DMA(())],
        ),
        out_shape=jax.ShapeDtypeStruct((n_rows, W), U32),
        compiler_params=_params("arbitrary"),
        name="dispatch",
    )(counts, pad_start, meta, slots, h1p)


def _experts_kernel(ge_ref, row0_ref, nsub_ref, meta_ref, bu_ref, bd_ref, wup_ref, wdn_ref, xs_ref, ys_ref,
                    xbuf_ref, acc_ref, ybuf_ref, wgf_ref, wlf_ref, wdf_ref, wgb_ref, wlb_ref, wdb_ref, stage_ref,
                    wsem, xsem, ysem, *, n_f):
    s = pl.program_id(0)
    n_groups = meta_ref[0]
    sub, half = stage_ref.shape[1:]
    tf = wdf_ref.shape[0]
    ff = n_f * tf

    def y_copy(ib, row):
        return pltpu.make_async_copy(ybuf_ref.at[ib], ys_ref.at[pl.ds(pl.multiple_of(row, sub), sub)], ysem)

    def weight_copies(e, j):
        glu_cols = pl.ds(pl.multiple_of(j * tf, tf), tf)
        lin_cols = pl.ds(pl.multiple_of(ff + j * tf, tf), tf)
        return (pltpu.make_async_copy(wup_ref.at[e, :, glu_cols], wgf_ref, wsem.at[0]),
                pltpu.make_async_copy(wup_ref.at[e, :, lin_cols], wlf_ref, wsem.at[1]),
                pltpu.make_async_copy(wdn_ref.at[e, glu_cols, :], wdf_ref, wsem.at[2]))

    def cast_weights(slot):
        wgb_ref[slot] = wgf_ref[...].astype(BF16)
        wlb_ref[slot] = wlf_ref[...].astype(BF16)
        wdb_ref[slot] = wdf_ref[...].astype(BF16)

    @pl.when(s < n_groups)
    def _():
        e = ge_ref[s]
        e_after = ge_ref[jnp.minimum(s + 1, n_groups - 1)]
        n_sub = nsub_ref[s]
        row0 = row0_ref[s]

        def x_copy(ib, slot):
            rows = pl.ds(pl.multiple_of(row0 + ib * sub, sub), sub)
            return pltpu.make_async_copy(xs_ref.at[rows], stage_ref.at[slot], xsem.at[slot])

        x_copy(0, 0).start()

        @pl.when(s == 0)
        def _():
            for copy in weight_copies(e, 0):
                copy.start()
            for copy in weight_copies(e, 0):
                copy.wait()
            cast_weights(0)

        def item(j, carry):
            slot = j % 2
            is_last = j == n_f - 1
            next_j = jnp.where(is_last, 0, j + 1)
            next_e = jnp.where(is_last, e_after, e)
            for copy in weight_copies(next_e, next_j):
                copy.start()

            def load_tokens(ib):
                xslot = ib % 2
                x_copy(ib, xslot).wait()

                @pl.when(ib + 1 < n_sub)
                def _():
                    x_copy(ib + 1, 1 - xslot).start()

                lo, hi = _unpack_bf16_pair(stage_ref[xslot])
                xbuf_ref[ib, :, :half] = lo
                xbuf_ref[ib, :, half:] = hi
                acc_ref[ib] = jnp.broadcast_to(bd_ref[...], acc_ref.shape[1:])

            def accumulate(ib):
                x = xbuf_ref[ib]
                glu = jnp.dot(x, wgb_ref[slot], preferred_element_type=F32) + bu_ref[pl.ds(j, 1), :]
                lin = jnp.dot(x, wlb_ref[slot], preferred_element_type=F32) + bu_ref[pl.ds(n_f + j, 1), :]
                glu = jnp.minimum(glu, SWIGLU_LIMIT)
                lin = jnp.clip(lin, -SWIGLU_LIMIT, SWIGLU_LIMIT)
                act = glu / (1.0 + jnp.exp(-SWIGLU_ALPHA * glu)) * (lin + 1.0)
                acc_ref[ib] += jnp.dot(act.astype(BF16), wdb_ref[slot], preferred_element_type=F32)

            def process(ibs, with_next_weights):
                @pl.when(j == 0)
                def _():
                    for ib in ibs:
                        load_tokens(ib)

                if with_next_weights:
                    for copy in weight_copies(next_e, next_j):
                        copy.wait()
                    cast_weights(1 - slot)
                for ib in ibs:
                    accumulate(ib)

                @pl.when(is_last)
                def _():
                    for ib in ibs:
                        y = acc_ref[ib]
                        ybuf_ref[ib] = _pack_bf16_pair(y[:, :half], y[:, half:])
                        y_copy(ib, row0 + ib * sub).start()

            def pair(p, carry):
                process([2 * p, 2 * p + 1], False)
                return carry

            lax.fori_loop(0, (n_sub - 1) // 2, pair, 0)

            @pl.when(n_sub % 2 == 0)
            def _():
                process([n_sub - 2, n_sub - 1], True)

            @pl.when(n_sub % 2 == 1)
            def _():
                process([n_sub - 1], True)

            return carry

        lax.fori_loop(0, n_f, item, 0)

        def drain(ib, carry):
            y_copy(ib, row0 + ib * sub).wait()
            return carry

        lax.fori_loop(0, n_sub, drain, 0)

    @pl.when(s == pl.num_programs(0) - 1)
    def _():
        n_total = ys_ref.shape[0] // sub
        ybuf_ref[0] = jnp.zeros(ybuf_ref.shape[1:], U32)

        def fill(ib, carry):
            y_copy(0, ib * sub).start()
            return carry

        def drain(ib, carry):
            y_copy(0, ib * sub).wait()
            return carry

        lax.fori_loop(meta_ref[1], n_total, fill, 0)
        lax.fori_loop(meta_ref[1], n_total, drain, 0)


def _experts(group_expert, group_row0, group_nsub, meta, xs, w_up, b_up, w_down, b_down):
    R, half = xs.shape
    E, D, ff2 = w_up.shape
    ff = ff2 // 2
    sub = EXPERT_SUB_ROWS
    gb = EXPERT_GROUP_SUBS
    tf = min(EXPERT_FF_COLS, ff)
    n_f = ff // tf
    assert n_f % 2 == 0
    grp = lambda s, ge, r0, ns, meta: (ge[jnp.minimum(s, meta[0] - 1)], 0, 0)
    return pl.pallas_call(
        functools.partial(_experts_kernel, n_f=n_f),
        grid_spec=pltpu.PrefetchScalarGridSpec(
            num_scalar_prefetch=4,
            grid=(group_expert.shape[0],),
            in_specs=[
                pl.BlockSpec((None, 2 * n_f, tf), grp),
                pl.BlockSpec((None, 1, D), grp),
                pl.BlockSpec(memory_space=pl.ANY),
                pl.BlockSpec(memory_space=pl.ANY),
                pl.BlockSpec(memory_space=pl.ANY),
            ],
            out_specs=pl.BlockSpec(memory_space=pl.ANY),
            scratch_shapes=[
                pltpu.VMEM((gb, sub, D), BF16),
                pltpu.VMEM((gb, sub, D), F32),
                pltpu.VMEM((gb, sub, half), U32),
                pltpu.VMEM((D, tf), F32),
                pltpu.VMEM((D, tf), F32),
                pltpu.VMEM((tf, D), F32),
                pltpu.VMEM((2, D, tf), BF16),
                pltpu.VMEM((2, D, tf), BF16),
                pltpu.VMEM((2, tf, D), BF16),
                pltpu.VMEM((2, sub, half), U32),
                pltpu.SemaphoreType.DMA((3,)),
                pltpu.SemaphoreType.DMA((2,)),
                pltpu.SemaphoreType.DMA(()),
            ],
        ),
        out_shape=jax.ShapeDtypeStruct((R, half), U32),
        compiler_params=pltpu.CompilerParams(dimension_semantics=("arbitrary",),
                                             vmem_limit_bytes=EXPERT_VMEM_LIMIT_BYTES),
        name="experts",
    )(group_expert, group_row0, group_nsub, meta, b_up.reshape(E, 2 * n_f, tf), b_down.reshape(E, 1, D),
      w_up, w_down, xs)


def _combine_kernel(dst_ref, nxt_ref, gate_ref, h1_ref, lng_ref, lnb_ref, ys_ref, out_ref, buf0_ref, buf1_ref,
                    sem, *, top_k):
    i = pl.program_id(0)
    last = pl.num_programs(0) - 1
    tb = h1_ref.shape[0]
    half = buf0_ref.shape[2]

    def row_copy(slots_ref, buf_ref, slot, r, k):
        row = slots_ref[0, k * tb + r]
        return pltpu.make_async_copy(ys_ref.at[pl.ds(row, 1)], buf_ref.at[k, pl.ds(r, 1)], sem.at[slot])

    def wait_block(buf_ref, slot):
        for k in range(top_k):
            pltpu.make_async_copy(ys_ref.at[pl.ds(0, tb)], buf_ref.at[k], sem.at[slot]).wait()

    @pl.when(i == 0)
    def _():
        def start(c, carry):
            for u in range(DMA_ISSUE_UNROLL):
                for k in range(top_k):
                    row_copy(dst_ref, buf0_ref, 0, c * DMA_ISSUE_UNROLL + u, k).start()
            return carry

        lax.fori_loop(0, tb // DMA_ISSUE_UNROLL, start, 0)

    def step(cur_ref, cur, nxt_buf_ref, nxt):
        wait_block(cur_ref, cur)
        for r in range(tb):
            for k in range(top_k):
                row_copy(nxt_ref, nxt_buf_ref, nxt, r, k).start()

        diag = lax.broadcasted_iota(I32, (tb, tb), 0) == lax.broadcasted_iota(I32, (tb, tb), 1)
        ffn_lo = jnp.zeros((tb, half), F32)
        ffn_hi = jnp.zeros((tb, half), F32)
        for k in range(top_k):
            gate = jnp.sum(jnp.where(diag, gate_ref[k:k + 1, :], 0.0), axis=1, keepdims=True)
            lo, hi = _unpack_bf16_pair(cur_ref[k])
            ffn_lo = ffn_lo + lo.astype(F32) * gate
            ffn_hi = ffn_hi + hi.astype(F32) * gate
        ffn = jnp.concatenate([ffn_lo, ffn_hi], axis=1)
        out_ref[...] = _layer_norm(DEEPNORM_ALPHA * h1_ref[...] + ffn, lng_ref[...], lnb_ref[...])

        @pl.when(i == last)
        def _():
            wait_block(nxt_buf_ref, nxt)

    @pl.when(i % 2 == 0)
    def _():
        step(buf0_ref, 0, buf1_ref, 1)

    @pl.when(i % 2 == 1)
    def _():
        step(buf1_ref, 1, buf0_ref, 0)


def _combine(dst, gates, h1, ln_g, ln_b, ys):
    T, D = h1.shape
    K = dst.shape[0]
    tb = min(COMBINE_ROWS, T)
    slots = _per_block_slots(dst, tb)
    n_blocks = T // tb
    return pl.pallas_call(
        functools.partial(_combine_kernel, top_k=K),
        grid=(n_blocks,),
        in_specs=[
            pl.BlockSpec((None, 1, K * tb), lambda i: (i, 0, 0), memory_space=pltpu.SMEM),
            pl.BlockSpec((None, 1, K * tb), lambda i: (jnp.minimum(i + 1, n_blocks - 1), 0, 0),
                         memory_space=pltpu.SMEM),
            pl.BlockSpec((K, tb), lambda i: (0, i)),
            pl.BlockSpec((tb, D), lambda i: (i, 0)),
            pl.BlockSpec((1, D), lambda i: (0, 0)),
            pl.BlockSpec((1, D), lambda i: (0, 0)),
            pl.BlockSpec(memory_space=pl.ANY),
        ],
        out_specs=pl.BlockSpec((tb, D), lambda i: (i, 0)),
        out_shape=jax.ShapeDtypeStruct((T, D), F32),
        scratch_shapes=[pltpu.VMEM((K, tb, D // 2), U32), pltpu.VMEM((K, tb, D // 2), U32),
                        pltpu.SemaphoreType.DMA((2,))],
        compiler_params=_params("arbitrary"),
        name="combine",
    )(slots, slots, gates, h1, ln_g, ln_b, ys)


def _expert_schedule(counts, n_tokens_routed):
    sub, gb = EXPERT_SUB_ROWS, EXPERT_GROUP_SUBS
    n_experts = counts.shape[0]
    max_subs = n_tokens_routed // sub + n_experts
    max_groups = n_experts + max_subs // gb
    n_sub = (counts + sub - 1) // sub
    sub_end = jnp.cumsum(n_sub)
    sub_start = sub_end - n_sub
    n_grp = (n_sub + gb - 1) // gb
    grp_end = jnp.cumsum(n_grp)
    grp_start = grp_end - n_grp
    s = jnp.arange(max_groups, dtype=I32)
    expert = jnp.minimum(jnp.sum(grp_end[None, :] <= s[:, None], axis=1), n_experts - 1).astype(I32)
    onehot = (expert[:, None] == jnp.arange(n_experts, dtype=I32)[None, :]).astype(I32)
    pick = lambda v: jnp.sum(onehot * v[None, :], axis=1)
    local = s - pick(grp_start)
    row0 = (pick(sub_start) + local * gb) * sub
    nsub = jnp.clip(pick(n_sub) - local * gb, 0, gb)
    meta = jnp.stack([grp_end[-1], sub_end[-1]])
    to_i32 = lambda a: a.astype(I32)
    return to_i32(sub_start * sub), to_i32(expert), to_i32(row0), to_i32(nsub), to_i32(meta), max_subs * sub


def kernel(x, ln0_g, ln0_b, w_in, ret_decay_fwd, ret_decay_bwd, ret_gn_g, w_out, ln1_g, ln1_b, w_router, b_router,
           w_up, b_up, w_down, b_down, ln2_g, ln2_b):
    B, S, D = x.shape
    assert B == 1 and w_in.shape[0] == DEPTH
    T = S
    fw = D // 2
    rw = D - fw
    gw = fw // N_FOURIER_GROUPS
    head_dim = rw // N_RET_HEADS
    E = w_router.shape[-1]
    row = lambda a: a.reshape(1, -1)

    h, uf, ur = _in_proj(x.reshape(T, D), row(ln0_g), row(ln0_b), w_in[0].astype(BF16), head_dim)
    yf = _fourier_mix(uf, gw)
    lg_fwd = -jnp.exp(ret_decay_fwd[0].astype(F32))
    lg_bwd = -jnp.exp(ret_decay_bwd[0].astype(F32))
    o_f, o_b = _retention(ur, lg_fwd, lg_bwd, N_RET_HEADS, head_dim)
    h1, h1p = _out_proj(yf, o_f, o_b, ur, row(ret_gn_g[0]), h, w_out[0].astype(BF16), row(ln1_g[0]), row(ln1_b[0]),
                        N_RET_HEADS)
    topi, gates, rank, cnt = _router(h1, w_router[0].T, b_router[0].reshape(E, 1))

    counts = cnt[:, 0]
    pad_start, group_expert, group_row0, group_nsub, meta, n_rows = _expert_schedule(counts, T * TOP_K)
    dst = _dst_rows(pad_start, topi, rank)
    xs = _dispatch(counts, pad_start, meta, dst, h1p, n_rows)
    ys = _experts(group_expert, group_row0, group_nsub, meta, xs, w_up.reshape(w_up.shape[1:]), b_up[0],
                  w_down.reshape(w_down.shape[1:]), b_down[0])
    out = _combine(dst, gates, h1, row(ln2_g[0]), row(ln2_b[0]), ys)
    return out.reshape(B, S, D)
```

```python
import functools

import numpy as np
import jax
import jax.numpy as jnp
from jax import lax
from jax.experimental import pallas as pl
from jax.experimental.pallas import tpu as pltpu

F32 = jnp.float32
BF16 = jnp.bfloat16
I32 = jnp.int32
U32 = jnp.uint32

N_FOURIER_GROUPS = 8
N_RET_HEADS = 4
ROPE_BASE = 10000.0
TOP_K = 4
SWIGLU_LIMIT = 7.0
SWIGLU_ALPHA = 1.702
LN_EPS = 1e-5
GN_EPS = 1e-6
DEPTH = 1
DEEPNORM_ALPHA = (2.0 * DEPTH) ** 0.25

V7X_VMEM_BYTES = 64 * 2**20
VMEM_LIMIT_BYTES = V7X_VMEM_BYTES * 3 // 4
LANES = 128

IN_PROJ_ROWS = 512
IN_PROJ_PART_ROWS = 256
FOURIER_COLS_PER_STEP = 4
RET_CHUNK = 256
OUT_PROJ_ROWS = 512
OUT_PROJ_PART_ROWS = 256
ROUTER_ROWS = 1024
ROUTER_PART_ROWS = 256
DISPATCH_ROWS = 256
EXPERT_SUB_ROWS = 256
EXPERT_GROUP_SUBS = 9
EXPERT_FF_COLS = 256
EXPERT_VMEM_LIMIT_BYTES = V7X_VMEM_BYTES * 7 // 8
COMBINE_ROWS = 128
DMA_ISSUE_UNROLL = 8


def _params(*semantics):
    return pltpu.CompilerParams(dimension_semantics=semantics, vmem_limit_bytes=VMEM_LIMIT_BYTES)


def _layer_norm(x, g, b):
    mu = jnp.mean(x, axis=-1, keepdims=True)
    xc = x - mu
    var = jnp.mean(xc * xc, axis=-1, keepdims=True)
    return xc * lax.rsqrt(var + LN_EPS) * g + b


def _in_proj_kernel(x_ref, g_ref, b_ref, w_ref, base_ref, cos_ref, sin_ref, h_ref, uf_ref, ur_ref, hb_ref, *,
                    head_dim, part_rows):
    j = pl.program_id(1)
    tm = x_ref.shape[0]
    parts = [slice(r0, r0 + part_rows) for r0 in range(0, tm, part_rows)]

    @pl.when(j == 0)
    def _():
        for rows in parts:
            h = _layer_norm(x_ref[rows, :], g_ref[...], b_ref[...])
            h_ref[rows, :] = h
            hb = h.astype(BF16)
            hb_ref[rows, :] = hb
            uf_ref[rows, :] = jnp.dot(hb, w_ref[...], preferred_element_type=F32).astype(BF16)

    @pl.when((j == 1) | (j == 2))
    def _():
        half = head_dim // 2
        cos0, sin0 = base_ref[0:1, :], base_ref[1:2, :]
        for rows in parts:
            acc = jnp.dot(hb_ref[rows, :], w_ref[...], preferred_element_type=F32)
            cos = cos0 * cos_ref[rows, :] - sin0 * sin_ref[rows, :]
            sin = sin0 * cos_ref[rows, :] + cos0 * sin_ref[rows, :]
            for lo in range(0, acc.shape[1], head_dim):
                x1 = acc[:, lo:lo + half]
                x2 = acc[:, lo + half:lo + head_dim]
                ur_ref[rows, lo:lo + half] = (x1 * cos - x2 * sin).astype(BF16)
                ur_ref[rows, lo + half:lo + head_dim] = (x2 * cos + x1 * sin).astype(BF16)

    @pl.when(j >= 3)
    def _():
        for rows in parts:
            ur_ref[rows, :] = jnp.dot(hb_ref[rows, :], w_ref[...], preferred_element_type=F32).astype(BF16)


def _in_proj(x2, ln_g, ln_b, w_bf, head_dim):
    T, D = x2.shape
    tn = D // 2
    n_col = w_bf.shape[1] // tn
    assert n_col == 5 and w_bf.shape[1] == 5 * tn
    tm = min(IN_PROJ_ROWS, T)
    half = head_dim // 2
    inv = ROPE_BASE ** (-jnp.arange(half, dtype=F32) / half)
    ang0 = (jnp.arange(T // tm, dtype=I32) * tm).astype(F32)[:, None] * inv[None, :]
    ang1 = jnp.arange(tm, dtype=I32).astype(F32)[:, None] * inv[None, :]
    base = jnp.stack([jnp.cos(ang0), jnp.sin(ang0)], axis=1)
    cos, sin = jnp.cos(ang1), jnp.sin(ang1)
    return pl.pallas_call(
        functools.partial(_in_proj_kernel, head_dim=head_dim, part_rows=min(IN_PROJ_PART_ROWS, tm)),
        grid=(T // tm, n_col),
        in_specs=[
            pl.BlockSpec((tm, D), lambda i, j: (i, 0)),
            pl.BlockSpec((1, D), lambda i, j: (0, 0)),
            pl.BlockSpec((1, D), lambda i, j: (0, 0)),
            pl.BlockSpec((D, tn), lambda i, j: (0, j)),
            pl.BlockSpec((None, 2, half), lambda i, j: (i, 0, 0)),
            pl.BlockSpec((tm, half), lambda i, j: (0, 0)),
            pl.BlockSpec((tm, half), lambda i, j: (0, 0)),
        ],
        out_specs=[
            pl.BlockSpec((tm, D), lambda i, j: (i, 0)),
            pl.BlockSpec((tm, tn), lambda i, j: (i, 0)),
            pl.BlockSpec((tm, tn), lambda i, j: (i, jnp.maximum(j - 1, 0))),
        ],
        out_shape=[
            jax.ShapeDtypeStruct((T, D), F32),
            jax.ShapeDtypeStruct((T, tn), BF16),
            jax.ShapeDtypeStruct((T, 4 * tn), BF16),
        ],
        scratch_shapes=[pltpu.VMEM((tm, D), BF16)],
        compiler_params=_params("parallel", "arbitrary"),
        name="in_proj",
    )(x2, ln_g, ln_b, w_bf, base, cos, sin)


def _fourier_tables(n1, n2, gw):
    def cos_sin(n, r, c):
        ang = 2.0 * np.pi * ((np.outer(r, c) % n).astype(np.float64)) / n
        return np.cos(ang), np.sin(ang)

    cg, sg = cos_sin(gw, np.arange(gw), np.arange(gw))
    c1, s1 = cos_sin(n1, np.arange(n1), np.arange(n1))
    c3, s3 = cos_sin(n2, np.arange(n2), np.arange(n2))
    tc, ts = cos_sin(n1 * n2, np.arange(n1), np.arange(n2))
    cs = np.concatenate([cg, sg], axis=1)
    m1 = np.block([[c1, -s1], [-s1, -c1]])
    m3 = np.concatenate([c3, s3], axis=1)
    as32 = lambda a: jnp.asarray(a.astype(np.float32))
    return as32(cs).astype(BF16), as32(m1).astype(BF16), as32(m3).astype(BF16), as32(tc), as32(ts)


def _fourier_a_kernel(uf_ref, cs_ref, m1_ref, twc_ref, tws_ref, zz_ref, *, nb, gw):
    bb = pl.program_id(0)
    n1 = uf_ref.shape[0]
    width = uf_ref.shape[1] // nb
    cs = cs_ref[...]
    m1 = m1_ref[...]
    lane = lax.broadcasted_iota(I32, twc_ref.shape, 1)
    for t in range(nb):
        x = uf_ref[:, t * width:(t + 1) * width]
        a_parts, b_parts = [], []
        for lo in range(0, width, gw):
            ab = jnp.dot(x[:, lo:lo + gw], cs, preferred_element_type=F32)
            a_parts.append(ab[:, :gw])
            b_parts.append(ab[:, gw:])
        v = jnp.concatenate([jnp.concatenate(a_parts, axis=1), jnp.concatenate(b_parts, axis=1)], axis=0)
        z = jnp.dot(m1, v.astype(BF16), preferred_element_type=F32)
        zr, zi = z[:n1], z[n1:]
        sel = lane == bb * nb + t
        tc = jnp.sum(jnp.where(sel, twc_ref[...], 0.0), axis=1, keepdims=True)
        ts = jnp.sum(jnp.where(sel, tws_ref[...], 0.0), axis=1, keepdims=True)
        base = t * 2 * width
        zz_ref[:, base:base + width] = (zr * tc + zi * ts).astype(BF16)
        zz_ref[:, base + width:base + 2 * width] = (zi * tc - zr * ts).astype(BF16)


def _fourier_b_kernel(zz_ref, m3_ref, yf_ref, *, scale):
    nc = zz_ref.shape[0]
    width = zz_ref.shape[2] // 2
    m3 = m3_ref[...]
    for t in range(nc):
        z = zz_ref[t]
        stacked = jnp.concatenate([z[:, :width], z[:, width:]], axis=0)
        x = jnp.dot(m3, stacked, preferred_element_type=F32)
        yf_ref[:, t * width:(t + 1) * width] = (x * scale).astype(BF16)


def _fourier_mix(uf, gw):
    T, F = uf.shape
    n1 = min(LANES, T)
    n2 = T // n1
    assert n1 * n2 == T
    cs, m1, m3, twc, tws = _fourier_tables(n1, n2, gw)
    nb = min(FOURIER_COLS_PER_STEP, n2)
    nc = min(FOURIER_COLS_PER_STEP, n1)
    zz = pl.pallas_call(
        functools.partial(_fourier_a_kernel, nb=nb, gw=gw),
        grid=(n2 // nb,),
        in_specs=[
            pl.BlockSpec((n1, nb * F), lambda b: (0, b)),
            pl.BlockSpec(cs.shape, lambda b: (0, 0)),
            pl.BlockSpec(m1.shape, lambda b: (0, 0)),
            pl.BlockSpec(twc.shape, lambda b: (0, 0)),
            pl.BlockSpec(tws.shape, lambda b: (0, 0)),
        ],
        out_specs=pl.BlockSpec((n1, nb * 2 * F), lambda b: (0, b)),
        out_shape=jax.ShapeDtypeStruct((n1, n2 * 2 * F), BF16),
        compiler_params=_params("parallel"),
        name="fourier_a",
    )(uf.reshape(n1, n2 * F), cs, m1, twc, tws)
    yf = pl.pallas_call(
        functools.partial(_fourier_b_kernel, scale=float(1.0 / np.sqrt(T * gw))),
        grid=(n1 // nc,),
        in_specs=[
            pl.BlockSpec((nc, n2, 2 * F), lambda c: (c, 0, 0)),
            pl.BlockSpec(m3.shape, lambda c: (0, 0)),
        ],
        out_specs=pl.BlockSpec((n2, nc * F), lambda c: (0, c)),
        out_shape=jax.ShapeDtypeStruct((n2, n1 * F), BF16),
        compiler_params=_params("parallel"),
        name="fourier_b",
    )(zz.reshape(n1, n2, 2 * F), m3)
    return yf.reshape(T, F)


_TAB_K_TO_END_F, _TAB_Q_FROM_START_F, _TAB_K_TO_END_B, _TAB_Q_FROM_START_B, _TAB_CHUNK_F, _TAB_CHUNK_B = range(6)


def _retention_kernel(lgf_ref, lgb_ref, qf_ref, kf_ref, vf_ref, qb_ref, kb_ref, vb_ref, of_ref, ob_ref,
                      sf_ref, sb_ref, dmat_ref, tab_ref, *, kscale, n_heads):
    n = pl.program_id(0)
    C = qf_ref.shape[0]
    hd = qf_ref.shape[1] // n_heads

    @pl.when(n == 0)
    def _():
        r = lax.broadcasted_iota(I32, (C, C), 0).astype(F32)
        c = lax.broadcasted_iota(I32, (C, C), 1).astype(F32)
        diff = r - c
        rr = lax.broadcasted_iota(I32, (C, hd), 0).astype(F32)
        for head in range(n_heads):
            lgf = lgf_ref[head]
            lgb = lgb_ref[head]
            dmat_ref[head] = kscale * jnp.where(diff >= 0.0, jnp.exp(lgf * jnp.maximum(diff, 0.0)),
                                                jnp.exp(lgb * jnp.maximum(-diff, 0.0)))
            tab_ref[head, _TAB_K_TO_END_F] = kscale * jnp.exp(lgf * (C - 1.0 - rr))
            tab_ref[head, _TAB_Q_FROM_START_F] = jnp.exp(lgf * (rr + 1.0))
            tab_ref[head, _TAB_K_TO_END_B] = kscale * jnp.exp(lgb * rr)
            tab_ref[head, _TAB_Q_FROM_START_B] = jnp.exp(lgb * (C - rr))
            tab_ref[head, _TAB_CHUNK_F] = jnp.exp(jnp.full((C, hd), lgf * C, F32))
            tab_ref[head, _TAB_CHUNK_B] = jnp.exp(jnp.full((C, hd), lgb * C, F32))
        sf_ref[...] = jnp.zeros_like(sf_ref)
        sb_ref[...] = jnp.zeros_like(sb_ref)

    nt = (((1,), (1,)), ((), ()))
    tn = (((0,), (0,)), ((), ()))

    for head in range(n_heads):
        cols = slice(head * hd, (head + 1) * hd)
        q = qf_ref[:, cols]
        k = kf_ref[:, cols]
        v = vf_ref[:, cols]
        scores = lax.dot_general(q, k, nt, preferred_element_type=F32) * dmat_ref[head]
        intra = jnp.dot(scores.astype(BF16), v, preferred_element_type=F32)
        cross = (jnp.dot(q, sf_ref[head].astype(BF16), preferred_element_type=F32)
                 * tab_ref[head, _TAB_Q_FROM_START_F])
        of_ref[:, cols] = intra + cross
        v_dec = (v.astype(F32) * tab_ref[head, _TAB_K_TO_END_F]).astype(BF16)
        sf_ref[head] = (sf_ref[head] * tab_ref[head, _TAB_CHUNK_F]
                        + lax.dot_general(k, v_dec, tn, preferred_element_type=F32))

        q = qb_ref[:, cols]
        k = kb_ref[:, cols]
        v = vb_ref[:, cols]
        ob_ref[:, cols] = (jnp.dot(q, sb_ref[head].astype(BF16), preferred_element_type=F32)
                           * tab_ref[head, _TAB_Q_FROM_START_B])
        v_dec = (v.astype(F32) * tab_ref[head, _TAB_K_TO_END_B]).astype(BF16)
        sb_ref[head] = (sb_ref[head] * tab_ref[head, _TAB_CHUNK_B]
                        + lax.dot_general(k, v_dec, tn, preferred_element_type=F32))


def _retention(ur, lg_fwd, lg_bwd, n_heads, head_dim):
    T = ur.shape[0]
    C = min(RET_CHUNK, T)
    assert C == head_dim
    N = T // C
    H = n_heads
    rw = H * head_dim
    fwd = lambda col: (lambda n, *_: (n, col))
    bwd = lambda col: (lambda n, *_: (N - 1 - n, col))
    blk = lambda imap: pl.BlockSpec((C, rw), imap)
    return pl.pallas_call(
        functools.partial(_retention_kernel, kscale=float(head_dim) ** -0.5, n_heads=H),
        grid_spec=pltpu.PrefetchScalarGridSpec(
            num_scalar_prefetch=2,
            grid=(N,),
            in_specs=[blk(fwd(0)), blk(fwd(1)), blk(fwd(2)), blk(bwd(0)), blk(bwd(1)), blk(bwd(2))],
            out_specs=[blk(fwd(0)), blk(bwd(0))],
            scratch_shapes=[
                pltpu.VMEM((H, head_dim, head_dim), F32),
                pltpu.VMEM((H, head_dim, head_dim), F32),
                pltpu.VMEM((H, C, C), F32),
                pltpu.VMEM((H, 6, C, head_dim), F32),
            ],
        ),
        out_shape=[jax.ShapeDtypeStruct((T, rw), F32)] * 2,
        compiler_params=_params("arbitrary"),
        name="retention",
    )(lg_fwd, lg_bwd, ur, ur, ur, ur, ur, ur)


def _pack_bf16_pair(lo, hi):
    lo_bits = lax.bitcast_convert_type(lo.astype(BF16).astype(F32), U32)
    hi_bits = lax.bitcast_convert_type(hi.astype(BF16).astype(F32), U32)
    return (hi_bits & jnp.uint32(0xFFFF0000)) | (lo_bits >> 16)


def _unpack_bf16_pair(words):
    lo = lax.bitcast_convert_type(words << 16, F32).astype(BF16)
    hi = lax.bitcast_convert_type(words & jnp.uint32(0xFFFF0000), F32).astype(BF16)
    return lo, hi


def _out_proj_kernel(yf_ref, of_ref, ob_ref, g_ref, gng_ref, h_ref, wo_ref, lng_ref, lnb_ref, h1_ref, h1p_ref, *,
                     n_heads, part_rows):
    tm, rw = of_ref.shape
    hd = rw // n_heads
    fw = yf_ref.shape[1]
    D = h_ref.shape[1]
    for r0 in range(0, tm, part_rows):
        rows = slice(r0, r0 + part_rows)
        o = of_ref[rows, :] + ob_ref[rows, :]
        parts = []
        for lo in range(0, rw, hd):
            oh = o[:, lo:lo + hd]
            mu = jnp.mean(oh, axis=-1, keepdims=True)
            d = oh - mu
            var = jnp.mean(d * d, axis=-1, keepdims=True)
            parts.append(d * lax.rsqrt(var + GN_EPS))
        on = jnp.concatenate(parts, axis=-1) * gng_ref[...]
        g = g_ref[rows, :].astype(F32)
        yr = (g / (1.0 + jnp.exp(-g))) * on

        mix = (jnp.dot(yf_ref[rows, :], wo_ref[:fw, :], preferred_element_type=F32)
               + jnp.dot(yr.astype(BF16), wo_ref[fw:, :], preferred_element_type=F32))
        h1 = _layer_norm(DEEPNORM_ALPHA * h_ref[rows, :] + mix, lng_ref[...], lnb_ref[...])
        h1_ref[rows, :] = h1
        h1p_ref[rows, :] = _pack_bf16_pair(h1[:, :D // 2], h1[:, D // 2:])


def _out_proj(yf, o_f, o_b, ur, gn_g, h, wo_bf, ln_g, ln_b, n_heads):
    T, D = h.shape
    fw = yf.shape[1]
    rw = o_f.shape[1]
    tm = min(OUT_PROJ_ROWS, T)
    gate_col = ur.shape[1] // rw - 1
    row = lambda w: pl.BlockSpec((tm, w), lambda i: (i, 0))
    full = lambda a: pl.BlockSpec(a.shape, lambda i: (0,) * a.ndim)
    return pl.pallas_call(
        functools.partial(_out_proj_kernel, n_heads=n_heads, part_rows=min(OUT_PROJ_PART_ROWS, tm)),
        grid=(T // tm,),
        in_specs=[row(fw), row(rw), row(rw), pl.BlockSpec((tm, rw), lambda i: (i, gate_col)), full(gn_g), row(D),
                  full(wo_bf), full(ln_g), full(ln_b)],
        out_specs=[row(D), row(D // 2)],
        out_shape=[jax.ShapeDtypeStruct((T, D), F32), jax.ShapeDtypeStruct((T, D // 2), U32)],
        compiler_params=_params("parallel"),
        name="out_proj",
    )(yf, o_f, o_b, ur, gn_g, h, wo_bf, ln_g, ln_b)


def _router_kernel(h1_ref, wr_ref, wrl_ref, br_ref, topi_ref, gate_ref, rank_ref, cnt_ref, carry_ref, *, top_k,
                   part_rows):
    i = pl.program_id(0)
    tm = h1_ref.shape[0]
    E = wr_ref.shape[0]
    tp = part_rows

    @pl.when(i == 0)
    def _():
        carry_ref[...] = jnp.zeros_like(carry_ref)

    eidx = lax.broadcasted_iota(I32, (E, tp), 0)
    earlier = lax.broadcasted_iota(I32, (tp, tp), 0) < lax.broadcasted_iota(I32, (tp, tp), 1)
    earlier = jnp.where(earlier, 1.0, 0.0).astype(BF16)
    carry = carry_ref[...]

    for r0 in range(0, tm, tp):
        rows = slice(r0, r0 + tp)
        h = h1_ref[rows, :]
        h_hi = h.astype(BF16)
        h_lo = (h - h_hi.astype(F32)).astype(BF16)
        nt = (((1,), (1,)), ((), ()))
        logits = (lax.dot_general(wr_ref[...], h_hi, nt, preferred_element_type=F32)
                  + lax.dot_general(wr_ref[...], h_lo, nt, preferred_element_type=F32)
                  + lax.dot_general(wrl_ref[...], h_hi, nt, preferred_element_type=F32)) + br_ref[...]
        cur = logits
        sel_idx, sel_val = [], []
        for _ in range(top_k):
            m = jnp.max(cur, axis=0, keepdims=True)
            idx = jnp.min(jnp.where(cur == m, eidx, E), axis=0, keepdims=True)
            sel_idx.append(idx)
            sel_val.append(m)
            cur = jnp.where(eidx == idx, -jnp.inf, cur)
        ex = [jnp.exp(val - sel_val[0]) for val in sel_val]
        denom = ex[0]
        for e_r in ex[1:]:
            denom = denom + e_r
        topi_ref[:, rows] = jnp.concatenate(sel_idx, axis=0)
        gate_ref[:, rows] = jnp.concatenate([e_r / denom for e_r in ex], axis=0)

        member = jnp.zeros((E, tp), F32)
        for idx in sel_idx:
            member = member + jnp.where(eidx == idx, 1.0, 0.0)
        rank_full = jnp.dot(member.astype(BF16), earlier, preferred_element_type=F32) + carry
        ranks = [jnp.sum(jnp.where(eidx == idx, rank_full, 0.0), axis=0, keepdims=True) for idx in sel_idx]
        rank_ref[:, rows] = jnp.concatenate(ranks, axis=0).astype(I32)
        carry = carry + jnp.sum(member, axis=1, keepdims=True)

    carry_ref[...] = carry
    cnt_ref[...] = jnp.broadcast_to(carry, cnt_ref.shape).astype(I32)


def _router(h1, wr_t, br):
    T, D = h1.shape
    E = wr_t.shape[0]
    wr_hi = wr_t.astype(BF16)
    wr_lo = (wr_t - wr_hi.astype(F32)).astype(BF16)
    tm = min(ROUTER_ROWS, T)
    full = lambda a: pl.BlockSpec(a.shape, lambda i: (0,) * a.ndim)
    tok = lambda: pl.BlockSpec((TOP_K, tm), lambda i: (0, i))
    return pl.pallas_call(
        functools.partial(_router_kernel, top_k=TOP_K, part_rows=min(ROUTER_PART_ROWS, tm)),
        grid=(T // tm,),
        in_specs=[pl.BlockSpec((tm, D), lambda i: (i, 0)), full(wr_hi), full(wr_lo), full(br)],
        out_specs=[tok(), tok(), tok(), pl.BlockSpec((E, LANES), lambda i: (0, 0))],
        out_shape=[
            jax.ShapeDtypeStruct((TOP_K, T), I32),
            jax.ShapeDtypeStruct((TOP_K, T), F32),
            jax.ShapeDtypeStruct((TOP_K, T), I32),
            jax.ShapeDtypeStruct((E, LANES), I32),
        ],
        scratch_shapes=[pltpu.VMEM((E, 1), F32)],
        compiler_params=_params("arbitrary"),
        name="router",
    )(h1, wr_hi, wr_lo, br)


def _dst_kernel(start_ref, topi_ref, rank_ref, dst_ref, *, n_experts):
    topi = topi_ref[...]
    dst = rank_ref[...]
    for e in range(n_experts):
        dst = dst + jnp.where(topi == e, start_ref[e], 0)
    dst_ref[...] = dst


def _dst_rows(pad_start, topi, rank):
    spec = pl.BlockSpec(topi.shape, lambda i, *_: (0, 0))
    return pl.pallas_call(
        functools.partial(_dst_kernel, n_experts=pad_start.shape[0]),
        grid_spec=pltpu.PrefetchScalarGridSpec(num_scalar_prefetch=1, grid=(1,), in_specs=[spec, spec], out_specs=spec),
        out_shape=jax.ShapeDtypeStruct(topi.shape, I32),
        compiler_params=_params("arbitrary"),
        name="dst",
    )(pad_start, topi, rank)


def _per_block_slots(dst, tb):
    K, T = dst.shape
    return dst.reshape(K, T // tb, tb).transpose(1, 0, 2).reshape(T // tb, 1, K * tb)


def _dispatch_kernel(cnt_ref, start_ref, meta_ref, dst_ref, h1p_ref, xs_ref, zero_ref, sem, zsem, *, top_k, sub):
    i = pl.program_id(0)
    tb = h1p_ref.shape[0]

    def padding_copies(e, act):
        pos = start_ref[e] + cnt_ref[e]
        n_single = (8 - pos % 8) % 8
        for u in range(7):
            @pl.when(u < n_single)
            def _():
                act(pltpu.make_async_copy(zero_ref.at[pl.ds(0, 1)], xs_ref.at[pl.ds(pos + u, 1)], zsem))

        pos8 = pos + n_single
        rem = start_ref[e] + (cnt_ref[e] + sub - 1) // sub * sub - pos8
        size = sub // 2
        while size >= 8:
            at = pl.multiple_of(pos8 + rem // (2 * size) * (2 * size), 8)

            @pl.when(rem % (2 * size) >= size)
            def _():
                act(pltpu.make_async_copy(zero_ref.at[pl.ds(0, size)], xs_ref.at[pl.ds(at, size)], zsem))

            size //= 2

    def tail_copy(sb):
        rows = pl.ds(pl.multiple_of(sb * sub, sub), sub)
        return pltpu.make_async_copy(zero_ref, xs_ref.at[rows], zsem)

    @pl.when(i == 0)
    def _():
        zero_ref[...] = jnp.zeros_like(zero_ref)
        n_experts = cnt_ref.shape[0]
        n_total = xs_ref.shape[0] // sub
        for act in (lambda copy: copy.start(), lambda copy: copy.wait()):
            def per_expert(e, carry, act=act):
                padding_copies(e, act)
                return carry

            def per_tail(sb, carry, act=act):
                act(tail_copy(sb))
                return carry

            lax.fori_loop(0, n_experts, per_expert, 0)
            lax.fori_loop(meta_ref[1], n_total, per_tail, 0)

    def start(c, carry):
        for u in range(DMA_ISSUE_UNROLL):
            r = c * DMA_ISSUE_UNROLL + u
            for k in range(top_k):
                row = dst_ref[0, k * tb + r]
                pltpu.make_async_copy(h1p_ref.at[pl.ds(r, 1)], xs_ref.at[pl.ds(row, 1)], sem).start(priority=k % 2)
        return carry

    lax.fori_loop(0, tb // DMA_ISSUE_UNROLL, start, 0)
    for k in range(top_k):
        pltpu.make_async_copy(h1p_ref, xs_ref.at[pl.ds(0, tb)], sem).wait()


def _dispatch(counts, pad_start, meta, dst, h1p, n_rows):
    T, W = h1p.shape
    tb = min(DISPATCH_ROWS, T)
    slots = _per_block_slots(dst, tb)
    return pl.pallas_call(
        functools.partial(_dispatch_kernel, top_k=dst.shape[0], sub=EXPERT_SUB_ROWS),
        grid_spec=pltpu.PrefetchScalarGridSpec(
            num_scalar_prefetch=3,
            grid=(T // tb,),
            in_specs=[
                pl.BlockSpec((None, 1, slots.shape[2]), lambda i, *_: (i, 0, 0), memory_space=pltpu.SMEM),
                pl.BlockSpec((tb, W), lambda i, *_: (i, 0)),
            ],
            out_specs=pl.BlockSpec(memory_space=pl.ANY),
            scratch_shapes=[pltpu.VMEM((EXPERT_SUB_ROWS, W), U32), pltpu.SemaphoreType.DMA(()),
                            pltpu.SemaphoreType.DMA(())],
        ),
        out_shape=jax.ShapeDtypeStruct((n_rows, W), U32),
        compiler_params=_params("arbitrary"),
        name="dispatch",
    )(counts, pad_start, meta, slots, h1p)


def _experts_kernel(ge_ref, row0_ref, nsub_ref, meta_ref, bu_ref, bd_ref, wup_ref, wdn_ref, xs_ref, ys_ref,
                    xbuf_ref, acc_ref, ybuf_ref, wgf_ref, wlf_ref, wdf_ref, wgb_ref, wlb_ref, wdb_ref, stage_ref,
                    wsem, xsem, ysem, *, n_f):
    s = pl.program_id(0)
    n_groups = meta_ref[0]
    sub, half = stage_ref.shape[1:]
    tf = wdf_ref.shape[0]
    ff = n_f * tf

    def y_copy(ib, row):
        return pltpu.make_async_copy(ybuf_ref.at[ib], ys_ref.at[pl.ds(pl.multiple_of(row, sub), sub)], ysem)

    def weight_copies(e, j):
        glu_cols = pl.ds(pl.multiple_of(j * tf, tf), tf)
        lin_cols = pl.ds(pl.multiple_of(ff + j * tf, tf), tf)
        return (pltpu.make_async_copy(wup_ref.at[e, :, glu_cols], wgf_ref, wsem.at[0]),
                pltpu.make_async_copy(wup_ref.at[e, :, lin_cols], wlf_ref, wsem.at[1]),
                pltpu.make_async_copy(wdn_ref.at[e, glu_cols, :], wdf_ref, wsem.at[2]))

    def cast_weights(slot):
        wgb_ref[slot] = wgf_ref[...].astype(BF16)
        wlb_ref[slot] = wlf_ref[...].astype(BF16)
        wdb_ref[slot] = wdf_ref[...].astype(BF16)

    @pl.when(s < n_groups)
    def _():
        e = ge_ref[s]
        e_after = ge_ref[jnp.minimum(s + 1, n_groups - 1)]
        n_sub = nsub_ref[s]
        row0 = row0_ref[s]

        def x_copy(ib, slot):
            rows = pl.ds(pl.multiple_of(row0 + ib * sub, sub), sub)
            return pltpu.make_async_copy(xs_ref.at[rows], stage_ref.at[slot], xsem.at[slot])

        x_copy(0, 0).start()

        @pl.when(s == 0)
        def _():
            for copy in weight_copies(e, 0):
                copy.start()
            for copy in weight_copies(e, 0):
                copy.wait()
            cast_weights(0)

        def item(j, carry):
            slot = j % 2
            is_last = j == n_f - 1
            next_j = jnp.where(is_last, 0, j + 1)
            next_e = jnp.where(is_last, e_after, e)
            for copy in weight_copies(next_e, next_j):
                copy.start()

            def load_tokens(ib):
                xslot = ib % 2
                x_copy(ib, xslot).wait()

                @pl.when(ib + 1 < n_sub)
                def _():
                    x_copy(ib + 1, 1 - xslot).start()

                lo, hi = _unpack_bf16_pair(stage_ref[xslot])
                xbuf_ref[ib, :, :half] = lo
                xbuf_ref[ib, :, half:] = hi
                acc_ref[ib] = jnp.broadcast_to(bd_ref[...], acc_ref.shape[1:])

            def accumulate(ib):
                x = xbuf_ref[ib]
                glu = jnp.dot(x, wgb_ref[slot], preferred_element_type=F32) + bu_ref[pl.ds(j, 1), :]
                lin = jnp.dot(x, wlb_ref[slot], preferred_element_type=F32) + bu_ref[pl.ds(n_f + j, 1), :]
                glu = jnp.minimum(glu, SWIGLU_LIMIT)
                lin = jnp.clip(lin, -SWIGLU_LIMIT, SWIGLU_LIMIT)
                act = glu / (1.0 + jnp.exp(-SWIGLU_ALPHA * glu)) * (lin + 1.0)
                acc_ref[ib] += jnp.dot(act.astype(BF16), wdb_ref[slot], preferred_element_type=F32)

            def process(ibs, with_next_weights):
                @pl.when(j == 0)
                def _():
                    for ib in ibs:
                        load_tokens(ib)

                if with_next_weights:
                    for copy in weight_copies(next_e, next_j):
                        copy.wait()
                    cast_weights(1 - slot)
                for ib in ibs:
                    accumulate(ib)

                @pl.when(is_last)
                def _():
                    for ib in ibs:
                        y = acc_ref[ib]
                        ybuf_ref[ib] = _pack_bf16_pair(y[:, :half], y[:, half:])
                        y_copy(ib, row0 + ib * sub).start()

            def pair(p, carry):
                process([2 * p, 2 * p + 1], False)
                return carry

            lax.fori_loop(0, (n_sub - 1) // 2, pair, 0)

            @pl.when(n_sub % 2 == 0)
            def _():
                process([n_sub - 2, n_sub - 1], True)

            @pl.when(n_sub % 2 == 1)
            def _():
                process([n_sub - 1], True)

            return carry

        lax.fori_loop(0, n_f, item, 0)

        def drain(ib, carry):
            y_copy(ib, row0 + ib * sub).wait()
            return carry

        lax.fori_loop(0, n_sub, drain, 0)

    @pl.when(s == pl.num_programs(0) - 1)
    def _():
        n_total = ys_ref.shape[0] // sub
        ybuf_ref[0] = jnp.zeros(ybuf_ref.shape[1:], U32)

        def fill(ib, carry):
            y_copy(0, ib * sub).start()
            return carry

        def drain(ib, carry):
            y_copy(0, ib * sub).wait()
            return carry

        lax.fori_loop(meta_ref[1], n_total, fill, 0)
        lax.fori_loop(meta_ref[1], n_total, drain, 0)


def _experts(group_expert, group_row0, group_nsub, meta, xs, w_up, b_up, w_down, b_down):
    R, half = xs.shape
    E, D, ff2 = w_up.shape
    ff = ff2 // 2
    sub = EXPERT_SUB_ROWS
    gb = EXPERT_GROUP_SUBS
    tf = min(EXPERT_FF_COLS, ff)
    n_f = ff // tf
    assert n_f % 2 == 0
    grp = lambda s, ge, r0, ns, meta: (ge[jnp.minimum(s, meta[0] - 1)], 0, 0)
    return pl.pallas_call(
        functools.partial(_experts_kernel, n_f=n_f),
        grid_spec=pltpu.PrefetchScalarGridSpec(
            num_scalar_prefetch=4,
            grid=(group_expert.shape[0],),
            in_specs=[
                pl.BlockSpec((None, 2 * n_f, tf), grp),
                pl.BlockSpec((None, 1, D), grp),
                pl.BlockSpec(memory_space=pl.ANY),
                pl.BlockSpec(memory_space=pl.ANY),
                pl.BlockSpec(memory_space=pl.ANY),
            ],
            out_specs=pl.BlockSpec(memory_space=pl.ANY),
            scratch_shapes=[
                pltpu.VMEM((gb, sub, D), BF16),
                pltpu.VMEM((gb, sub, D), F32),
                pltpu.VMEM((gb, sub, half), U32),
                pltpu.VMEM((D, tf), F32),
                pltpu.VMEM((D, tf), F32),
                pltpu.VMEM((tf, D), F32),
                pltpu.VMEM((2, D, tf), BF16),
                pltpu.VMEM((2, D, tf), BF16),
                pltpu.VMEM((2, tf, D), BF16),
                pltpu.VMEM((2, sub, half), U32),
                pltpu.SemaphoreType.DMA((3,)),
                pltpu.SemaphoreType.DMA((2,)),
                pltpu.SemaphoreType.DMA(()),
            ],
        ),
        out_shape=jax.ShapeDtypeStruct((R, half), U32),
        compiler_params=pltpu.CompilerParams(dimension_semantics=("arbitrary",),
                                             vmem_limit_bytes=EXPERT_VMEM_LIMIT_BYTES),
        name="experts",
    )(group_expert, group_row0, group_nsub, meta, b_up.reshape(E, 2 * n_f, tf), b_down.reshape(E, 1, D),
      w_up, w_down, xs)


def _combine_kernel(dst_ref, nxt_ref, gate_ref, h1_ref, lng_ref, lnb_ref, ys_ref, out_ref, buf0_ref, buf1_ref,
                    sem, *, top_k):
    i = pl.program_id(0)
    last = pl.num_programs(0) - 1
    tb = h1_ref.shape[0]
    half = buf0_ref.shape[2]

    def row_copy(slots_ref, buf_ref, slot, r, k):
        row = slots_ref[0, k * tb + r]
        return pltpu.make_async_copy(ys_ref.at[pl.ds(row, 1)], buf_ref.at[k, pl.ds(r, 1)], sem.at[slot])

    def wait_block(buf_ref, slot):
        for k in range(top_k):
            pltpu.make_async_copy(ys_ref.at[pl.ds(0, tb)], buf_ref.at[k], sem.at[slot]).wait()

    @pl.when(i == 0)
    def _():
        def start(c, carry):
            for u in range(DMA_ISSUE_UNROLL):
                for k in range(top_k):
                    row_copy(dst_ref, buf0_ref, 0, c * DMA_ISSUE_UNROLL + u, k).start(priority=k % 2)
            return carry

        lax.fori_loop(0, tb // DMA_ISSUE_UNROLL, start, 0)

    def step(cur_ref, cur, nxt_buf_ref, nxt):
        wait_block(cur_ref, cur)
        for r in range(tb):
            for k in range(top_k):
                row_copy(nxt_ref, nxt_buf_ref, nxt, r, k).start(priority=k % 2)

        diag = lax.broadcasted_iota(I32, (tb, tb), 0) == lax.broadcasted_iota(I32, (tb, tb), 1)
        ffn_lo = jnp.zeros((tb, half), F32)
        ffn_hi = jnp.zeros((tb, half), F32)
        for k in range(top_k):
            gate = jnp.sum(jnp.where(diag, gate_ref[k:k + 1, :], 0.0), axis=1, keepdims=True)
            lo, hi = _unpack_bf16_pair(cur_ref[k])
            ffn_lo = ffn_lo + lo.astype(F32) * gate
            ffn_hi = ffn_hi + hi.astype(F32) * gate
        ffn = jnp.concatenate([ffn_lo, ffn_hi], axis=1)
        out_ref[...] = _layer_norm(DEEPNORM_ALPHA * h1_ref[...] + ffn, lng_ref[...], lnb_ref[...])

        @pl.when(i == last)
        def _():
            wait_block(nxt_buf_ref, nxt)

    @pl.when(i % 2 == 0)
    def _():
        step(buf0_ref, 0, buf1_ref, 1)

    @pl.when(i % 2 == 1)
    def _():
        step(buf1_ref, 1, buf0_ref, 0)


def _combine(dst, gates, h1, ln_g, ln_b, ys):
    T, D = h1.shape
    K = dst.shape[0]
    tb = min(COMBINE_ROWS, T)
    slots = _per_block_slots(dst, tb)
    n_blocks = T // tb
    return pl.pallas_call(
        functools.partial(_combine_kernel, top_k=K),
        grid=(n_blocks,),
        in_specs=[
            pl.BlockSpec((None, 1, K * tb), lambda i: (i, 0, 0), memory_space=pltpu.SMEM),
            pl.BlockSpec((None, 1, K * tb), lambda i: (jnp.minimum(i + 1, n_blocks - 1), 0, 0),
                         memory_space=pltpu.SMEM),
            pl.BlockSpec((K, tb), lambda i: (0, i)),
            pl.BlockSpec((tb, D), lambda i: (i, 0)),
            pl.BlockSpec((1, D), lambda i: (0, 0)),
            pl.BlockSpec((1, D), lambda i: (0, 0)),
            pl.BlockSpec(memory_space=pl.ANY),
        ],
        out_specs=pl.BlockSpec((tb, D), lambda i: (i, 0)),
        out_shape=jax.ShapeDtypeStruct((T, D), F32),
        scratch_shapes=[pltpu.VMEM((K, tb, D // 2), U32), pltpu.VMEM((K, tb, D // 2), U32),
                        pltpu.SemaphoreType.DMA((2,))],
        compiler_params=_params("arbitrary"),
        name="combine",
    )(slots, slots, gates, h1, ln_g, ln_b, ys)


def _expert_schedule(counts, n_tokens_routed):
    sub, gb = EXPERT_SUB_ROWS, EXPERT_GROUP_SUBS
    n_experts = counts.shape[0]
    max_subs = n_tokens_routed // sub + n_experts
    max_groups = n_experts + max_subs // gb
    n_sub = (counts + sub - 1) // sub
    sub_end = jnp.cumsum(n_sub)
    sub_start = sub_end - n_sub
    n_grp = (n_sub + gb - 1) // gb
    grp_end = jnp.cumsum(n_grp)
    grp_start = grp_end - n_grp
    s = jnp.arange(max_groups, dtype=I32)
    expert = jnp.minimum(jnp.sum(grp_end[None, :] <= s[:, None], axis=1), n_experts - 1).astype(I32)
    onehot = (expert[:, None] == jnp.arange(n_experts, dtype=I32)[None, :]).astype(I32)
    pick = lambda v: jnp.sum(onehot * v[None, :], axis=1)
    local = s - pick(grp_start)
    row0 = (pick(sub_start) + local * gb) * sub
    nsub = jnp.clip(pick(n_sub) - local * gb, 0, gb)
    meta = jnp.stack([grp_end[-1], sub_end[-1]])
    to_i32 = lambda a: a.astype(I32)
    return to_i32(sub_start * sub), to_i32(expert), to_i32(row0), to_i32(nsub), to_i32(meta), max_subs * sub


def kernel(x, ln0_g, ln0_b, w_in, ret_decay_fwd, ret_decay_bwd, ret_gn_g, w_out, ln1_g, ln1_b, w_router, b_router,
           w_up, b_up, w_down, b_down, ln2_g, ln2_b):
    B, S, D = x.shape
    assert B == 1 and w_in.shape[0] == DEPTH
    T = S
    fw = D // 2
    rw = D - fw
    gw = fw // N_FOURIER_GROUPS
    head_dim = rw // N_RET_HEADS
    E = w_router.shape[-1]
    row = lambda a: a.reshape(1, -1)

    h, uf, ur = _in_proj(x.reshape(T, D), row(ln0_g), row(ln0_b), w_in[0].astype(BF16), head_dim)
    yf = _fourier_mix(uf, gw)
    lg_fwd = -jnp.exp(ret_decay_fwd[0].astype(F32))
    lg_bwd = -jnp.exp(ret_decay_bwd[0].astype(F32))
    o_f, o_b = _retention(ur, lg_fwd, lg_bwd, N_RET_HEADS, head_dim)
    h1, h1p = _out_proj(yf, o_f, o_b, ur, row(ret_gn_g[0]), h, w_out[0].astype(BF16), row(ln1_g[0]), row(ln1_b[0]),
                        N_RET_HEADS)
    topi, gates, rank, cnt = _router(h1, w_router[0].T, b_router[0].reshape(E, 1))

    counts = cnt[:, 0]
    pad_start, group_expert, group_row0, group_nsub, meta, n_rows = _expert_schedule(counts, T * TOP_K)
    dst = _dst_rows(pad_start, topi, rank)
    xs = _dispatch(counts, pad_start, meta, dst, h1p, n_rows)
    ys = _experts(group_expert, group_row0, group_nsub, meta, xs, w_up.reshape(w_up.shape[1:]), b_up[0],
                  w_down.reshape(w_down.shape[1:]), b_down[0])
    out = _combine(dst, gates, h1, row(ln2_g[0]), row(ln2_b[0]), ys)
    return out.reshape(B, S, D)
```

```python
import functools

import numpy as np
import jax
import jax.numpy as jnp
from jax import lax
from jax.experimental import pallas as pl
from jax.experimental.pallas import tpu as pltpu

F32 = jnp.float32
BF16 = jnp.bfloat16
I32 = jnp.int32
U32 = jnp.uint32

N_FOURIER_GROUPS = 8
N_RET_HEADS = 4
ROPE_BASE = 10000.0
TOP_K = 4
SWIGLU_LIMIT = 7.0
SWIGLU_ALPHA = 1.702
LN_EPS = 1e-5
GN_EPS = 1e-6
DEPTH = 1
DEEPNORM_ALPHA = (2.0 * DEPTH) ** 0.25

V7X_VMEM_BYTES = 64 * 2**20
VMEM_LIMIT_BYTES = V7X_VMEM_BYTES * 3 // 4
LANES = 128

IN_PROJ_ROWS = 512
IN_PROJ_PART_ROWS = 256
FOURIER_COLS_PER_STEP = 4
RET_CHUNK = 256
OUT_PROJ_ROWS = 512
OUT_PROJ_PART_ROWS = 256
ROUTER_ROWS = 1024
ROUTER_PART_ROWS = 256
DISPATCH_ROWS = 256
EXPERT_SUB_ROWS = 256
EXPERT_GROUP_SUBS = 9
EXPERT_UNIT_SUBS = 4
EXPERT_FF_COLS = 256
EXPERT_VMEM_LIMIT_BYTES = V7X_VMEM_BYTES * 7 // 8
COMBINE_ROWS = 128
DMA_ISSUE_UNROLL = 8


def _params(*semantics):
    return pltpu.CompilerParams(dimension_semantics=semantics, vmem_limit_bytes=VMEM_LIMIT_BYTES)


def _layer_norm(x, g, b):
    mu = jnp.mean(x, axis=-1, keepdims=True)
    xc = x - mu
    var = jnp.mean(xc * xc, axis=-1, keepdims=True)
    return xc * lax.rsqrt(var + LN_EPS) * g + b


def _in_proj_kernel(x_ref, g_ref, b_ref, w_ref, base_ref, cos_ref, sin_ref, h_ref, uf_ref, ur_ref, hb_ref, *,
                    head_dim, part_rows):
    j = pl.program_id(1)
    tm = x_ref.shape[0]
    parts = [slice(r0, r0 + part_rows) for r0 in range(0, tm, part_rows)]

    @pl.when(j == 0)
    def _():
        for rows in parts:
            h = _layer_norm(x_ref[rows, :], g_ref[...], b_ref[...])
            h_ref[rows, :] = h
            hb = h.astype(BF16)
            hb_ref[rows, :] = hb
            uf_ref[rows, :] = jnp.dot(hb, w_ref[...], preferred_element_type=F32).astype(BF16)

    @pl.when((j == 1) | (j == 2))
    def _():
        half = head_dim // 2
        cos0, sin0 = base_ref[0:1, :], base_ref[1:2, :]
        for rows in parts:
            acc = jnp.dot(hb_ref[rows, :], w_ref[...], preferred_element_type=F32)
            cos = cos0 * cos_ref[rows, :] - sin0 * sin_ref[rows, :]
            sin = sin0 * cos_ref[rows, :] + cos0 * sin_ref[rows, :]
            for lo in range(0, acc.shape[1], head_dim):
                x1 = acc[:, lo:lo + half]
                x2 = acc[:, lo + half:lo + head_dim]
                ur_ref[rows, lo:lo + half] = (x1 * cos - x2 * sin).astype(BF16)
                ur_ref[rows, lo + half:lo + head_dim] = (x2 * cos + x1 * sin).astype(BF16)

    @pl.when(j >= 3)
    def _():
        for rows in parts:
            ur_ref[rows, :] = jnp.dot(hb_ref[rows, :], w_ref[...], preferred_element_type=F32).astype(BF16)


def _in_proj(x2, ln_g, ln_b, w_bf, head_dim):
    T, D = x2.shape
    tn = D // 2
    n_col = w_bf.shape[1] // tn
    assert n_col == 5 and w_bf.shape[1] == 5 * tn
    tm = min(IN_PROJ_ROWS, T)
    half = head_dim // 2
    inv = ROPE_BASE ** (-jnp.arange(half, dtype=F32) / half)
    ang0 = (jnp.arange(T // tm, dtype=I32) * tm).astype(F32)[:, None] * inv[None, :]
    ang1 = jnp.arange(tm, dtype=I32).astype(F32)[:, None] * inv[None, :]
    base = jnp.stack([jnp.cos(ang0), jnp.sin(ang0)], axis=1)
    cos, sin = jnp.cos(ang1), jnp.sin(ang1)
    return pl.pallas_call(
        functools.partial(_in_proj_kernel, head_dim=head_dim, part_rows=min(IN_PROJ_PART_ROWS, tm)),
        grid=(T // tm, n_col),
        in_specs=[
            pl.BlockSpec((tm, D), lambda i, j: (i, 0)),
            pl.BlockSpec((1, D), lambda i, j: (0, 0)),
            pl.BlockSpec((1, D), lambda i, j: (0, 0)),
            pl.BlockSpec((D, tn), lambda i, j: (0, j)),
            pl.BlockSpec((None, 2, half), lambda i, j: (i, 0, 0)),
            pl.BlockSpec((tm, half), lambda i, j: (0, 0)),
            pl.BlockSpec((tm, half), lambda i, j: (0, 0)),
        ],
        out_specs=[
            pl.BlockSpec((tm, D), lambda i, j: (i, 0)),
            pl.BlockSpec((tm, tn), lambda i, j: (i, 0)),
            pl.BlockSpec((tm, tn), lambda i, j: (i, jnp.maximum(j - 1, 0))),
        ],
        out_shape=[
            jax.ShapeDtypeStruct((T, D), F32),
            jax.ShapeDtypeStruct((T, tn), BF16),
            jax.ShapeDtypeStruct((T, 4 * tn), BF16),
        ],
        scratch_shapes=[pltpu.VMEM((tm, D), BF16)],
        compiler_params=_params("parallel", "arbitrary"),
        name="in_proj",
    )(x2, ln_g, ln_b, w_bf, base, cos, sin)


def _fourier_tables(n1, n2, gw):
    def cos_sin(n, r, c):
        ang = 2.0 * np.pi * ((np.outer(r, c) % n).astype(np.float64)) / n
        return np.cos(ang), np.sin(ang)

    cg, sg = cos_sin(gw, np.arange(gw), np.arange(gw))
    c1, s1 = cos_sin(n1, np.arange(n1), np.arange(n1))
    c3, s3 = cos_sin(n2, np.arange(n2), np.arange(n2))
    tc, ts = cos_sin(n1 * n2, np.arange(n1), np.arange(n2))
    cs = np.concatenate([cg, sg], axis=1)
    m1 = np.block([[c1, -s1], [-s1, -c1]])
    m3 = np.concatenate([c3, s3], axis=1)
    as32 = lambda a: jnp.asarray(a.astype(np.float32))
    return as32(cs).astype(BF16), as32(m1).astype(BF16), as32(m3).astype(BF16), as32(tc), as32(ts)


def _fourier_a_kernel(uf_ref, cs_ref, m1_ref, twc_ref, tws_ref, zz_ref, *, nb, gw):
    bb = pl.program_id(0)
    n1 = uf_ref.shape[0]
    width = uf_ref.shape[1] // nb
    cs = cs_ref[...]
    m1 = m1_ref[...]
    lane = lax.broadcasted_iota(I32, twc_ref.shape, 1)
    for t in range(nb):
        x = uf_ref[:, t * width:(t + 1) * width]
        a_parts, b_parts = [], []
        for lo in range(0, width, gw):
            ab = jnp.dot(x[:, lo:lo + gw], cs, preferred_element_type=F32)
            a_parts.append(ab[:, :gw])
            b_parts.append(ab[:, gw:])
        v = jnp.concatenate([jnp.concatenate(a_parts, axis=1), jnp.concatenate(b_parts, axis=1)], axis=0)
        z = jnp.dot(m1, v.astype(BF16), preferred_element_type=F32)
        zr, zi = z[:n1], z[n1:]
        sel = lane == bb * nb + t
        tc = jnp.sum(jnp.where(sel, twc_ref[...], 0.0), axis=1, keepdims=True)
        ts = jnp.sum(jnp.where(sel, tws_ref[...], 0.0), axis=1, keepdims=True)
        base = t * 2 * width
        zz_ref[:, base:base + width] = (zr * tc + zi * ts).astype(BF16)
        zz_ref[:, base + width:base + 2 * width] = (zi * tc - zr * ts).astype(BF16)


def _fourier_b_kernel(zz_ref, m3_ref, yf_ref, *, scale):
    nc = zz_ref.shape[0]
    width = zz_ref.shape[2] // 2
    m3 = m3_ref[...]
    for t in range(nc):
        z = zz_ref[t]
        stacked = jnp.concatenate([z[:, :width], z[:, width:]], axis=0)
        x = jnp.dot(m3, stacked, preferred_element_type=F32)
        yf_ref[:, t * width:(t + 1) * width] = (x * scale).astype(BF16)


def _fourier_mix(uf, gw):
    T, F = uf.shape
    n1 = min(LANES, T)
    n2 = T // n1
    assert n1 * n2 == T
    cs, m1, m3, twc, tws = _fourier_tables(n1, n2, gw)
    nb = min(FOURIER_COLS_PER_STEP, n2)
    nc = min(FOURIER_COLS_PER_STEP, n1)
    zz = pl.pallas_call(
        functools.partial(_fourier_a_kernel, nb=nb, gw=gw),
        grid=(n2 // nb,),
        in_specs=[
            pl.BlockSpec((n1, nb * F), lambda b: (0, b)),
            pl.BlockSpec(cs.shape, lambda b: (0, 0)),
            pl.BlockSpec(m1.shape, lambda b: (0, 0)),
            pl.BlockSpec(twc.shape, lambda b: (0, 0)),
            pl.BlockSpec(tws.shape, lambda b: (0, 0)),
        ],
        out_specs=pl.BlockSpec((n1, nb * 2 * F), lambda b: (0, b)),
        out_shape=jax.ShapeDtypeStruct((n1, n2 * 2 * F), BF16),
        compiler_params=_params("parallel"),
        name="fourier_a",
    )(uf.reshape(n1, n2 * F), cs, m1, twc, tws)
    yf = pl.pallas_call(
        functools.partial(_fourier_b_kernel, scale=float(1.0 / np.sqrt(T * gw))),
        grid=(n1 // nc,),
        in_specs=[
            pl.BlockSpec((nc, n2, 2 * F), lambda c: (c, 0, 0)),
            pl.BlockSpec(m3.shape, lambda c: (0, 0)),
        ],
        out_specs=pl.BlockSpec((n2, nc * F), lambda c: (0, c)),
        out_shape=jax.ShapeDtypeStruct((n2, n1 * F), BF16),
        compiler_params=_params("parallel"),
        name="fourier_b",
    )(zz.reshape(n1, n2, 2 * F), m3)
    return yf.reshape(T, F)


_TAB_K_TO_END_F, _TAB_Q_FROM_START_F, _TAB_K_TO_END_B, _TAB_Q_FROM_START_B, _TAB_CHUNK_F, _TAB_CHUNK_B = range(6)


def _retention_kernel(lgf_ref, lgb_ref, qf_ref, kf_ref, vf_ref, qb_ref, kb_ref, vb_ref, of_ref, ob_ref,
                      sf_ref, sb_ref, dmat_ref, tab_ref, *, kscale, n_heads):
    n = pl.program_id(0)
    C = qf_ref.shape[0]
    hd = qf_ref.shape[1] // n_heads

    @pl.when(n == 0)
    def _():
        r = lax.broadcasted_iota(I32, (C, C), 0).astype(F32)
        c = lax.broadcasted_iota(I32, (C, C), 1).astype(F32)
        diff = r - c
        rr = lax.broadcasted_iota(I32, (C, hd), 0).astype(F32)
        for head in range(n_heads):
            lgf = lgf_ref[head]
            lgb = lgb_ref[head]
            dmat_ref[head] = kscale * jnp.where(diff >= 0.0, jnp.exp(lgf * jnp.maximum(diff, 0.0)),
                                                jnp.exp(lgb * jnp.maximum(-diff, 0.0)))
            tab_ref[head, _TAB_K_TO_END_F] = kscale * jnp.exp(lgf * (C - 1.0 - rr))
            tab_ref[head, _TAB_Q_FROM_START_F] = jnp.exp(lgf * (rr + 1.0))
            tab_ref[head, _TAB_K_TO_END_B] = kscale * jnp.exp(lgb * rr)
            tab_ref[head, _TAB_Q_FROM_START_B] = jnp.exp(lgb * (C - rr))
            tab_ref[head, _TAB_CHUNK_F] = jnp.exp(jnp.full((C, hd), lgf * C, F32))
            tab_ref[head, _TAB_CHUNK_B] = jnp.exp(jnp.full((C, hd), lgb * C, F32))
        sf_ref[...] = jnp.zeros_like(sf_ref)
        sb_ref[...] = jnp.zeros_like(sb_ref)

    nt = (((1,), (1,)), ((), ()))
    tn = (((0,), (0,)), ((), ()))

    for head in range(n_heads):
        cols = slice(head * hd, (head + 1) * hd)
        q = qf_ref[:, cols]
        k = kf_ref[:, cols]
        v = vf_ref[:, cols]
        scores = lax.dot_general(q, k, nt, preferred_element_type=F32) * dmat_ref[head]
        intra = jnp.dot(scores.astype(BF16), v, preferred_element_type=F32)
        cross = (jnp.dot(q, sf_ref[head].astype(BF16), preferred_element_type=F32)
                 * tab_ref[head, _TAB_Q_FROM_START_F])
        of_ref[:, cols] = intra + cross
        v_dec = (v.astype(F32) * tab_ref[head, _TAB_K_TO_END_F]).astype(BF16)
        sf_ref[head] = (sf_ref[head] * tab_ref[head, _TAB_CHUNK_F]
                        + lax.dot_general(k, v_dec, tn, preferred_element_type=F32))

        q = qb_ref[:, cols]
        k = kb_ref[:, cols]
        v = vb_ref[:, cols]
        ob_ref[:, cols] = (jnp.dot(q, sb_ref[head].astype(BF16), preferred_element_type=F32)
                           * tab_ref[head, _TAB_Q_FROM_START_B])
        v_dec = (v.astype(F32) * tab_ref[head, _TAB_K_TO_END_B]).astype(BF16)
        sb_ref[head] = (sb_ref[head] * tab_ref[head, _TAB_CHUNK_B]
                        + lax.dot_general(k, v_dec, tn, preferred_element_type=F32))


def _retention(ur, lg_fwd, lg_bwd, n_heads, head_dim):
    T = ur.shape[0]
    C = min(RET_CHUNK, T)
    assert C == head_dim
    N = T // C
    H = n_heads
    rw = H * head_dim
    fwd = lambda col: (lambda n, *_: (n, col))
    bwd = lambda col: (lambda n, *_: (N - 1 - n, col))
    blk = lambda imap: pl.BlockSpec((C, rw), imap)
    return pl.pallas_call(
        functools.partial(_retention_kernel, kscale=float(head_dim) ** -0.5, n_heads=H),
        grid_spec=pltpu.PrefetchScalarGridSpec(
            num_scalar_prefetch=2,
            grid=(N,),
            in_specs=[blk(fwd(0)), blk(fwd(1)), blk(fwd(2)), blk(bwd(0)), blk(bwd(1)), blk(bwd(2))],
            out_specs=[blk(fwd(0)), blk(bwd(0))],
            scratch_shapes=[
                pltpu.VMEM((H, head_dim, head_dim), F32),
                pltpu.VMEM((H, head_dim, head_dim), F32),
                pltpu.VMEM((H, C, C), F32),
                pltpu.VMEM((H, 6, C, head_dim), F32),
            ],
        ),
        out_shape=[jax.ShapeDtypeStruct((T, rw), F32)] * 2,
        compiler_params=_params("arbitrary"),
        name="retention",
    )(lg_fwd, lg_bwd, ur, ur, ur, ur, ur, ur)


def _pack_bf16_pair(lo, hi):
    lo_bits = lax.bitcast_convert_type(lo.astype(BF16).astype(F32), U32)
    hi_bits = lax.bitcast_convert_type(hi.astype(BF16).astype(F32), U32)
    return (hi_bits & jnp.uint32(0xFFFF0000)) | (lo_bits >> 16)


def _unpack_bf16_pair(words):
    lo = lax.bitcast_convert_type(words << 16, F32).astype(BF16)
    hi = lax.bitcast_convert_type(words & jnp.uint32(0xFFFF0000), F32).astype(BF16)
    return lo, hi


def _out_proj_kernel(yf_ref, of_ref, ob_ref, g_ref, gng_ref, h_ref, wo_ref, lng_ref, lnb_ref, h1_ref, h1p_ref, *,
                     n_heads, part_rows):
    tm, rw = of_ref.shape
    hd = rw // n_heads
    fw = yf_ref.shape[1]
    D = h_ref.shape[1]
    for r0 in range(0, tm, part_rows):
        rows = slice(r0, r0 + part_rows)
        o = of_ref[rows, :] + ob_ref[rows, :]
        parts = []
        for lo in range(0, rw, hd):
            oh = o[:, lo:lo + hd]
            mu = jnp.mean(oh, axis=-1, keepdims=True)
            d = oh - mu
            var = jnp.mean(d * d, axis=-1, keepdims=True)
            parts.append(d * lax.rsqrt(var + GN_EPS))
        on = jnp.concatenate(parts, axis=-1) * gng_ref[...]
        g = g_ref[rows, :].astype(F32)
        yr = (g / (1.0 + jnp.exp(-g))) * on

        mix = (jnp.dot(yf_ref[rows, :], wo_ref[:fw, :], preferred_element_type=F32)
               + jnp.dot(yr.astype(BF16), wo_ref[fw:, :], preferred_element_type=F32))
        h1 = _layer_norm(DEEPNORM_ALPHA * h_ref[rows, :] + mix, lng_ref[...], lnb_ref[...])
        h1_ref[rows, :] = h1
        h1p_ref[rows, :] = _pack_bf16_pair(h1[:, :D // 2], h1[:, D // 2:])


def _out_proj(yf, o_f, o_b, ur, gn_g, h, wo_bf, ln_g, ln_b, n_heads):
    T, D = h.shape
    fw = yf.shape[1]
    rw = o_f.shape[1]
    tm = min(OUT_PROJ_ROWS, T)
    gate_col = ur.shape[1] // rw - 1
    row = lambda w: pl.BlockSpec((tm, w), lambda i: (i, 0))
    full = lambda a: pl.BlockSpec(a.shape, lambda i: (0,) * a.ndim)
    return pl.pallas_call(
        functools.partial(_out_proj_kernel, n_heads=n_heads, part_rows=min(OUT_PROJ_PART_ROWS, tm)),
        grid=(T // tm,),
        in_specs=[row(fw), row(rw), row(rw), pl.BlockSpec((tm, rw), lambda i: (i, gate_col)), full(gn_g), row(D),
                  full(wo_bf), full(ln_g), full(ln_b)],
        out_specs=[row(D), row(D // 2)],
        out_shape=[jax.ShapeDtypeStruct((T, D), F32), jax.ShapeDtypeStruct((T, D // 2), U32)],
        compiler_params=_params("parallel"),
        name="out_proj",
    )(yf, o_f, o_b, ur, gn_g, h, wo_bf, ln_g, ln_b)


def _router_kernel(h1_ref, wr_ref, wrl_ref, br_ref, topi_ref, gate_ref, rank_ref, cnt_ref, carry_ref, *, top_k,
                   part_rows):
    i = pl.program_id(0)
    tm = h1_ref.shape[0]
    E = wr_ref.shape[0]
    tp = part_rows

    @pl.when(i == 0)
    def _():
        carry_ref[...] = jnp.zeros_like(carry_ref)

    eidx = lax.broadcasted_iota(I32, (E, tp), 0)
    earlier = lax.broadcasted_iota(I32, (tp, tp), 0) < lax.broadcasted_iota(I32, (tp, tp), 1)
    earlier = jnp.where(earlier, 1.0, 0.0).astype(BF16)
    carry = carry_ref[...]

    for r0 in range(0, tm, tp):
        rows = slice(r0, r0 + tp)
        h = h1_ref[rows, :]
        h_hi = h.astype(BF16)
        h_lo = (h - h_hi.astype(F32)).astype(BF16)
        nt = (((1,), (1,)), ((), ()))
        logits = (lax.dot_general(wr_ref[...], h_hi, nt, preferred_element_type=F32)
                  + lax.dot_general(wr_ref[...], h_lo, nt, preferred_element_type=F32)
                  + lax.dot_general(wrl_ref[...], h_hi, nt, preferred_element_type=F32)) + br_ref[...]
        cur = logits
        sel_idx, sel_val = [], []
        for _ in range(top_k):
            m = jnp.max(cur, axis=0, keepdims=True)
            idx = jnp.min(jnp.where(cur == m, eidx, E), axis=0, keepdims=True)
            sel_idx.append(idx)
            sel_val.append(m)
            cur = jnp.where(eidx == idx, -jnp.inf, cur)
        ex = [jnp.exp(val - sel_val[0]) for val in sel_val]
        denom = ex[0]
        for e_r in ex[1:]:
            denom = denom + e_r
        topi_ref[:, rows] = jnp.concatenate(sel_idx, axis=0)
        gate_ref[:, rows] = jnp.concatenate([e_r / denom for e_r in ex], axis=0)

        member = jnp.zeros((E, tp), F32)
        for idx in sel_idx:
            member = member + jnp.where(eidx == idx, 1.0, 0.0)
        rank_full = jnp.dot(member.astype(BF16), earlier, preferred_element_type=F32) + carry
        ranks = [jnp.sum(jnp.where(eidx == idx, rank_full, 0.0), axis=0, keepdims=True) for idx in sel_idx]
        rank_ref[:, rows] = jnp.concatenate(ranks, axis=0).astype(I32)
        carry = carry + jnp.sum(member, axis=1, keepdims=True)

    carry_ref[...] = carry
    cnt_ref[...] = jnp.broadcast_to(carry, cnt_ref.shape).astype(I32)


def _router(h1, wr_t, br):
    T, D = h1.shape
    E = wr_t.shape[0]
    wr_hi = wr_t.astype(BF16)
    wr_lo = (wr_t - wr_hi.astype(F32)).astype(BF16)
    tm = min(ROUTER_ROWS, T)
    full = lambda a: pl.BlockSpec(a.shape, lambda i: (0,) * a.ndim)
    tok = lambda: pl.BlockSpec((TOP_K, tm), lambda i: (0, i))
    return pl.pallas_call(
        functools.partial(_router_kernel, top_k=TOP_K, part_rows=min(ROUTER_PART_ROWS, tm)),
        grid=(T // tm,),
        in_specs=[pl.BlockSpec((tm, D), lambda i: (i, 0)), full(wr_hi), full(wr_lo), full(br)],
        out_specs=[tok(), tok(), tok(), pl.BlockSpec((E, LANES), lambda i: (0, 0))],
        out_shape=[
            jax.ShapeDtypeStruct((TOP_K, T), I32),
            jax.ShapeDtypeStruct((TOP_K, T), F32),
            jax.ShapeDtypeStruct((TOP_K, T), I32),
            jax.ShapeDtypeStruct((E, LANES), I32),
        ],
        scratch_shapes=[pltpu.VMEM((E, 1), F32)],
        compiler_params=_params("arbitrary"),
        name="router",
    )(h1, wr_hi, wr_lo, br)


def _dst_kernel(start_ref, topi_ref, rank_ref, dst_ref, *, n_experts):
    topi = topi_ref[...]
    dst = rank_ref[...]
    for e in range(n_experts):
        dst = dst + jnp.where(topi == e, start_ref[e], 0)
    dst_ref[...] = dst


def _dst_rows(pad_start, topi, rank):
    spec = pl.BlockSpec(topi.shape, lambda i, *_: (0, 0))
    return pl.pallas_call(
        functools.partial(_dst_kernel, n_experts=pad_start.shape[0]),
        grid_spec=pltpu.PrefetchScalarGridSpec(num_scalar_prefetch=1, grid=(1,), in_specs=[spec, spec], out_specs=spec),
        out_shape=jax.ShapeDtypeStruct(topi.shape, I32),
        compiler_params=_params("arbitrary"),
        name="dst",
    )(pad_start, topi, rank)


def _per_block_slots(dst, tb):
    K, T = dst.shape
    return dst.reshape(K, T // tb, tb).transpose(1, 0, 2).reshape(T // tb, 1, K * tb)


def _dispatch_kernel(cnt_ref, start_ref, meta_ref, dst_ref, h1p_ref, xs_ref, zero_ref, sem, zsem, *, top_k, sub):
    i = pl.program_id(0)
    tb = h1p_ref.shape[0]

    def padding_copies(e, act):
        pos = start_ref[e] + cnt_ref[e]
        n_single = (8 - pos % 8) % 8
        for u in range(7):
            @pl.when(u < n_single)
            def _():
                act(pltpu.make_async_copy(zero_ref.at[pl.ds(0, 1)], xs_ref.at[pl.ds(pos + u, 1)], zsem))

        pos8 = pos + n_single
        rem = start_ref[e] + (cnt_ref[e] + sub - 1) // sub * sub - pos8
        size = sub // 2
        while size >= 8:
            at = pl.multiple_of(pos8 + rem // (2 * size) * (2 * size), 8)

            @pl.when(rem % (2 * size) >= size)
            def _():
                act(pltpu.make_async_copy(zero_ref.at[pl.ds(0, size)], xs_ref.at[pl.ds(at, size)], zsem))

            size //= 2

    def tail_copy(sb):
        rows = pl.ds(pl.multiple_of(sb * sub, sub), sub)
        return pltpu.make_async_copy(zero_ref, xs_ref.at[rows], zsem)

    @pl.when(i == 0)
    def _():
        zero_ref[...] = jnp.zeros_like(zero_ref)
        n_experts = cnt_ref.shape[0]
        n_total = xs_ref.shape[0] // sub
        for act in (lambda copy: copy.start(), lambda copy: copy.wait()):
            def per_expert(e, carry, act=act):
                padding_copies(e, act)
                return carry

            def per_tail(sb, carry, act=act):
                act(tail_copy(sb))
                return carry

            lax.fori_loop(0, n_experts, per_expert, 0)
            lax.fori_loop(meta_ref[1], n_total, per_tail, 0)

    def start(c, carry):
        for u in range(DMA_ISSUE_UNROLL):
            r = c * DMA_ISSUE_UNROLL + u
            for k in range(top_k):
                row = dst_ref[0, k * tb + r]
                pltpu.make_async_copy(h1p_ref.at[pl.ds(r, 1)], xs_ref.at[pl.ds(row, 1)], sem).start(priority=k % 2)
        return carry

    lax.fori_loop(0, tb // DMA_ISSUE_UNROLL, start, 0)
    for k in range(top_k):
        pltpu.make_async_copy(h1p_ref, xs_ref.at[pl.ds(0, tb)], sem).wait()


def _dispatch(counts, pad_start, meta, dst, h1p, n_rows):
    T, W = h1p.shape
    tb = min(DISPATCH_ROWS, T)
    slots = _per_block_slots(dst, tb)
    return pl.pallas_call(
        functools.partial(_dispatch_kernel, top_k=dst.shape[0], sub=EXPERT_SUB_ROWS),
        grid_spec=pltpu.PrefetchScalarGridSpec(
            num_scalar_prefetch=3,
            grid=(T // tb,),
            in_specs=[
                pl.BlockSpec((None, 1, slots.shape[2]), lambda i, *_: (i, 0, 0), memory_space=pltpu.SMEM),
                pl.BlockSpec((tb, W), lambda i, *_: (i, 0)),
            ],
            out_specs=pl.BlockSpec(memory_space=pl.ANY),
            scratch_shapes=[pltpu.VMEM((EXPERT_SUB_ROWS, W), U32), pltpu.SemaphoreType.DMA(()),
                            pltpu.SemaphoreType.DMA(())],
        ),
        out_shape=jax.ShapeDtypeStruct((n_rows, W), U32),
        compiler_params=_params("arbitrary"),
        name="dispatch",
    )(counts, pad_start, meta, slots, h1p)


def _experts_kernel(ge_ref, row0_ref, nsub_ref, meta_ref, bu_ref, bd_ref, wup_ref, wdn_ref, xs_ref, ys_ref,
                    xbuf_ref, acc_ref, ybuf_ref, wgf_ref, wlf_ref, wdf_ref, wgb_ref, wlb_ref, wdb_ref, stage_ref,
                    wsem, xsem, ysem, *, n_f):
    s = pl.program_id(0)
    n_groups = meta_ref[0]
    sub, half = stage_ref.shape[1:]
    tf = wdf_ref.shape[0]
    ff = n_f * tf

    def y_copy(ib, row):
        return pltpu.make_async_copy(ybuf_ref.at[ib], ys_ref.at[pl.ds(pl.multiple_of(row, sub), sub)], ysem)

    def weight_copies(e, j):
        glu_cols = pl.ds(pl.multiple_of(j * tf, tf), tf)
        lin_cols = pl.ds(pl.multiple_of(ff + j * tf, tf), tf)
        return (pltpu.make_async_copy(wup_ref.at[e, :, glu_cols], wgf_ref, wsem.at[0]),
                pltpu.make_async_copy(wup_ref.at[e, :, lin_cols], wlf_ref, wsem.at[1]),
                pltpu.make_async_copy(wdn_ref.at[e, glu_cols, :], wdf_ref, wsem.at[2]))

    def cast_weights(slot):
        wgb_ref[slot] = wgf_ref[...].astype(BF16)
        wlb_ref[slot] = wlf_ref[...].astype(BF16)
        wdb_ref[slot] = wdf_ref[...].astype(BF16)

    @pl.when(s < n_groups)
    def _():
        e = ge_ref[s]
        e_after = ge_ref[jnp.minimum(s + 1, n_groups - 1)]
        n_sub = nsub_ref[s]
        row0 = row0_ref[s]

        def x_copy(ib, slot):
            rows = pl.ds(pl.multiple_of(row0 + ib * sub, sub), sub)
            return pltpu.make_async_copy(xs_ref.at[rows], stage_ref.at[slot], xsem.at[slot])

        x_copy(0, 0).start()

        @pl.when(s == 0)
        def _():
            for copy in weight_copies(e, 0):
                copy.start()
            for copy in weight_copies(e, 0):
                copy.wait()
            cast_weights(0)

        def item(j, carry):
            slot = j % 2
            is_last = j == n_f - 1
            next_j = jnp.where(is_last, 0, j + 1)
            next_e = jnp.where(is_last, e_after, e)
            for copy in weight_copies(next_e, next_j):
                copy.start()

            def load_tokens(ib):
                xslot = ib % 2
                x_copy(ib, xslot).wait()

                @pl.when(ib + 1 < n_sub)
                def _():
                    x_copy(ib + 1, 1 - xslot).start()

                lo, hi = _unpack_bf16_pair(stage_ref[xslot])
                xbuf_ref[ib, :, :half] = lo
                xbuf_ref[ib, :, half:] = hi
                acc_ref[ib] = jnp.broadcast_to(bd_ref[...], acc_ref.shape[1:])

            def accumulate(ib, n):
                width = xbuf_ref.shape[2]
                x = xbuf_ref[pl.ds(ib, n)].reshape(n * sub, width)
                glu = jnp.dot(x, wgb_ref[slot], preferred_element_type=F32) + bu_ref[pl.ds(j, 1), :]
                lin = jnp.dot(x, wlb_ref[slot], preferred_element_type=F32) + bu_ref[pl.ds(n_f + j, 1), :]
                glu = jnp.minimum(glu, SWIGLU_LIMIT)
                lin = jnp.clip(lin, -SWIGLU_LIMIT, SWIGLU_LIMIT)
                act = glu / (1.0 + jnp.exp(-SWIGLU_ALPHA * glu)) * (lin + 1.0)
                part = jnp.dot(act.astype(BF16), wdb_ref[slot], preferred_element_type=F32)
                acc_ref[pl.ds(ib, n)] += part.reshape(n, sub, width)

            def process(ib0, count, with_next_weights):
                if count <= 2:
                    chains = [(ib, 1) for ib in range(count)]
                else:
                    chains = [(ib, 2) for ib in range(0, count - 1, 2)] + [(count - 1, 1)] * (count % 2)

                @pl.when(j == 0)
                def _():
                    for ib in range(count):
                        load_tokens(ib0 + ib)

                if with_next_weights:
                    for copy in weight_copies(next_e, next_j):
                        copy.wait()
                    cast_weights(1 - slot)
                for first, n in chains:
                    accumulate(ib0 + first, n)

                @pl.when(is_last)
                def _():
                    for ib in range(count):
                        y = acc_ref[ib0 + ib]
                        ybuf_ref[ib0 + ib] = _pack_bf16_pair(y[:, :half], y[:, half:])
                        y_copy(ib0 + ib, row0 + (ib0 + ib) * sub).start()

            n_units = (n_sub - 1) // EXPERT_UNIT_SUBS

            def unit(p, carry):
                process(EXPERT_UNIT_SUBS * p, EXPERT_UNIT_SUBS, False)
                return carry

            lax.fori_loop(0, n_units, unit, 0)
            for rest in range(1, EXPERT_UNIT_SUBS + 1):
                @pl.when(n_sub - EXPERT_UNIT_SUBS * n_units == rest)
                def _(rest=rest):
                    process(EXPERT_UNIT_SUBS * n_units, rest, True)

            return carry

        lax.fori_loop(0, n_f, item, 0)

        def drain(ib, carry):
            y_copy(ib, row0 + ib * sub).wait()
            return carry

        lax.fori_loop(0, n_sub, drain, 0)

    @pl.when(s == pl.num_programs(0) - 1)
    def _():
        n_total = ys_ref.shape[0] // sub
        ybuf_ref[0] = jnp.zeros(ybuf_ref.shape[1:], U32)

        def fill(ib, carry):
            y_copy(0, ib * sub).start()
            return carry

        def drain(ib, carry):
            y_copy(0, ib * sub).wait()
            return carry

        lax.fori_loop(meta_ref[1], n_total, fill, 0)
        lax.fori_loop(meta_ref[1], n_total, drain, 0)


def _experts(group_expert, group_row0, group_nsub, meta, xs, w_up, b_up, w_down, b_down):
    R, half = xs.shape
    E, D, ff2 = w_up.shape
    ff = ff2 // 2
    sub = EXPERT_SUB_ROWS
    gb = EXPERT_GROUP_SUBS
    tf = min(EXPERT_FF_COLS, ff)
    n_f = ff // tf
    assert n_f % 2 == 0
    grp = lambda s, ge, r0, ns, meta: (ge[jnp.minimum(s, meta[0] - 1)], 0, 0)
    return pl.pallas_call(
        functools.partial(_experts_kernel, n_f=n_f),
        grid_spec=pltpu.PrefetchScalarGridSpec(
            num_scalar_prefetch=4,
            grid=(group_expert.shape[0],),
            in_specs=[
                pl.BlockSpec((None, 2 * n_f, tf), grp),
                pl.BlockSpec((None, 1, D), grp),
                pl.BlockSpec(memory_space=pl.ANY),
                pl.BlockSpec(memory_space=pl.ANY),
                pl.BlockSpec(memory_space=pl.ANY),
            ],
            out_specs=pl.BlockSpec(memory_space=pl.ANY),
            scratch_shapes=[
                pltpu.VMEM((gb, sub, D), BF16),
                pltpu.VMEM((gb, sub, D), F32),
                pltpu.VMEM((gb, sub, half), U32),
                pltpu.VMEM((D, tf), F32),
                pltpu.VMEM((D, tf), F32),
                pltpu.VMEM((tf, D), F32),
                pltpu.VMEM((2, D, tf), BF16),
                pltpu.VMEM((2, D, tf), BF16),
                pltpu.VMEM((2, tf, D), BF16),
                pltpu.VMEM((2, sub, half), U32),
                pltpu.SemaphoreType.DMA((3,)),
                pltpu.SemaphoreType.DMA((2,)),
                pltpu.SemaphoreType.DMA(()),
            ],
        ),
        out_shape=jax.ShapeDtypeStruct((R, half), U32),
        compiler_params=pltpu.CompilerParams(dimension_semantics=("arbitrary",),
                                             vmem_limit_bytes=EXPERT_VMEM_LIMIT_BYTES),
        name="experts",
    )(group_expert, group_row0, group_nsub, meta, b_up.reshape(E, 2 * n_f, tf), b_down.reshape(E, 1, D),
      w_up, w_down, xs)


def _combine_kernel(dst_ref, nxt_ref, gate_ref, h1_ref, lng_ref, lnb_ref, ys_ref, out_ref, buf0_ref, buf1_ref,
                    sem, *, top_k):
    i = pl.program_id(0)
    last = pl.num_programs(0) - 1
    tb = h1_ref.shape[0]
    half = buf0_ref.shape[2]

    def row_copy(slots_ref, buf_ref, slot, r, k):
        row = slots_ref[0, k * tb + r]
        return pltpu.make_async_copy(ys_ref.at[pl.ds(row, 1)], buf_ref.at[k, pl.ds(r, 1)], sem.at[slot])

    def wait_block(buf_ref, slot):
        for k in range(top_k):
            pltpu.make_async_copy(ys_ref.at[pl.ds(0, tb)], buf_ref.at[k], sem.at[slot]).wait()

    @pl.when(i == 0)
    def _():
        def start(c, carry):
            for u in range(DMA_ISSUE_UNROLL):
                for k in range(top_k):
                    row_copy(dst_ref, buf0_ref, 0, c * DMA_ISSUE_UNROLL + u, k).start(priority=k % 2)
            return carry

        lax.fori_loop(0, tb // DMA_ISSUE_UNROLL, start, 0)

    def step(cur_ref, cur, nxt_buf_ref, nxt):
        wait_block(cur_ref, cur)
        for r in range(tb):
            for k in range(top_k):
                row_copy(nxt_ref, nxt_buf_ref, nxt, r, k).start(priority=k % 2)

        diag = lax.broadcasted_iota(I32, (tb, tb), 0) == lax.broadcasted_iota(I32, (tb, tb), 1)
        ffn_lo = jnp.zeros((tb, half), F32)
        ffn_hi = jnp.zeros((tb, half), F32)
        for k in range(top_k):
            gate = jnp.sum(jnp.where(diag, gate_ref[k:k + 1, :], 0.0), axis=1, keepdims=True)
            lo, hi = _unpack_bf16_pair(cur_ref[k])
            ffn_lo = ffn_lo + lo.astype(F32) * gate
            ffn_hi = ffn_hi + hi.astype(F32) * gate
        ffn = jnp.concatenate([ffn_lo, ffn_hi], axis=1)
        out_ref[...] = _layer_norm(DEEPNORM_ALPHA * h1_ref[...] + ffn, lng_ref[...], lnb_ref[...])

        @pl.when(i == last)
        def _():
            wait_block(nxt_buf_ref, nxt)

    @pl.when(i % 2 == 0)
    def _():
        step(buf0_ref, 0, buf1_ref, 1)

    @pl.when(i % 2 == 1)
    def _():
        step(buf1_ref, 1, buf0_ref, 0)


def _combine(dst, gates, h1, ln_g, ln_b, ys):
    T, D = h1.shape
    K = dst.shape[0]
    tb = min(COMBINE_ROWS, T)
    slots = _per_block_slots(dst, tb)
    n_blocks = T // tb
    return pl.pallas_call(
        functools.partial(_combine_kernel, top_k=K),
        grid=(n_blocks,),
        in_specs=[
            pl.BlockSpec((None, 1, K * tb), lambda i: (i, 0, 0), memory_space=pltpu.SMEM),
            pl.BlockSpec((None, 1, K * tb), lambda i: (jnp.minimum(i + 1, n_blocks - 1), 0, 0),
                         memory_space=pltpu.SMEM),
            pl.BlockSpec((K, tb), lambda i: (0, i)),
            pl.BlockSpec((tb, D), lambda i: (i, 0)),
            pl.BlockSpec((1, D), lambda i: (0, 0)),
            pl.BlockSpec((1, D), lambda i: (0, 0)),
            pl.BlockSpec(memory_space=pl.ANY),
        ],
        out_specs=pl.BlockSpec((tb, D), lambda i: (i, 0)),
        out_shape=jax.ShapeDtypeStruct((T, D), F32),
        scratch_shapes=[pltpu.VMEM((K, tb, D // 2), U32), pltpu.VMEM((K, tb, D // 2), U32),
                        pltpu.SemaphoreType.DMA((2,))],
        compiler_params=_params("arbitrary"),
        name="combine",
    )(slots, slots, gates, h1, ln_g, ln_b, ys)


def _expert_schedule(counts, n_tokens_routed):
    sub, gb = EXPERT_SUB_ROWS, EXPERT_GROUP_SUBS
    n_experts = counts.shape[0]
    max_subs = n_tokens_routed // sub + n_experts
    max_groups = n_experts + max_subs // gb
    n_sub = (counts + sub - 1) // sub
    sub_end = jnp.cumsum(n_sub)
    sub_start = sub_end - n_sub
    n_grp = (n_sub + gb - 1) // gb
    grp_end = jnp.cumsum(n_grp)
    grp_start = grp_end - n_grp
    s = jnp.arange(max_groups, dtype=I32)
    expert = jnp.minimum(jnp.sum(grp_end[None, :] <= s[:, None], axis=1), n_experts - 1).astype(I32)
    onehot = (expert[:, None] == jnp.arange(n_experts, dtype=I32)[None, :]).astype(I32)
    pick = lambda v: jnp.sum(onehot * v[None, :], axis=1)
    local = s - pick(grp_start)
    row0 = (pick(sub_start) + local * gb) * sub
    nsub = jnp.clip(pick(n_sub) - local * gb, 0, gb)
    meta = jnp.stack([grp_end[-1], sub_end[-1]])
    to_i32 = lambda a: a.astype(I32)
    return to_i32(sub_start * sub), to_i32(expert), to_i32(row0), to_i32(nsub), to_i32(meta), max_subs * sub


def kernel(x, ln0_g, ln0_b, w_in, ret_decay_fwd, ret_decay_bwd, ret_gn_g, w_out, ln1_g, ln1_b, w_router, b_router,
           w_up, b_up, w_down, b_down, ln2_g, ln2_b):
    B, S, D = x.shape
    assert B == 1 and w_in.shape[0] == DEPTH
    T = S
    fw = D // 2
    rw = D - fw
    gw = fw // N_FOURIER_GROUPS
    head_dim = rw // N_RET_HEADS
    E = w_router.shape[-1]
    row = lambda a: a.reshape(1, -1)

    h, uf, ur = _in_proj(x.reshape(T, D), row(ln0_g), row(ln0_b), w_in[0].astype(BF16), head_dim)
    yf = _fourier_mix(uf, gw)
    lg_fwd = -jnp.exp(ret_decay_fwd[0].astype(F32))
    lg_bwd = -jnp.exp(ret_decay_bwd[0].astype(F32))
    o_f, o_b = _retention(ur, lg_fwd, lg_bwd, N_RET_HEADS, head_dim)
    h1, h1p = _out_proj(yf, o_f, o_b, ur, row(ret_gn_g[0]), h, w_out[0].astype(BF16), row(ln1_g[0]), row(ln1_b[0]),
                        N_RET_HEADS)
    topi, gates, rank, cnt = _router(h1, w_router[0].T, b_router[0].reshape(E, 1))

    counts = cnt[:, 0]
    pad_start, group_expert, group_row0, group_nsub, meta, n_rows = _expert_schedule(counts, T * TOP_K)
    dst = _dst_rows(pad_start, topi, rank)
    xs = _dispatch(counts, pad_start, meta, dst, h1p, n_rows)
    ys = _experts(group_expert, group_row0, group_nsub, meta, xs, w_up.reshape(w_up.shape[1:]), b_up[0],
                  w_down.reshape(w_down.shape[1:]), b_down[0])
    out = _combine(dst, gates, h1, row(ln2_g[0]), row(ln2_b[0]), ys)
    return out.reshape(B, S, D)
```

```python
import functools

import numpy as np
import jax
import jax.numpy as jnp
from jax import lax
from jax.experimental import pallas as pl
from jax.experimental.pallas import tpu as pltpu

F32 = jnp.float32
BF16 = jnp.bfloat16
I32 = jnp.int32
U32 = jnp.uint32

N_FOURIER_GROUPS = 8
N_RET_HEADS = 4
ROPE_BASE = 10000.0
TOP_K = 4
SWIGLU_LIMIT = 7.0
SWIGLU_ALPHA = 1.702
LN_EPS = 1e-5
GN_EPS = 1e-6
DEPTH = 1
DEEPNORM_ALPHA = (2.0 * DEPTH) ** 0.25

V7X_VMEM_BYTES = 64 * 2**20
VMEM_LIMIT_BYTES = V7X_VMEM_BYTES * 3 // 4
LANES = 128

IN_PROJ_ROWS = 512
IN_PROJ_PART_ROWS = 256
FOURIER_COLS_PER_STEP = 8
RET_CHUNK = 256
OUT_PROJ_ROWS = 512
OUT_PROJ_PART_ROWS = 256
ROUTER_ROWS = 1024
ROUTER_PART_ROWS = 256
DISPATCH_ROWS = 512
EXPERT_SUB_ROWS = 256
EXPERT_GROUP_SUBS = 9
EXPERT_UNIT_SUBS = 4
EXPERT_FF_COLS = 256
EXPERT_VMEM_LIMIT_BYTES = V7X_VMEM_BYTES * 7 // 8
COMBINE_ROWS = 128
DMA_ISSUE_UNROLL = 8


def _params(*semantics):
    return pltpu.CompilerParams(dimension_semantics=semantics, vmem_limit_bytes=VMEM_LIMIT_BYTES)


def _layer_norm(x, g, b):
    mu = jnp.mean(x, axis=-1, keepdims=True)
    xc = x - mu
    var = jnp.mean(xc * xc, axis=-1, keepdims=True)
    return xc * lax.rsqrt(var + LN_EPS) * g + b


def _in_proj_kernel(x_ref, g_ref, b_ref, w_ref, base_ref, cos_ref, sin_ref, h_ref, uf_ref, ur_ref, hb_ref, *,
                    head_dim, part_rows):
    j = pl.program_id(1)
    tm = x_ref.shape[0]
    parts = [slice(r0, r0 + part_rows) for r0 in range(0, tm, part_rows)]

    @pl.when(j == 0)
    def _():
        for rows in parts:
            h = _layer_norm(x_ref[rows, :], g_ref[...], b_ref[...])
            h_ref[rows, :] = h
            hb = h.astype(BF16)
            hb_ref[rows, :] = hb
            uf_ref[rows, :] = jnp.dot(hb, w_ref[...], preferred_element_type=F32).astype(BF16)

    @pl.when((j == 1) | (j == 2))
    def _():
        half = head_dim // 2
        cos0, sin0 = base_ref[0:1, :], base_ref[1:2, :]
        for rows in parts:
            acc = jnp.dot(hb_ref[rows, :], w_ref[...], preferred_element_type=F32)
            cos = cos0 * cos_ref[rows, :] - sin0 * sin_ref[rows, :]
            sin = sin0 * cos_ref[rows, :] + cos0 * sin_ref[rows, :]
            for lo in range(0, acc.shape[1], head_dim):
                x1 = acc[:, lo:lo + half]
                x2 = acc[:, lo + half:lo + head_dim]
                ur_ref[rows, lo:lo + half] = (x1 * cos - x2 * sin).astype(BF16)
                ur_ref[rows, lo + half:lo + head_dim] = (x2 * cos + x1 * sin).astype(BF16)

    @pl.when(j >= 3)
    def _():
        for rows in parts:
            ur_ref[rows, :] = jnp.dot(hb_ref[rows, :], w_ref[...], preferred_element_type=F32).astype(BF16)


def _in_proj(x2, ln_g, ln_b, w_bf, head_dim):
    T, D = x2.shape
    tn = D // 2
    n_col = w_bf.shape[1] // tn
    assert n_col == 5 and w_bf.shape[1] == 5 * tn
    tm = min(IN_PROJ_ROWS, T)
    half = head_dim // 2
    inv = ROPE_BASE ** (-jnp.arange(half, dtype=F32) / half)
    ang0 = (jnp.arange(T // tm, dtype=I32) * tm).astype(F32)[:, None] * inv[None, :]
    ang1 = jnp.arange(tm, dtype=I32).astype(F32)[:, None] * inv[None, :]
    base = jnp.stack([jnp.cos(ang0), jnp.sin(ang0)], axis=1)
    cos, sin = jnp.cos(ang1), jnp.sin(ang1)
    return pl.pallas_call(
        functools.partial(_in_proj_kernel, head_dim=head_dim, part_rows=min(IN_PROJ_PART_ROWS, tm)),
        grid=(T // tm, n_col),
        in_specs=[
            pl.BlockSpec((tm, D), lambda i, j: (i, 0)),
            pl.BlockSpec((1, D), lambda i, j: (0, 0)),
            pl.BlockSpec((1, D), lambda i, j: (0, 0)),
            pl.BlockSpec((D, tn), lambda i, j: (0, j)),
            pl.BlockSpec((None, 2, half), lambda i, j: (i, 0, 0)),
            pl.BlockSpec((tm, half), lambda i, j: (0, 0)),
            pl.BlockSpec((tm, half), lambda i, j: (0, 0)),
        ],
        out_specs=[
            pl.BlockSpec((tm, D), lambda i, j: (i, 0)),
            pl.BlockSpec((tm, tn), lambda i, j: (i, 0)),
            pl.BlockSpec((tm, tn), lambda i, j: (i, jnp.maximum(j - 1, 0))),
        ],
        out_shape=[
            jax.ShapeDtypeStruct((T, D), F32),
            jax.ShapeDtypeStruct((T, tn), BF16),
            jax.ShapeDtypeStruct((T, 4 * tn), BF16),
        ],
        scratch_shapes=[pltpu.VMEM((tm, D), BF16)],
        compiler_params=_params("parallel", "arbitrary"),
        name="in_proj",
    )(x2, ln_g, ln_b, w_bf, base, cos, sin)


def _fourier_tables(n1, n2, gw):
    def cos_sin(n, r, c):
        ang = 2.0 * np.pi * ((np.outer(r, c) % n).astype(np.float64)) / n
        return np.cos(ang), np.sin(ang)

    cg, sg = cos_sin(gw, np.arange(gw), np.arange(gw))
    c1, s1 = cos_sin(n1, np.arange(n1), np.arange(n1))
    c3, s3 = cos_sin(n2, np.arange(n2), np.arange(n2))
    tc, ts = cos_sin(n1 * n2, np.arange(n1), np.arange(n2))
    cs = np.concatenate([cg, sg], axis=1)
    m1 = np.block([[c1, -s1], [-s1, -c1]])
    m3 = np.concatenate([c3, s3], axis=1)
    as32 = lambda a: jnp.asarray(a.astype(np.float32))
    return as32(cs).astype(BF16), as32(m1).astype(BF16), as32(m3).astype(BF16), as32(tc), as32(ts)


def _fourier_a_kernel(uf_ref, cs_ref, m1_ref, twc_ref, tws_ref, zz_ref, *, nb, gw):
    bb = pl.program_id(0)
    n1 = uf_ref.shape[0]
    width = uf_ref.shape[1] // nb
    cs = cs_ref[...]
    m1 = m1_ref[...]
    lane = lax.broadcasted_iota(I32, twc_ref.shape, 1)
    for t in range(nb):
        x = uf_ref[:, t * width:(t + 1) * width]
        a_parts, b_parts = [], []
        for lo in range(0, width, gw):
            ab = jnp.dot(x[:, lo:lo + gw], cs, preferred_element_type=F32)
            a_parts.append(ab[:, :gw])
            b_parts.append(ab[:, gw:])
        v = jnp.concatenate([jnp.concatenate(a_parts, axis=1), jnp.concatenate(b_parts, axis=1)], axis=0)
        z = jnp.dot(m1, v.astype(BF16), preferred_element_type=F32)
        zr, zi = z[:n1], z[n1:]
        sel = lane == bb * nb + t
        tc = jnp.sum(jnp.where(sel, twc_ref[...], 0.0), axis=1, keepdims=True)
        ts = jnp.sum(jnp.where(sel, tws_ref[...], 0.0), axis=1, keepdims=True)
        base = t * 2 * width
        zz_ref[:, base:base + width] = (zr * tc + zi * ts).astype(BF16)
        zz_ref[:, base + width:base + 2 * width] = (zi * tc - zr * ts).astype(BF16)


def _fourier_b_kernel(zz_ref, m3_ref, yf_ref, *, scale):
    nc = zz_ref.shape[0]
    width = zz_ref.shape[2] // 2
    m3 = m3_ref[...]
    for t in range(nc):
        z = zz_ref[t]
        stacked = jnp.concatenate([z[:, :width], z[:, width:]], axis=0)
        x = jnp.dot(m3, stacked, preferred_element_type=F32)
        yf_ref[:, t * width:(t + 1) * width] = (x * scale).astype(BF16)


def _fourier_mix(uf, gw):
    T, F = uf.shape
    n1 = min(LANES, T)
    n2 = T // n1
    assert n1 * n2 == T
    cs, m1, m3, twc, tws = _fourier_tables(n1, n2, gw)
    nb = min(FOURIER_COLS_PER_STEP, n2)
    nc = min(FOURIER_COLS_PER_STEP, n1)
    zz = pl.pallas_call(
        functools.partial(_fourier_a_kernel, nb=nb, gw=gw),
        grid=(n2 // nb,),
        in_specs=[
            pl.BlockSpec((n1, nb * F), lambda b: (0, b)),
            pl.BlockSpec(cs.shape, lambda b: (0, 0)),
            pl.BlockSpec(m1.shape, lambda b: (0, 0)),
            pl.BlockSpec(twc.shape, lambda b: (0, 0)),
            pl.BlockSpec(tws.shape, lambda b: (0, 0)),
        ],
        out_specs=pl.BlockSpec((n1, nb * 2 * F), lambda b: (0, b)),
        out_shape=jax.ShapeDtypeStruct((n1, n2 * 2 * F), BF16),
        compiler_params=_params("parallel"),
        name="fourier_a",
    )(uf.reshape(n1, n2 * F), cs, m1, twc, tws)
    yf = pl.pallas_call(
        functools.partial(_fourier_b_kernel, scale=float(1.0 / np.sqrt(T * gw))),
        grid=(n1 // nc,),
        in_specs=[
            pl.BlockSpec((nc, n2, 2 * F), lambda c: (c, 0, 0)),
            pl.BlockSpec(m3.shape, lambda c: (0, 0)),
        ],
        out_specs=pl.BlockSpec((n2, nc * F), lambda c: (0, c)),
        out_shape=jax.ShapeDtypeStruct((n2, n1 * F), BF16),
        compiler_params=_params("parallel"),
        name="fourier_b",
    )(zz.reshape(n1, n2, 2 * F), m3)
    return yf.reshape(T, F)


_TAB_K_TO_END_F, _TAB_Q_FROM_START_F, _TAB_K_TO_END_B, _TAB_Q_FROM_START_B, _TAB_CHUNK_F, _TAB_CHUNK_B = range(6)


def _retention_kernel(lgf_ref, lgb_ref, qf_ref, kf_ref, vf_ref, qb_ref, kb_ref, vb_ref, of_ref, ob_ref,
                      sf_ref, sb_ref, dmat_ref, tab_ref, *, kscale, n_heads):
    n = pl.program_id(0)
    C = qf_ref.shape[0]
    hd = qf_ref.shape[1] // n_heads

    @pl.when(n == 0)
    def _():
        r = lax.broadcasted_iota(I32, (C, C), 0).astype(F32)
        c = lax.broadcasted_iota(I32, (C, C), 1).astype(F32)
        diff = r - c
        rr = lax.broadcasted_iota(I32, (C, hd), 0).astype(F32)
        for head in range(n_heads):
            lgf = lgf_ref[head]
            lgb = lgb_ref[head]
            dmat_ref[head] = kscale * jnp.where(diff >= 0.0, jnp.exp(lgf * jnp.maximum(diff, 0.0)),
                                                jnp.exp(lgb * jnp.maximum(-diff, 0.0)))
            tab_ref[head, _TAB_K_TO_END_F] = kscale * jnp.exp(lgf * (C - 1.0 - rr))
            tab_ref[head, _TAB_Q_FROM_START_F] = jnp.exp(lgf * (rr + 1.0))
            tab_ref[head, _TAB_K_TO_END_B] = kscale * jnp.exp(lgb * rr)
            tab_ref[head, _TAB_Q_FROM_START_B] = jnp.exp(lgb * (C - rr))
            tab_ref[head, _TAB_CHUNK_F] = jnp.exp(jnp.full((C, hd), lgf * C, F32))
            tab_ref[head, _TAB_CHUNK_B] = jnp.exp(jnp.full((C, hd), lgb * C, F32))
        sf_ref[...] = jnp.zeros_like(sf_ref)
        sb_ref[...] = jnp.zeros_like(sb_ref)

    nt = (((1,), (1,)), ((), ()))
    tn = (((0,), (0,)), ((), ()))

    for head in range(n_heads):
        cols = slice(head * hd, (head + 1) * hd)
        q = qf_ref[:, cols]
        k = kf_ref[:, cols]
        v = vf_ref[:, cols]
        scores = lax.dot_general(q, k, nt, preferred_element_type=F32) * dmat_ref[head]
        intra = jnp.dot(scores.astype(BF16), v, preferred_element_type=F32)
        cross = (jnp.dot(q, sf_ref[head].astype(BF16), preferred_element_type=F32)
                 * tab_ref[head, _TAB_Q_FROM_START_F])
        of_ref[:, cols] = intra + cross
        v_dec = (v.astype(F32) * tab_ref[head, _TAB_K_TO_END_F]).astype(BF16)
        sf_ref[head] = (sf_ref[head] * tab_ref[head, _TAB_CHUNK_F]
                        + lax.dot_general(k, v_dec, tn, preferred_element_type=F32))

        q = qb_ref[:, cols]
        k = kb_ref[:, cols]
        v = vb_ref[:, cols]
        ob_ref[:, cols] = (jnp.dot(q, sb_ref[head].astype(BF16), preferred_element_type=F32)
                           * tab_ref[head, _TAB_Q_FROM_START_B])
        v_dec = (v.astype(F32) * tab_ref[head, _TAB_K_TO_END_B]).astype(BF16)
        sb_ref[head] = (sb_ref[head] * tab_ref[head, _TAB_CHUNK_B]
                        + lax.dot_general(k, v_dec, tn, preferred_element_type=F32))


def _retention(ur, lg_fwd, lg_bwd, n_heads, head_dim):
    T = ur.shape[0]
    C = min(RET_CHUNK, T)
    assert C == head_dim
    N = T // C
    H = n_heads
    rw = H * head_dim
    fwd = lambda col: (lambda n, *_: (n, col))
    bwd = lambda col: (lambda n, *_: (N - 1 - n, col))
    blk = lambda imap: pl.BlockSpec((C, rw), imap)
    return pl.pallas_call(
        functools.partial(_retention_kernel, kscale=float(head_dim) ** -0.5, n_heads=H),
        grid_spec=pltpu.PrefetchScalarGridSpec(
            num_scalar_prefetch=2,
            grid=(N,),
            in_specs=[blk(fwd(0)), blk(fwd(1)), blk(fwd(2)), blk(bwd(0)), blk(bwd(1)), blk(bwd(2))],
            out_specs=[blk(fwd(0)), blk(bwd(0))],
            scratch_shapes=[
                pltpu.VMEM((H, head_dim, head_dim), F32),
                pltpu.VMEM((H, head_dim, head_dim), F32),
                pltpu.VMEM((H, C, C), F32),
                pltpu.VMEM((H, 6, C, head_dim), F32),
            ],
        ),
        out_shape=[jax.ShapeDtypeStruct((T, rw), F32)] * 2,
        compiler_params=_params("arbitrary"),
        name="retention",
    )(lg_fwd, lg_bwd, ur, ur, ur, ur, ur, ur)


def _pack_bf16_pair(lo, hi):
    lo_bits = lax.bitcast_convert_type(lo.astype(BF16).astype(F32), U32)
    hi_bits = lax.bitcast_convert_type(hi.astype(BF16).astype(F32), U32)
    return (hi_bits & jnp.uint32(0xFFFF0000)) | (lo_bits >> 16)


def _unpack_bf16_pair(words):
    lo = lax.bitcast_convert_type(words << 16, F32).astype(BF16)
    hi = lax.bitcast_convert_type(words & jnp.uint32(0xFFFF0000), F32).astype(BF16)
    return lo, hi


def _out_proj_kernel(yf_ref, of_ref, ob_ref, g_ref, gng_ref, h_ref, wo_ref, lng_ref, lnb_ref, h1_ref, h1p_ref, *,
                     n_heads, part_rows):
    tm, rw = of_ref.shape
    hd = rw // n_heads
    fw = yf_ref.shape[1]
    D = h_ref.shape[1]
    for r0 in range(0, tm, part_rows):
        rows = slice(r0, r0 + part_rows)
        o = of_ref[rows, :] + ob_ref[rows, :]
        parts = []
        for lo in range(0, rw, hd):
            oh = o[:, lo:lo + hd]
            mu = jnp.mean(oh, axis=-1, keepdims=True)
            d = oh - mu
            var = jnp.mean(d * d, axis=-1, keepdims=True)
            parts.append(d * lax.rsqrt(var + GN_EPS))
        on = jnp.concatenate(parts, axis=-1) * gng_ref[...]
        g = g_ref[rows, :].astype(F32)
        yr = (g / (1.0 + jnp.exp(-g))) * on

        mix = (jnp.dot(yf_ref[rows, :], wo_ref[:fw, :], preferred_element_type=F32)
               + jnp.dot(yr.astype(BF16), wo_ref[fw:, :], preferred_element_type=F32))
        h1 = _layer_norm(DEEPNORM_ALPHA * h_ref[rows, :] + mix, lng_ref[...], lnb_ref[...])
        h1_ref[rows, :] = h1
        h1p_ref[rows, :] = _pack_bf16_pair(h1[:, :D // 2], h1[:, D // 2:])


def _out_proj(yf, o_f, o_b, ur, gn_g, h, wo_bf, ln_g, ln_b, n_heads):
    T, D = h.shape
    fw = yf.shape[1]
    rw = o_f.shape[1]
    tm = min(OUT_PROJ_ROWS, T)
    gate_col = ur.shape[1] // rw - 1
    row = lambda w: pl.BlockSpec((tm, w), lambda i: (i, 0))
    full = lambda a: pl.BlockSpec(a.shape, lambda i: (0,) * a.ndim)
    return pl.pallas_call(
        functools.partial(_out_proj_kernel, n_heads=n_heads, part_rows=min(OUT_PROJ_PART_ROWS, tm)),
        grid=(T // tm,),
        in_specs=[row(fw), row(rw), row(rw), pl.BlockSpec((tm, rw), lambda i: (i, gate_col)), full(gn_g), row(D),
                  full(wo_bf), full(ln_g), full(ln_b)],
        out_specs=[row(D), row(D // 2)],
        out_shape=[jax.ShapeDtypeStruct((T, D), F32), jax.ShapeDtypeStruct((T, D // 2), U32)],
        compiler_params=_params("parallel"),
        name="out_proj",
    )(yf, o_f, o_b, ur, gn_g, h, wo_bf, ln_g, ln_b)


def _router_kernel(h1p_ref, wr_ref, br_ref, topi_ref, gate_ref, rank_ref, cnt_ref, carry_ref, *, top_k, part_rows):
    i = pl.program_id(0)
    tm, half = h1p_ref.shape
    E = wr_ref.shape[0]
    tp = part_rows

    @pl.when(i == 0)
    def _():
        carry_ref[...] = jnp.zeros_like(carry_ref)

    eidx = lax.broadcasted_iota(I32, (E, tp), 0)
    earlier = lax.broadcasted_iota(I32, (tp, tp), 0) < lax.broadcasted_iota(I32, (tp, tp), 1)
    earlier = jnp.where(earlier, 1.0, 0.0).astype(BF16)
    carry = carry_ref[...]

    for r0 in range(0, tm, tp):
        rows = slice(r0, r0 + tp)
        h_lo, h_hi = _unpack_bf16_pair(h1p_ref[rows, :])
        nt = (((1,), (1,)), ((), ()))
        logits = (lax.dot_general(wr_ref[:, :half], h_lo, nt, preferred_element_type=F32)
                  + lax.dot_general(wr_ref[:, half:], h_hi, nt, preferred_element_type=F32)) + br_ref[...]
        cur = logits
        sel_idx, sel_val = [], []
        for _ in range(top_k):
            m = jnp.max(cur, axis=0, keepdims=True)
            idx = jnp.min(jnp.where(cur == m, eidx, E), axis=0, keepdims=True)
            sel_idx.append(idx)
            sel_val.append(m)
            cur = jnp.where(eidx == idx, -jnp.inf, cur)
        ex = [jnp.exp(val - sel_val[0]) for val in sel_val]
        denom = ex[0]
        for e_r in ex[1:]:
            denom = denom + e_r
        topi_ref[:, rows] = jnp.concatenate(sel_idx, axis=0)
        gate_ref[:, rows] = jnp.concatenate([e_r / denom for e_r in ex], axis=0)

        member = jnp.zeros((E, tp), F32)
        for idx in sel_idx:
            member = member + jnp.where(eidx == idx, 1.0, 0.0)
        rank_full = jnp.dot(member.astype(BF16), earlier, preferred_element_type=F32) + carry
        ranks = [jnp.sum(jnp.where(eidx == idx, rank_full, 0.0), axis=0, keepdims=True) for idx in sel_idx]
        rank_ref[:, rows] = jnp.concatenate(ranks, axis=0).astype(I32)
        carry = carry + jnp.sum(member, axis=1, keepdims=True)

    carry_ref[...] = carry
    cnt_ref[...] = jnp.broadcast_to(carry, cnt_ref.shape).astype(I32)


def _router(h1p, wr_t, br):
    T, half = h1p.shape
    E = wr_t.shape[0]
    wr_bf = wr_t.astype(BF16)
    tm = min(ROUTER_ROWS, T)
    full = lambda a: pl.BlockSpec(a.shape, lambda i: (0,) * a.ndim)
    tok = lambda: pl.BlockSpec((TOP_K, tm), lambda i: (0, i))
    return pl.pallas_call(
        functools.partial(_router_kernel, top_k=TOP_K, part_rows=min(ROUTER_PART_ROWS, tm)),
        grid=(T // tm,),
        in_specs=[pl.BlockSpec((tm, half), lambda i: (i, 0)), full(wr_bf), full(br)],
        out_specs=[tok(), tok(), tok(), pl.BlockSpec((E, LANES), lambda i: (0, 0))],
        out_shape=[
            jax.ShapeDtypeStruct((TOP_K, T), I32),
            jax.ShapeDtypeStruct((TOP_K, T), F32),
            jax.ShapeDtypeStruct((TOP_K, T), I32),
            jax.ShapeDtypeStruct((E, LANES), I32),
        ],
        scratch_shapes=[pltpu.VMEM((E, 1), F32)],
        compiler_params=_params("arbitrary"),
        name="router",
    )(h1p, wr_bf, br)


def _dst_kernel(start_ref, topi_ref, rank_ref, dst_ref, *, n_experts):
    topi = topi_ref[...]
    dst = rank_ref[...]
    for e in range(n_experts):
        dst = dst + jnp.where(topi == e, start_ref[e], 0)
    dst_ref[...] = dst


def _dst_rows(pad_start, topi, rank):
    spec = pl.BlockSpec(topi.shape, lambda i, *_: (0, 0))
    return pl.pallas_call(
        functools.partial(_dst_kernel, n_experts=pad_start.shape[0]),
        grid_spec=pltpu.PrefetchScalarGridSpec(num_scalar_prefetch=1, grid=(1,), in_specs=[spec, spec], out_specs=spec),
        out_shape=jax.ShapeDtypeStruct(topi.shape, I32),
        compiler_params=_params("arbitrary"),
        name="dst",
    )(pad_start, topi, rank)


def _per_block_slots(dst, tb):
    K, T = dst.shape
    return dst.reshape(K, T // tb, tb).transpose(1, 0, 2).reshape(T // tb, 1, K * tb)


def _dispatch_kernel(cnt_ref, start_ref, meta_ref, dst_ref, h1p_ref, xs_ref, zero_ref, sem, zsem, *, top_k, sub):
    i = pl.program_id(0)
    tb = h1p_ref.shape[0]

    def padding_copies(e, act):
        pos = start_ref[e] + cnt_ref[e]
        n_single = (8 - pos % 8) % 8
        for u in range(7):
            @pl.when(u < n_single)
            def _():
                act(pltpu.make_async_copy(zero_ref.at[pl.ds(0, 1)], xs_ref.at[pl.ds(pos + u, 1)], zsem))

        pos8 = pos + n_single
        rem = start_ref[e] + (cnt_ref[e] + sub - 1) // sub * sub - pos8
        size = sub // 2
        while size >= 8:
            at = pl.multiple_of(pos8 + rem // (2 * size) * (2 * size), 8)

            @pl.when(rem % (2 * size) >= size)
            def _():
                act(pltpu.make_async_copy(zero_ref.at[pl.ds(0, size)], xs_ref.at[pl.ds(at, size)], zsem))

            size //= 2

    def tail_copy(sb):
        rows = pl.ds(pl.multiple_of(sb * sub, sub), sub)
        return pltpu.make_async_copy(zero_ref, xs_ref.at[rows], zsem)

    @pl.when(i == 0)
    def _():
        zero_ref[...] = jnp.zeros_like(zero_ref)
        n_experts = cnt_ref.shape[0]
        n_total = xs_ref.shape[0] // sub
        for act in (lambda copy: copy.start(), lambda copy: copy.wait()):
            def per_expert(e, carry, act=act):
                padding_copies(e, act)
                return carry

            def per_tail(sb, carry, act=act):
                act(tail_copy(sb))
                return carry

            lax.fori_loop(0, n_experts, per_expert, 0)
            lax.fori_loop(meta_ref[1], n_total, per_tail, 0)

    def start(c, carry):
        for u in range(DMA_ISSUE_UNROLL):
            r = c * DMA_ISSUE_UNROLL + u
            for k in range(top_k):
                row = dst_ref[0, k * tb + r]
                pltpu.make_async_copy(h1p_ref.at[pl.ds(r, 1)], xs_ref.at[pl.ds(row, 1)], sem).start(priority=k % 2)
        return carry

    lax.fori_loop(0, tb // DMA_ISSUE_UNROLL, start, 0)
    for k in range(top_k):
        pltpu.make_async_copy(h1p_ref, xs_ref.at[pl.ds(0, tb)], sem).wait()


def _dispatch(counts, pad_start, meta, dst, h1p, n_rows):
    T, W = h1p.shape
    tb = min(DISPATCH_ROWS, T)
    slots = _per_block_slots(dst, tb)
    return pl.pallas_call(
        functools.partial(_dispatch_kernel, top_k=dst.shape[0], sub=EXPERT_SUB_ROWS),
        grid_spec=pltpu.PrefetchScalarGridSpec(
            num_scalar_prefetch=3,
            grid=(T // tb,),
            in_specs=[
                pl.BlockSpec((None, 1, slots.shape[2]), lambda i, *_: (i, 0, 0), memory_space=pltpu.SMEM),
                pl.BlockSpec((tb, W), lambda i, *_: (i, 0)),
            ],
            out_specs=pl.BlockSpec(memory_space=pl.ANY),
            scratch_shapes=[pltpu.VMEM((EXPERT_SUB_ROWS, W), U32), pltpu.SemaphoreType.DMA(()),
                            pltpu.SemaphoreType.DMA(())],
        ),
        out_shape=jax.ShapeDtypeStruct((n_rows, W), U32),
        compiler_params=_params("arbitrary"),
        name="dispatch",
    )(counts, pad_start, meta, slots, h1p)


def _experts_kernel(ge_ref, row0_ref, nsub_ref, meta_ref, bu_ref, bd_ref, wup_ref, wdn_ref, xs_ref, ys_ref,
                    xbuf_ref, acc_ref, wgf_ref, wlf_ref, wdf_ref, wgb_ref, wlb_ref, wdb_ref, stage_ref,
                    wsem, xsem, ysem, *, n_f):
    s = pl.program_id(0)
    n_groups = meta_ref[0]
    sub, half = stage_ref.shape[1:]
    tf = wdf_ref.shape[0]
    ff = n_f * tf

    def y_copy(ib, row):
        return pltpu.make_async_copy(acc_ref.at[ib], ys_ref.at[pl.ds(pl.multiple_of(row, sub), sub)], ysem)

    def weight_copies(e, j):
        glu_cols = pl.ds(pl.multiple_of(j * tf, tf), tf)
        lin_cols = pl.ds(pl.multiple_of(ff + j * tf, tf), tf)
        return (pltpu.make_async_copy(wup_ref.at[e, :, glu_cols], wgf_ref, wsem.at[0]),
                pltpu.make_async_copy(wup_ref.at[e, :, lin_cols], wlf_ref, wsem.at[1]),
                pltpu.make_async_copy(wdn_ref.at[e, glu_cols, :], wdf_ref, wsem.at[2]))

    def cast_weights(slot):
        wgb_ref[slot] = wgf_ref[...].astype(BF16)
        wlb_ref[slot] = wlf_ref[...].astype(BF16)
        wdb_ref[slot] = wdf_ref[...].astype(BF16)

    @pl.when(s < n_groups)
    def _():
        e = ge_ref[s]
        e_after = ge_ref[jnp.minimum(s + 1, n_groups - 1)]
        n_sub = nsub_ref[s]
        row0 = row0_ref[s]

        def x_copy(ib, slot):
            rows = pl.ds(pl.multiple_of(row0 + ib * sub, sub), sub)
            return pltpu.make_async_copy(xs_ref.at[rows], stage_ref.at[slot], xsem.at[slot])

        x_copy(0, 0).start()

        @pl.when(s == 0)
        def _():
            for copy in weight_copies(e, 0):
                copy.start()
            for copy in weight_copies(e, 0):
                copy.wait()
            cast_weights(0)

        def item(j, carry):
            slot = j % 2
            is_last = j == n_f - 1
            next_j = jnp.where(is_last, 0, j + 1)
            next_e = jnp.where(is_last, e_after, e)
            for copy in weight_copies(next_e, next_j):
                copy.start()

            def load_tokens(ib):
                xslot = ib % 2
                x_copy(ib, xslot).wait()

                @pl.when(ib + 1 < n_sub)
                def _():
                    x_copy(ib + 1, 1 - xslot).start()

                lo, hi = _unpack_bf16_pair(stage_ref[xslot])
                xbuf_ref[ib, :, :half] = lo
                xbuf_ref[ib, :, half:] = hi
                acc_ref[ib] = jnp.broadcast_to(bd_ref[...], acc_ref.shape[1:])

            def accumulate(ib, n):
                width = xbuf_ref.shape[2]
                x = xbuf_ref[pl.ds(ib, n)].reshape(n * sub, width)
                glu = jnp.dot(x, wgb_ref[slot], preferred_element_type=F32) + bu_ref[pl.ds(j, 1), :]
                lin = jnp.dot(x, wlb_ref[slot], preferred_element_type=F32) + bu_ref[pl.ds(n_f + j, 1), :]
                glu = jnp.minimum(glu, SWIGLU_LIMIT)
                lin = jnp.clip(lin, -SWIGLU_LIMIT, SWIGLU_LIMIT)
                act = glu / (1.0 + jnp.exp(-SWIGLU_ALPHA * glu)) * (lin + 1.0)
                part = jnp.dot(act.astype(BF16), wdb_ref[slot], preferred_element_type=F32)
                acc_ref[pl.ds(ib, n)] += part.reshape(n, sub, width)

            def process(ib0, count, with_next_weights):
                if count <= 2:
                    chains = [(ib, 1) for ib in range(count)]
                else:
                    chains = [(ib, 2) for ib in range(0, count - 1, 2)] + [(count - 1, 1)] * (count % 2)

                @pl.when(j == 0)
                def _():
                    for ib in range(count):
                        load_tokens(ib0 + ib)

                if with_next_weights:
                    for copy in weight_copies(next_e, next_j):
                        copy.wait()
                    cast_weights(1 - slot)
                for first, n in chains:
                    accumulate(ib0 + first, n)

                @pl.when(is_last)
                def _():
                    for ib in range(count):
                        y_copy(ib0 + ib, row0 + (ib0 + ib) * sub).start()

            n_units = (n_sub - 1) // EXPERT_UNIT_SUBS

            def unit(p, carry):
                process(EXPERT_UNIT_SUBS * p, EXPERT_UNIT_SUBS, False)
                return carry

            lax.fori_loop(0, n_units, unit, 0)
            for rest in range(1, EXPERT_UNIT_SUBS + 1):
                @pl.when(n_sub - EXPERT_UNIT_SUBS * n_units == rest)
                def _(rest=rest):
                    process(EXPERT_UNIT_SUBS * n_units, rest, True)

            return carry

        lax.fori_loop(0, n_f, item, 0)

        def drain(ib, carry):
            y_copy(ib, row0 + ib * sub).wait()
            return carry

        lax.fori_loop(0, n_sub, drain, 0)

    @pl.when(s == pl.num_programs(0) - 1)
    def _():
        n_total = ys_ref.shape[0] // sub
        acc_ref[0] = jnp.zeros(acc_ref.shape[1:], F32)

        def fill(ib, carry):
            y_copy(0, ib * sub).start()
            return carry

        def drain(ib, carry):
            y_copy(0, ib * sub).wait()
            return carry

        lax.fori_loop(meta_ref[1], n_total, fill, 0)
        lax.fori_loop(meta_ref[1], n_total, drain, 0)


def _experts(group_expert, group_row0, group_nsub, meta, xs, w_up, b_up, w_down, b_down):
    R, half = xs.shape
    E, D, ff2 = w_up.shape
    ff = ff2 // 2
    sub = EXPERT_SUB_ROWS
    gb = EXPERT_GROUP_SUBS
    tf = min(EXPERT_FF_COLS, ff)
    n_f = ff // tf
    assert n_f % 2 == 0
    grp = lambda s, ge, r0, ns, meta: (ge[jnp.minimum(s, meta[0] - 1)], 0, 0)
    return pl.pallas_call(
        functools.partial(_experts_kernel, n_f=n_f),
        grid_spec=pltpu.PrefetchScalarGridSpec(
            num_scalar_prefetch=4,
            grid=(group_expert.shape[0],),
            in_specs=[
                pl.BlockSpec((None, 2 * n_f, tf), grp),
                pl.BlockSpec((None, 1, D), grp),
                pl.BlockSpec(memory_space=pl.ANY),
                pl.BlockSpec(memory_space=pl.ANY),
                pl.BlockSpec(memory_space=pl.ANY),
            ],
            out_specs=pl.BlockSpec(memory_space=pl.ANY),
            scratch_shapes=[
                pltpu.VMEM((gb, sub, D), BF16),
                pltpu.VMEM((gb, sub, D), F32),
                pltpu.VMEM((D, tf), F32),
                pltpu.VMEM((D, tf), F32),
                pltpu.VMEM((tf, D), F32),
                pltpu.VMEM((2, D, tf), BF16),
                pltpu.VMEM((2, D, tf), BF16),
                pltpu.VMEM((2, tf, D), BF16),
                pltpu.VMEM((2, sub, half), U32),
                pltpu.SemaphoreType.DMA((3,)),
                pltpu.SemaphoreType.DMA((2,)),
                pltpu.SemaphoreType.DMA(()),
            ],
        ),
        out_shape=jax.ShapeDtypeStruct((R, D), F32),
        compiler_params=pltpu.CompilerParams(dimension_semantics=("arbitrary",),
                                             vmem_limit_bytes=EXPERT_VMEM_LIMIT_BYTES),
        name="experts",
    )(group_expert, group_row0, group_nsub, meta, b_up.reshape(E, 2 * n_f, tf), b_down.reshape(E, 1, D),
      w_up, w_down, xs)


def _combine_kernel(dst_ref, nxt_ref, gate_ref, h1_ref, lng_ref, lnb_ref, ys_ref, out_ref, buf0_ref, buf1_ref,
                    sem, *, top_k):
    i = pl.program_id(0)
    last = pl.num_programs(0) - 1
    tb = h1_ref.shape[0]

    def row_copy(slots_ref, buf_ref, slot, r, k):
        row = slots_ref[0, k * tb + r]
        return pltpu.make_async_copy(ys_ref.at[pl.ds(row, 1)], buf_ref.at[k, pl.ds(r, 1)], sem.at[slot])

    def wait_block(buf_ref, slot):
        for k in range(top_k):
            pltpu.make_async_copy(ys_ref.at[pl.ds(0, tb)], buf_ref.at[k], sem.at[slot]).wait()

    @pl.when(i == 0)
    def _():
        def start(c, carry):
            for u in range(DMA_ISSUE_UNROLL):
                for k in range(top_k):
                    row_copy(dst_ref, buf0_ref, 0, c * DMA_ISSUE_UNROLL + u, k).start(priority=k % 2)
            return carry

        lax.fori_loop(0, tb // DMA_ISSUE_UNROLL, start, 0)

    def step(cur_ref, cur, nxt_buf_ref, nxt):
        wait_block(cur_ref, cur)
        for r in range(tb):
            for k in range(top_k):
                row_copy(nxt_ref, nxt_buf_ref, nxt, r, k).start(priority=k % 2)

        diag = lax.broadcasted_iota(I32, (tb, tb), 0) == lax.broadcasted_iota(I32, (tb, tb), 1)
        ffn = jnp.zeros(h1_ref.shape, F32)
        for k in range(top_k):
            gate = jnp.sum(jnp.where(diag, gate_ref[k:k + 1, :], 0.0), axis=1, keepdims=True)
            ffn = ffn + cur_ref[k] * gate
        out_ref[...] = _layer_norm(DEEPNORM_ALPHA * h1_ref[...] + ffn, lng_ref[...], lnb_ref[...])

        @pl.when(i == last)
        def _():
            wait_block(nxt_buf_ref, nxt)

    @pl.when(i % 2 == 0)
    def _():
        step(buf0_ref, 0, buf1_ref, 1)

    @pl.when(i % 2 == 1)
    def _():
        step(buf1_ref, 1, buf0_ref, 0)


def _combine(dst, gates, h1, ln_g, ln_b, ys):
    T, D = h1.shape
    K = dst.shape[0]
    tb = min(COMBINE_ROWS, T)
    slots = _per_block_slots(dst, tb)
    n_blocks = T // tb
    return pl.pallas_call(
        functools.partial(_combine_kernel, top_k=K),
        grid=(n_blocks,),
        in_specs=[
            pl.BlockSpec((None, 1, K * tb), lambda i: (i, 0, 0), memory_space=pltpu.SMEM),
            pl.BlockSpec((None, 1, K * tb), lambda i: (jnp.minimum(i + 1, n_blocks - 1), 0, 0),
                         memory_space=pltpu.SMEM),
            pl.BlockSpec((K, tb), lambda i: (0, i)),
            pl.BlockSpec((tb, D), lambda i: (i, 0)),
            pl.BlockSpec((1, D), lambda i: (0, 0)),
            pl.BlockSpec((1, D), lambda i: (0, 0)),
            pl.BlockSpec(memory_space=pl.ANY),
        ],
        out_specs=pl.BlockSpec((tb, D), lambda i: (i, 0)),
        out_shape=jax.ShapeDtypeStruct((T, D), F32),
        scratch_shapes=[pltpu.VMEM((K, tb, D), F32), pltpu.VMEM((K, tb, D), F32),
                        pltpu.SemaphoreType.DMA((2,))],
        compiler_params=_params("arbitrary"),
        name="combine",
    )(slots, slots, gates, h1, ln_g, ln_b, ys)


def _expert_schedule(counts, n_tokens_routed):
    sub, gb = EXPERT_SUB_ROWS, EXPERT_GROUP_SUBS
    n_experts = counts.shape[0]
    max_subs = n_tokens_routed // sub + n_experts
    max_groups = n_experts + max_subs // gb
    n_sub = (counts + sub - 1) // sub
    sub_end = jnp.cumsum(n_sub)
    sub_start = sub_end - n_sub
    n_grp = (n_sub + gb - 1) // gb
    grp_end = jnp.cumsum(n_grp)
    grp_start = grp_end - n_grp
    s = jnp.arange(max_groups, dtype=I32)
    expert = jnp.minimum(jnp.sum(grp_end[None, :] <= s[:, None], axis=1), n_experts - 1).astype(I32)
    onehot = (expert[:, None] == jnp.arange(n_experts, dtype=I32)[None, :]).astype(I32)
    pick = lambda v: jnp.sum(onehot * v[None, :], axis=1)
    local = s - pick(grp_start)
    row0 = (pick(sub_start) + local * gb) * sub
    nsub = jnp.clip(pick(n_sub) - local * gb, 0, gb)
    meta = jnp.stack([grp_end[-1], sub_end[-1]])
    to_i32 = lambda a: a.astype(I32)
    return to_i32(sub_start * sub), to_i32(expert), to_i32(row0), to_i32(nsub), to_i32(meta), max_subs * sub


def kernel(x, ln0_g, ln0_b, w_in, ret_decay_fwd, ret_decay_bwd, ret_gn_g, w_out, ln1_g, ln1_b, w_router, b_router,
           w_up, b_up, w_down, b_down, ln2_g, ln2_b):
    B, S, D = x.shape
    assert B == 1 and w_in.shape[0] == DEPTH
    T = S
    fw = D // 2
    rw = D - fw
    gw = fw // N_FOURIER_GROUPS
    head_dim = rw // N_RET_HEADS
    E = w_router.shape[-1]
    row = lambda a: a.reshape(1, -1)

    h, uf, ur = _in_proj(x.reshape(T, D), row(ln0_g), row(ln0_b), w_in[0].astype(BF16), head_dim)
    yf = _fourier_mix(uf, gw)
    lg_fwd = -jnp.exp(ret_decay_fwd[0].astype(F32))
    lg_bwd = -jnp.exp(ret_decay_bwd[0].astype(F32))
    o_f, o_b = _retention(ur, lg_fwd, lg_bwd, N_RET_HEADS, head_dim)
    h1, h1p = _out_proj(yf, o_f, o_b, ur, row(ret_gn_g[0]), h, w_out[0].astype(BF16), row(ln1_g[0]), row(ln1_b[0]),
                        N_RET_HEADS)
    topi, gates, rank, cnt = _router(h1p, w_router[0].T, b_router[0].reshape(E, 1))

    counts = cnt[:, 0]
    pad_start, group_expert, group_row0, group_nsub, meta, n_rows = _expert_schedule(counts, T * TOP_K)
    dst = _dst_rows(pad_start, topi, rank)
    xs = _dispatch(counts, pad_start, meta, dst, h1p, n_rows)
    ys = _experts(group_expert, group_row0, group_nsub, meta, xs, w_up.reshape(w_up.shape[1:]), b_up[0],
                  w_down.reshape(w_down.shape[1:]), b_down[0])
    out = _combine(dst, gates, h1, row(ln2_g[0]), row(ln2_b[0]), ys)
    return out.reshape(B, S, D)
```

```python
import functools

import numpy as np
import jax
import jax.numpy as jnp
from jax import lax
from jax.experimental import pallas as pl
from jax.experimental.pallas import tpu as pltpu

F32 = jnp.float32
BF16 = jnp.bfloat16
I32 = jnp.int32
U32 = jnp.uint32

N_FOURIER_GROUPS = 8
N_RET_HEADS = 4
ROPE_BASE = 10000.0
TOP_K = 4
SWIGLU_LIMIT = 7.0
SWIGLU_ALPHA = 1.702
LN_EPS = 1e-5
GN_EPS = 1e-6
DEPTH = 1
DEEPNORM_ALPHA = (2.0 * DEPTH) ** 0.25

V7X_VMEM_BYTES = 64 * 2**20
VMEM_LIMIT_BYTES = V7X_VMEM_BYTES * 3 // 4
LANES = 128

IN_PROJ_ROWS = 512
IN_PROJ_PART_ROWS = 256
FOURIER_COLS_PER_STEP = 16
RET_CHUNK = 256
OUT_PROJ_ROWS = 512
OUT_PROJ_PART_ROWS = 256
ROUTER_ROWS = 1024
ROUTER_PART_ROWS = 256
DISPATCH_ROWS = 512
EXPERT_SUB_ROWS = 256
EXPERT_GROUP_SUBS = 9
EXPERT_UNIT_SUBS = 4
EXPERT_FF_COLS = 256
EXPERT_VMEM_LIMIT_BYTES = V7X_VMEM_BYTES * 7 // 8
COMBINE_ROWS = 128
DMA_ISSUE_UNROLL = 8


def _params(*semantics):
    return pltpu.CompilerParams(dimension_semantics=semantics, vmem_limit_bytes=VMEM_LIMIT_BYTES)


def _layer_norm(x, g, b):
    mu = jnp.mean(x, axis=-1, keepdims=True)
    xc = x - mu
    var = jnp.mean(xc * xc, axis=-1, keepdims=True)
    return xc * lax.rsqrt(var + LN_EPS) * g + b


def _in_proj_kernel(x_ref, g_ref, b_ref, w_ref, base_ref, cos_ref, sin_ref, h_ref, uf_ref, ur_ref, hb_ref, *,
                    head_dim, part_rows):
    j = pl.program_id(1)
    tm = x_ref.shape[0]
    parts = [slice(r0, r0 + part_rows) for r0 in range(0, tm, part_rows)]

    @pl.when(j == 0)
    def _():
        for rows in parts:
            h = _layer_norm(x_ref[rows, :], g_ref[...], b_ref[...])
            h_ref[rows, :] = h
            hb = h.astype(BF16)
            hb_ref[rows, :] = hb
            uf_ref[rows, :] = jnp.dot(hb, w_ref[...], preferred_element_type=F32).astype(BF16)

    @pl.when((j == 1) | (j == 2))
    def _():
        half = head_dim // 2
        cos0, sin0 = base_ref[0:1, :], base_ref[1:2, :]
        for rows in parts:
            acc = jnp.dot(hb_ref[rows, :], w_ref[...], preferred_element_type=F32)
            cos = cos0 * cos_ref[rows, :] - sin0 * sin_ref[rows, :]
            sin = sin0 * cos_ref[rows, :] + cos0 * sin_ref[rows, :]
            for lo in range(0, acc.shape[1], head_dim):
                x1 = acc[:, lo:lo + half]
                x2 = acc[:, lo + half:lo + head_dim]
                ur_ref[rows, lo:lo + half] = (x1 * cos - x2 * sin).astype(BF16)
                ur_ref[rows, lo + half:lo + head_dim] = (x2 * cos + x1 * sin).astype(BF16)

    @pl.when(j >= 3)
    def _():
        for rows in parts:
            ur_ref[rows, :] = jnp.dot(hb_ref[rows, :], w_ref[...], preferred_element_type=F32).astype(BF16)


def _in_proj(x2, ln_g, ln_b, w_bf, head_dim):
    T, D = x2.shape
    tn = D // 2
    n_col = w_bf.shape[1] // tn
    assert n_col == 5 and w_bf.shape[1] == 5 * tn
    tm = min(IN_PROJ_ROWS, T)
    half = head_dim // 2
    inv = ROPE_BASE ** (-jnp.arange(half, dtype=F32) / half)
    ang0 = (jnp.arange(T // tm, dtype=I32) * tm).astype(F32)[:, None] * inv[None, :]
    ang1 = jnp.arange(tm, dtype=I32).astype(F32)[:, None] * inv[None, :]
    base = jnp.stack([jnp.cos(ang0), jnp.sin(ang0)], axis=1)
    cos, sin = jnp.cos(ang1), jnp.sin(ang1)
    return pl.pallas_call(
        functools.partial(_in_proj_kernel, head_dim=head_dim, part_rows=min(IN_PROJ_PART_ROWS, tm)),
        grid=(T // tm, n_col),
        in_specs=[
            pl.BlockSpec((tm, D), lambda i, j: (i, 0)),
            pl.BlockSpec((1, D), lambda i, j: (0, 0)),
            pl.BlockSpec((1, D), lambda i, j: (0, 0)),
            pl.BlockSpec((D, tn), lambda i, j: (0, j)),
            pl.BlockSpec((None, 2, half), lambda i, j: (i, 0, 0)),
            pl.BlockSpec((tm, half), lambda i, j: (0, 0)),
            pl.BlockSpec((tm, half), lambda i, j: (0, 0)),
        ],
        out_specs=[
            pl.BlockSpec((tm, D), lambda i, j: (i, 0)),
            pl.BlockSpec((tm, tn), lambda i, j: (i, 0)),
            pl.BlockSpec((tm, tn), lambda i, j: (i, jnp.maximum(j - 1, 0))),
        ],
        out_shape=[
            jax.ShapeDtypeStruct((T, D), F32),
            jax.ShapeDtypeStruct((T, tn), BF16),
            jax.ShapeDtypeStruct((T, 4 * tn), BF16),
        ],
        scratch_shapes=[pltpu.VMEM((tm, D), BF16)],
        compiler_params=_params("parallel", "arbitrary"),
        name="in_proj",
    )(x2, ln_g, ln_b, w_bf, base, cos, sin)


def _fourier_tables(n1, n2, gw):
    def cos_sin(n, r, c):
        ang = 2.0 * np.pi * ((np.outer(r, c) % n).astype(np.float64)) / n
        return np.cos(ang), np.sin(ang)

    cg, sg = cos_sin(gw, np.arange(gw), np.arange(gw))
    c1, s1 = cos_sin(n1, np.arange(n1), np.arange(n1))
    c3, s3 = cos_sin(n2, np.arange(n2), np.arange(n2))
    tc, ts = cos_sin(n1 * n2, np.arange(n1), np.arange(n2))
    cs = np.concatenate([cg, sg], axis=1)
    m1 = np.block([[c1, -s1], [-s1, -c1]])
    m3 = np.concatenate([c3, s3], axis=1)
    as32 = lambda a: jnp.asarray(a.astype(np.float32))
    return as32(cs).astype(BF16), as32(m1).astype(BF16), as32(m3).astype(BF16), as32(tc), as32(ts)


def _fourier_a_kernel(uf_ref, cs_ref, m1_ref, twc_ref, tws_ref, zz_ref, *, nb, gw):
    bb = pl.program_id(0)
    n1 = uf_ref.shape[0]
    width = uf_ref.shape[1] // nb
    cs = cs_ref[...]
    m1 = m1_ref[...]
    lane = lax.broadcasted_iota(I32, twc_ref.shape, 1)
    for t in range(nb):
        x = uf_ref[:, t * width:(t + 1) * width]
        a_parts, b_parts = [], []
        for lo in range(0, width, gw):
            ab = jnp.dot(x[:, lo:lo + gw], cs, preferred_element_type=F32)
            a_parts.append(ab[:, :gw])
            b_parts.append(ab[:, gw:])
        v = jnp.concatenate([jnp.concatenate(a_parts, axis=1), jnp.concatenate(b_parts, axis=1)], axis=0)
        z = jnp.dot(m1, v.astype(BF16), preferred_element_type=F32)
        zr, zi = z[:n1], z[n1:]
        sel = lane == bb * nb + t
        tc = jnp.sum(jnp.where(sel, twc_ref[...], 0.0), axis=1, keepdims=True)
        ts = jnp.sum(jnp.where(sel, tws_ref[...], 0.0), axis=1, keepdims=True)
        base = t * 2 * width
        zz_ref[:, base:base + width] = (zr * tc + zi * ts).astype(BF16)
        zz_ref[:, base + width:base + 2 * width] = (zi * tc - zr * ts).astype(BF16)


def _fourier_b_kernel(zz_ref, m3_ref, yf_ref, *, scale):
    nc = zz_ref.shape[0]
    width = zz_ref.shape[2] // 2
    m3 = m3_ref[...]
    for t in range(nc):
        z = zz_ref[t]
        stacked = jnp.concatenate([z[:, :width], z[:, width:]], axis=0)
        x = jnp.dot(m3, stacked, preferred_element_type=F32)
        yf_ref[:, t * width:(t + 1) * width] = (x * scale).astype(BF16)


def _fourier_mix(uf, gw):
    T, F = uf.shape
    n1 = min(LANES, T)
    n2 = T // n1
    assert n1 * n2 == T
    cs, m1, m3, twc, tws = _fourier_tables(n1, n2, gw)
    nb = min(FOURIER_COLS_PER_STEP, n2)
    nc = min(FOURIER_COLS_PER_STEP, n1)
    zz = pl.pallas_call(
        functools.partial(_fourier_a_kernel, nb=nb, gw=gw),
        grid=(n2 // nb,),
        in_specs=[
            pl.BlockSpec((n1, nb * F), lambda b: (0, b)),
            pl.BlockSpec(cs.shape, lambda b: (0, 0)),
            pl.BlockSpec(m1.shape, lambda b: (0, 0)),
            pl.BlockSpec(twc.shape, lambda b: (0, 0)),
            pl.BlockSpec(tws.shape, lambda b: (0, 0)),
        ],
        out_specs=pl.BlockSpec((n1, nb * 2 * F), lambda b: (0, b)),
        out_shape=jax.ShapeDtypeStruct((n1, n2 * 2 * F), BF16),
        compiler_params=_params("parallel"),
        name="fourier_a",
    )(uf.reshape(n1, n2 * F), cs, m1, twc, tws)
    yf = pl.pallas_call(
        functools.partial(_fourier_b_kernel, scale=float(1.0 / np.sqrt(T * gw))),
        grid=(n1 // nc,),
        in_specs=[
            pl.BlockSpec((nc, n2, 2 * F), lambda c: (c, 0, 0)),
            pl.BlockSpec(m3.shape, lambda c: (0, 0)),
        ],
        out_specs=pl.BlockSpec((n2, nc * F), lambda c: (0, c)),
        out_shape=jax.ShapeDtypeStruct((n2, n1 * F), BF16),
        compiler_params=_params("parallel"),
        name="fourier_b",
    )(zz.reshape(n1, n2, 2 * F), m3)
    return yf.reshape(T, F)


_TAB_K_TO_END_F, _TAB_Q_FROM_START_F, _TAB_K_TO_END_B, _TAB_Q_FROM_START_B, _TAB_CHUNK_F, _TAB_CHUNK_B = range(6)


def _retention_kernel(lgf_ref, lgb_ref, qf_ref, kf_ref, vf_ref, qb_ref, kb_ref, vb_ref, of_ref, ob_ref,
                      sf_ref, sb_ref, dmat_ref, tab_ref, *, kscale, n_heads):
    n = pl.program_id(0)
    C = qf_ref.shape[0]
    hd = qf_ref.shape[1] // n_heads

    @pl.when(n == 0)
    def _():
        r = lax.broadcasted_iota(I32, (C, C), 0).astype(F32)
        c = lax.broadcasted_iota(I32, (C, C), 1).astype(F32)
        diff = r - c
        rr = lax.broadcasted_iota(I32, (C, hd), 0).astype(F32)
        for head in range(n_heads):
            lgf = lgf_ref[head]
            lgb = lgb_ref[head]
            dmat_ref[head] = kscale * jnp.where(diff >= 0.0, jnp.exp(lgf * jnp.maximum(diff, 0.0)),
                                                jnp.exp(lgb * jnp.maximum(-diff, 0.0)))
            tab_ref[head, _TAB_K_TO_END_F] = kscale * jnp.exp(lgf * (C - 1.0 - rr))
            tab_ref[head, _TAB_Q_FROM_START_F] = jnp.exp(lgf * (rr + 1.0))
            tab_ref[head, _TAB_K_TO_END_B] = kscale * jnp.exp(lgb * rr)
            tab_ref[head, _TAB_Q_FROM_START_B] = jnp.exp(lgb * (C - rr))
            tab_ref[head, _TAB_CHUNK_F] = jnp.exp(jnp.full((C, hd), lgf * C, F32))
            tab_ref[head, _TAB_CHUNK_B] = jnp.exp(jnp.full((C, hd), lgb * C, F32))
        sf_ref[...] = jnp.zeros_like(sf_ref)
        sb_ref[...] = jnp.zeros_like(sb_ref)

    nt = (((1,), (1,)), ((), ()))
    tn = (((0,), (0,)), ((), ()))

    for head in range(n_heads):
        cols = slice(head * hd, (head + 1) * hd)
        q = qf_ref[:, cols]
        k = kf_ref[:, cols]
        v = vf_ref[:, cols]
        scores = lax.dot_general(q, k, nt, preferred_element_type=F32) * dmat_ref[head]
        intra = jnp.dot(scores.astype(BF16), v, preferred_element_type=F32)
        cross = (jnp.dot(q, sf_ref[head].astype(BF16), preferred_element_type=F32)
                 * tab_ref[head, _TAB_Q_FROM_START_F])
        of_ref[:, cols] = (intra + cross).astype(of_ref.dtype)
        v_dec = (v.astype(F32) * tab_ref[head, _TAB_K_TO_END_F]).astype(BF16)
        sf_ref[head] = (sf_ref[head] * tab_ref[head, _TAB_CHUNK_F]
                        + lax.dot_general(k, v_dec, tn, preferred_element_type=F32))

        q = qb_ref[:, cols]
        k = kb_ref[:, cols]
        v = vb_ref[:, cols]
        ob_ref[:, cols] = (jnp.dot(q, sb_ref[head].astype(BF16), preferred_element_type=F32)
                           * tab_ref[head, _TAB_Q_FROM_START_B]).astype(ob_ref.dtype)
        v_dec = (v.astype(F32) * tab_ref[head, _TAB_K_TO_END_B]).astype(BF16)
        sb_ref[head] = (sb_ref[head] * tab_ref[head, _TAB_CHUNK_B]
                        + lax.dot_general(k, v_dec, tn, preferred_element_type=F32))


def _retention(ur, lg_fwd, lg_bwd, n_heads, head_dim):
    T = ur.shape[0]
    C = min(RET_CHUNK, T)
    assert C == head_dim
    N = T // C
    H = n_heads
    rw = H * head_dim
    fwd = lambda col: (lambda n, *_: (n, col))
    bwd = lambda col: (lambda n, *_: (N - 1 - n, col))
    blk = lambda imap: pl.BlockSpec((C, rw), imap)
    return pl.pallas_call(
        functools.partial(_retention_kernel, kscale=float(head_dim) ** -0.5, n_heads=H),
        grid_spec=pltpu.PrefetchScalarGridSpec(
            num_scalar_prefetch=2,
            grid=(N,),
            in_specs=[blk(fwd(0)), blk(fwd(1)), blk(fwd(2)), blk(bwd(0)), blk(bwd(1)), blk(bwd(2))],
            out_specs=[blk(fwd(0)), blk(bwd(0))],
            scratch_shapes=[
                pltpu.VMEM((H, head_dim, head_dim), F32),
                pltpu.VMEM((H, head_dim, head_dim), F32),
                pltpu.VMEM((H, C, C), F32),
                pltpu.VMEM((H, 6, C, head_dim), F32),
            ],
        ),
        out_shape=[jax.ShapeDtypeStruct((T, rw), BF16)] * 2,
        compiler_params=_params("arbitrary"),
        name="retention",
    )(lg_fwd, lg_bwd, ur, ur, ur, ur, ur, ur)


def _pack_bf16_pair(lo, hi):
    lo_bits = lax.bitcast_convert_type(lo.astype(BF16).astype(F32), U32)
    hi_bits = lax.bitcast_convert_type(hi.astype(BF16).astype(F32), U32)
    return (hi_bits & jnp.uint32(0xFFFF0000)) | (lo_bits >> 16)


def _unpack_bf16_pair(words):
    lo = lax.bitcast_convert_type(words << 16, F32).astype(BF16)
    hi = lax.bitcast_convert_type(words & jnp.uint32(0xFFFF0000), F32).astype(BF16)
    return lo, hi


def _out_proj_kernel(yf_ref, of_ref, ob_ref, g_ref, gng_ref, h_ref, wo_ref, lng_ref, lnb_ref, h1_ref, h1p_ref, *,
                     n_heads, part_rows):
    tm, rw = of_ref.shape
    hd = rw // n_heads
    fw = yf_ref.shape[1]
    D = h_ref.shape[1]
    for r0 in range(0, tm, part_rows):
        rows = slice(r0, r0 + part_rows)
        o = of_ref[rows, :].astype(F32) + ob_ref[rows, :].astype(F32)
        parts = []
        for lo in range(0, rw, hd):
            oh = o[:, lo:lo + hd]
            mu = jnp.mean(oh, axis=-1, keepdims=True)
            d = oh - mu
            var = jnp.mean(d * d, axis=-1, keepdims=True)
            parts.append(d * lax.rsqrt(var + GN_EPS))
        on = jnp.concatenate(parts, axis=-1) * gng_ref[...]
        g = g_ref[rows, :].astype(F32)
        yr = (g / (1.0 + jnp.exp(-g))) * on

        mix = (jnp.dot(yf_ref[rows, :], wo_ref[:fw, :], preferred_element_type=F32)
               + jnp.dot(yr.astype(BF16), wo_ref[fw:, :], preferred_element_type=F32))
        h1 = _layer_norm(DEEPNORM_ALPHA * h_ref[rows, :] + mix, lng_ref[...], lnb_ref[...])
        h1_ref[rows, :] = h1
        h1p_ref[rows, :] = _pack_bf16_pair(h1[:, :D // 2], h1[:, D // 2:])


def _out_proj(yf, o_f, o_b, ur, gn_g, h, wo_bf, ln_g, ln_b, n_heads):
    T, D = h.shape
    fw = yf.shape[1]
    rw = o_f.shape[1]
    tm = min(OUT_PROJ_ROWS, T)
    gate_col = ur.shape[1] // rw - 1
    row = lambda w: pl.BlockSpec((tm, w), lambda i: (i, 0))
    full = lambda a: pl.BlockSpec(a.shape, lambda i: (0,) * a.ndim)
    return pl.pallas_call(
        functools.partial(_out_proj_kernel, n_heads=n_heads, part_rows=min(OUT_PROJ_PART_ROWS, tm)),
        grid=(T // tm,),
        in_specs=[row(fw), row(rw), row(rw), pl.BlockSpec((tm, rw), lambda i: (i, gate_col)), full(gn_g), row(D),
                  full(wo_bf), full(ln_g), full(ln_b)],
        out_specs=[row(D), row(D // 2)],
        out_shape=[jax.ShapeDtypeStruct((T, D), F32), jax.ShapeDtypeStruct((T, D // 2), U32)],
        compiler_params=_params("parallel"),
        name="out_proj",
    )(yf, o_f, o_b, ur, gn_g, h, wo_bf, ln_g, ln_b)


def _router_kernel(h1p_ref, wr_ref, br_ref, topi_ref, gate_ref, rank_ref, cnt_ref, carry_ref, *, top_k, part_rows):
    i = pl.program_id(0)
    tm, half = h1p_ref.shape
    E = wr_ref.shape[0]
    tp = part_rows

    @pl.when(i == 0)
    def _():
        carry_ref[...] = jnp.zeros_like(carry_ref)

    eidx = lax.broadcasted_iota(I32, (E, tp), 0)
    earlier = lax.broadcasted_iota(I32, (tp, tp), 0) < lax.broadcasted_iota(I32, (tp, tp), 1)
    earlier = jnp.where(earlier, 1.0, 0.0).astype(BF16)
    carry = carry_ref[...]

    for r0 in range(0, tm, tp):
        rows = slice(r0, r0 + tp)
        h_lo, h_hi = _unpack_bf16_pair(h1p_ref[rows, :])
        nt = (((1,), (1,)), ((), ()))
        logits = (lax.dot_general(wr_ref[:, :half], h_lo, nt, preferred_element_type=F32)
                  + lax.dot_general(wr_ref[:, half:], h_hi, nt, preferred_element_type=F32)) + br_ref[...]
        cur = logits
        sel_idx, sel_val = [], []
        for _ in range(top_k):
            m = jnp.max(cur, axis=0, keepdims=True)
            idx = jnp.min(jnp.where(cur == m, eidx, E), axis=0, keepdims=True)
            sel_idx.append(idx)
            sel_val.append(m)
            cur = jnp.where(eidx == idx, -jnp.inf, cur)
        ex = [jnp.exp(val - sel_val[0]) for val in sel_val]
        denom = ex[0]
        for e_r in ex[1:]:
            denom = denom + e_r
        topi_ref[:, rows] = jnp.concatenate(sel_idx, axis=0)
        gate_ref[:, rows] = jnp.concatenate([e_r / denom for e_r in ex], axis=0)

        member = jnp.zeros((E, tp), F32)
        for idx in sel_idx:
            member = member + jnp.where(eidx == idx, 1.0, 0.0)
        rank_full = jnp.dot(member.astype(BF16), earlier, preferred_element_type=F32) + carry
        ranks = [jnp.sum(jnp.where(eidx == idx, rank_full, 0.0), axis=0, keepdims=True) for idx in sel_idx]
        rank_ref[:, rows] = jnp.concatenate(ranks, axis=0).astype(I32)
        carry = carry + jnp.sum(member, axis=1, keepdims=True)

    carry_ref[...] = carry
    cnt_ref[...] = jnp.broadcast_to(carry, cnt_ref.shape).astype(I32)


def _router(h1p, wr_t, br):
    T, half = h1p.shape
    E = wr_t.shape[0]
    wr_bf = wr_t.astype(BF16)
    tm = min(ROUTER_ROWS, T)
    full = lambda a: pl.BlockSpec(a.shape, lambda i: (0,) * a.ndim)
    tok = lambda: pl.BlockSpec((TOP_K, tm), lambda i: (0, i))
    return pl.pallas_call(
        functools.partial(_router_kernel, top_k=TOP_K, part_rows=min(ROUTER_PART_ROWS, tm)),
        grid=(T // tm,),
        in_specs=[pl.BlockSpec((tm, half), lambda i: (i, 0)), full(wr_bf), full(br)],
        out_specs=[tok(), tok(), tok(), pl.BlockSpec((E, LANES), lambda i: (0, 0))],
        out_shape=[
            jax.ShapeDtypeStruct((TOP_K, T), I32),
            jax.ShapeDtypeStruct((TOP_K, T), F32),
            jax.ShapeDtypeStruct((TOP_K, T), I32),
            jax.ShapeDtypeStruct((E, LANES), I32),
        ],
        scratch_shapes=[pltpu.VMEM((E, 1), F32)],
        compiler_params=_params("arbitrary"),
        name="router",
    )(h1p, wr_bf, br)


def _dst_kernel(start_ref, topi_ref, rank_ref, dst_ref, *, n_experts):
    topi = topi_ref[...]
    dst = rank_ref[...]
    for e in range(n_experts):
        dst = dst + jnp.where(topi == e, start_ref[e], 0)
    dst_ref[...] = dst


def _dst_rows(pad_start, topi, rank):
    spec = pl.BlockSpec(topi.shape, lambda i, *_: (0, 0))
    return pl.pallas_call(
        functools.partial(_dst_kernel, n_experts=pad_start.shape[0]),
        grid_spec=pltpu.PrefetchScalarGridSpec(num_scalar_prefetch=1, grid=(1,), in_specs=[spec, spec], out_specs=spec),
        out_shape=jax.ShapeDtypeStruct(topi.shape, I32),
        compiler_params=_params("arbitrary"),
        name="dst",
    )(pad_start, topi, rank)


def _per_block_slots(dst, tb):
    K, T = dst.shape
    return dst.reshape(K, T // tb, tb).transpose(1, 0, 2).reshape(T // tb, 1, K * tb)


def _dispatch_kernel(cnt_ref, start_ref, meta_ref, dst_ref, h1p_ref, xs_ref, zero_ref, sem, zsem, *, top_k, sub):
    i = pl.program_id(0)
    tb = h1p_ref.shape[0]

    def padding_copies(e, act):
        pos = start_ref[e] + cnt_ref[e]
        n_single = (8 - pos % 8) % 8
        for u in range(7):
            @pl.when(u < n_single)
            def _():
                act(pltpu.make_async_copy(zero_ref.at[pl.ds(0, 1)], xs_ref.at[pl.ds(pos + u, 1)], zsem))

        pos8 = pos + n_single
        rem = start_ref[e] + (cnt_ref[e] + sub - 1) // sub * sub - pos8
        size = sub // 2
        while size >= 8:
            at = pl.multiple_of(pos8 + rem // (2 * size) * (2 * size), 8)

            @pl.when(rem % (2 * size) >= size)
            def _():
                act(pltpu.make_async_copy(zero_ref.at[pl.ds(0, size)], xs_ref.at[pl.ds(at, size)], zsem))

            size //= 2

    def tail_copy(sb):
        rows = pl.ds(pl.multiple_of(sb * sub, sub), sub)
        return pltpu.make_async_copy(zero_ref, xs_ref.at[rows], zsem)

    @pl.when(i == 0)
    def _():
        zero_ref[...] = jnp.zeros_like(zero_ref)
        n_experts = cnt_ref.shape[0]
        n_total = xs_ref.shape[0] // sub
        for act in (lambda copy: copy.start(), lambda copy: copy.wait()):
            def per_expert(e, carry, act=act):
                padding_copies(e, act)
                return carry

            def per_tail(sb, carry, act=act):
                act(tail_copy(sb))
                return carry

            lax.fori_loop(0, n_experts, per_expert, 0)
            lax.fori_loop(meta_ref[1], n_total, per_tail, 0)

    def start(c, carry):
        for u in range(DMA_ISSUE_UNROLL):
            r = c * DMA_ISSUE_UNROLL + u
            for k in range(top_k):
                row = dst_ref[0, k * tb + r]
                pltpu.make_async_copy(h1p_ref.at[pl.ds(r, 1)], xs_ref.at[pl.ds(row, 1)], sem).start(priority=k % 2)
        return carry

    lax.fori_loop(0, tb // DMA_ISSUE_UNROLL, start, 0)
    for k in range(top_k):
        pltpu.make_async_copy(h1p_ref, xs_ref.at[pl.ds(0, tb)], sem).wait()


def _dispatch(counts, pad_start, meta, dst, h1p, n_rows):
    T, W = h1p.shape
    tb = min(DISPATCH_ROWS, T)
    slots = _per_block_slots(dst, tb)
    return pl.pallas_call(
        functools.partial(_dispatch_kernel, top_k=dst.shape[0], sub=EXPERT_SUB_ROWS),
        grid_spec=pltpu.PrefetchScalarGridSpec(
            num_scalar_prefetch=3,
            grid=(T // tb,),
            in_specs=[
                pl.BlockSpec((None, 1, slots.shape[2]), lambda i, *_: (i, 0, 0), memory_space=pltpu.SMEM),
                pl.BlockSpec((tb, W), lambda i, *_: (i, 0)),
            ],
            out_specs=pl.BlockSpec(memory_space=pl.ANY),
            scratch_shapes=[pltpu.VMEM((EXPERT_SUB_ROWS, W), U32), pltpu.SemaphoreType.DMA(()),
                            pltpu.SemaphoreType.DMA(())],
        ),
        out_shape=jax.ShapeDtypeStruct((n_rows, W), U32),
        compiler_params=_params("arbitrary"),
        name="dispatch",
    )(counts, pad_start, meta, slots, h1p)


def _experts_kernel(ge_ref, row0_ref, nsub_ref, meta_ref, bu_ref, bd_ref, wup_ref, wdn_ref, xs_ref, ys_ref,
                    xbuf_ref, acc_ref, ybuf_ref, wgf_ref, wlf_ref, wdf_ref, wgb_ref, wlb_ref, wdb_ref, stage_ref,
                    wsem, xsem, ysem, *, n_f):
    s = pl.program_id(0)
    n_groups = meta_ref[0]
    sub, half = stage_ref.shape[1:]
    tf = wdf_ref.shape[0]
    ff = n_f * tf

    def y_copy(ib, row):
        return pltpu.make_async_copy(ybuf_ref.at[ib], ys_ref.at[pl.ds(pl.multiple_of(row, sub), sub)], ysem)

    def weight_copies(e, j):
        glu_cols = pl.ds(pl.multiple_of(j * tf, tf), tf)
        lin_cols = pl.ds(pl.multiple_of(ff + j * tf, tf), tf)
        return (pltpu.make_async_copy(wup_ref.at[e, :, glu_cols], wgf_ref, wsem.at[0]),
                pltpu.make_async_copy(wup_ref.at[e, :, lin_cols], wlf_ref, wsem.at[1]),
                pltpu.make_async_copy(wdn_ref.at[e, glu_cols, :], wdf_ref, wsem.at[2]))

    def cast_weights(slot):
        wgb_ref[slot] = wgf_ref[...].astype(BF16)
        wlb_ref[slot] = wlf_ref[...].astype(BF16)
        wdb_ref[slot] = wdf_ref[...].astype(BF16)

    @pl.when(s < n_groups)
    def _():
        e = ge_ref[s]
        e_after = ge_ref[jnp.minimum(s + 1, n_groups - 1)]
        n_sub = nsub_ref[s]
        row0 = row0_ref[s]

        def x_copy(ib, slot):
            rows = pl.ds(pl.multiple_of(row0 + ib * sub, sub), sub)
            return pltpu.make_async_copy(xs_ref.at[rows], stage_ref.at[slot], xsem.at[slot])

        x_copy(0, 0).start()

        @pl.when(s == 0)
        def _():
            for copy in weight_copies(e, 0):
                copy.start()
            for copy in weight_copies(e, 0):
                copy.wait()
            cast_weights(0)

        def item(j, carry):
            slot = j % 2
            is_last = j == n_f - 1
            next_j = jnp.where(is_last, 0, j + 1)
            next_e = jnp.where(is_last, e_after, e)
            for copy in weight_copies(next_e, next_j):
                copy.start()

            def load_tokens(ib):
                xslot = ib % 2
                x_copy(ib, xslot).wait()

                @pl.when(ib + 1 < n_sub)
                def _():
                    x_copy(ib + 1, 1 - xslot).start()

                lo, hi = _unpack_bf16_pair(stage_ref[xslot])
                xbuf_ref[ib, :, :half] = lo
                xbuf_ref[ib, :, half:] = hi
                acc_ref[ib] = jnp.broadcast_to(bd_ref[...], acc_ref.shape[1:])

            def accumulate(ib, n):
                width = xbuf_ref.shape[2]
                x = xbuf_ref[pl.ds(ib, n)].reshape(n * sub, width)
                glu = jnp.dot(x, wgb_ref[slot], preferred_element_type=F32) + bu_ref[pl.ds(j, 1), :]
                lin = jnp.dot(x, wlb_ref[slot], preferred_element_type=F32) + bu_ref[pl.ds(n_f + j, 1), :]
                glu = jnp.minimum(glu, SWIGLU_LIMIT)
                lin = jnp.clip(lin, -SWIGLU_LIMIT, SWIGLU_LIMIT)
                act = glu / (1.0 + jnp.exp(-SWIGLU_ALPHA * glu)) * (lin + 1.0)
                part = jnp.dot(act.astype(BF16), wdb_ref[slot], preferred_element_type=F32)
                acc_ref[pl.ds(ib, n)] += part.reshape(n, sub, width)

            def process(ib0, count, with_next_weights):
                if count <= 2:
                    chains = [(ib, 1) for ib in range(count)]
                else:
                    chains = [(ib, 2) for ib in range(0, count - 1, 2)] + [(count - 1, 1)] * (count % 2)

                @pl.when(j == 0)
                def _():
                    for ib in range(count):
                        load_tokens(ib0 + ib)

                if with_next_weights:
                    for copy in weight_copies(next_e, next_j):
                        copy.wait()
                    cast_weights(1 - slot)
                for first, n in chains:
                    accumulate(ib0 + first, n)

                @pl.when(is_last)
                def _():
                    for ib in range(count):
                        y = acc_ref[ib0 + ib]
                        ybuf_ref[ib0 + ib] = _pack_bf16_pair(y[:, :half], y[:, half:])
                        y_copy(ib0 + ib, row0 + (ib0 + ib) * sub).start()

            n_units = (n_sub - 1) // EXPERT_UNIT_SUBS

            def unit(p, carry):
                process(EXPERT_UNIT_SUBS * p, EXPERT_UNIT_SUBS, False)
                return carry

            lax.fori_loop(0, n_units, unit, 0)
            for rest in range(1, EXPERT_UNIT_SUBS + 1):
                @pl.when(n_sub - EXPERT_UNIT_SUBS * n_units == rest)
                def _(rest=rest):
                    process(EXPERT_UNIT_SUBS * n_units, rest, True)

            return carry

        lax.fori_loop(0, n_f, item, 0)

        def drain(ib, carry):
            y_copy(ib, row0 + ib * sub).wait()
            return carry

        lax.fori_loop(0, n_sub, drain, 0)

    @pl.when(s == pl.num_programs(0) - 1)
    def _():
        n_total = ys_ref.shape[0] // sub
        ybuf_ref[0] = jnp.zeros(ybuf_ref.shape[1:], U32)

        def fill(ib, carry):
            y_copy(0, ib * sub).start()
            return carry

        def drain(ib, carry):
            y_copy(0, ib * sub).wait()
            return carry

        lax.fori_loop(meta_ref[1], n_total, fill, 0)
        lax.fori_loop(meta_ref[1], n_total, drain, 0)


def _experts(group_expert, group_row0, group_nsub, meta, xs, w_up, b_up, w_down, b_down):
    R, half = xs.shape
    E, D, ff2 = w_up.shape
    ff = ff2 // 2
    sub = EXPERT_SUB_ROWS
    gb = EXPERT_GROUP_SUBS
    tf = min(EXPERT_FF_COLS, ff)
    n_f = ff // tf
    assert n_f % 2 == 0
    grp = lambda s, ge, r0, ns, meta: (ge[jnp.minimum(s, meta[0] - 1)], 0, 0)
    return pl.pallas_call(
        functools.partial(_experts_kernel, n_f=n_f),
        grid_spec=pltpu.PrefetchScalarGridSpec(
            num_scalar_prefetch=4,
            grid=(group_expert.shape[0],),
            in_specs=[
                pl.BlockSpec((None, 2 * n_f, tf), grp),
                pl.BlockSpec((None, 1, D), grp),
                pl.BlockSpec(memory_space=pl.ANY),
                pl.BlockSpec(memory_space=pl.ANY),
                pl.BlockSpec(memory_space=pl.ANY),
            ],
            out_specs=pl.BlockSpec(memory_space=pl.ANY),
            scratch_shapes=[
                pltpu.VMEM((gb, sub, D), BF16),
                pltpu.VMEM((gb, sub, D), F32),
                pltpu.VMEM((gb, sub, half), U32),
                pltpu.VMEM((D, tf), F32),
                pltpu.VMEM((D, tf), F32),
                pltpu.VMEM((tf, D), F32),
                pltpu.VMEM((2, D, tf), BF16),
                pltpu.VMEM((2, D, tf), BF16),
                pltpu.VMEM((2, tf, D), BF16),
                pltpu.VMEM((2, sub, half), U32),
                pltpu.SemaphoreType.DMA((3,)),
                pltpu.SemaphoreType.DMA((2,)),
                pltpu.SemaphoreType.DMA(()),
            ],
        ),
        out_shape=jax.ShapeDtypeStruct((R, half), U32),
        compiler_params=pltpu.CompilerParams(dimension_semantics=("arbitrary",),
                                             vmem_limit_bytes=EXPERT_VMEM_LIMIT_BYTES),
        name="experts",
    )(group_expert, group_row0, group_nsub, meta, b_up.reshape(E, 2 * n_f, tf), b_down.reshape(E, 1, D),
      w_up, w_down, xs)


def _combine_kernel(dst_ref, nxt_ref, gate_ref, h1_ref, lng_ref, lnb_ref, ys_ref, out_ref, buf0_ref, buf1_ref,
                    sem, *, top_k):
    i = pl.program_id(0)
    last = pl.num_programs(0) - 1
    tb = h1_ref.shape[0]
    half = buf0_ref.shape[2]

    def row_copy(slots_ref, buf_ref, slot, r, k):
        row = slots_ref[0, k * tb + r]
        return pltpu.make_async_copy(ys_ref.at[pl.ds(row, 1)], buf_ref.at[k, pl.ds(r, 1)], sem.at[slot])

    def wait_block(buf_ref, slot):
        for k in range(top_k):
            pltpu.make_async_copy(ys_ref.at[pl.ds(0, tb)], buf_ref.at[k], sem.at[slot]).wait()

    @pl.when(i == 0)
    def _():
        def start(c, carry):
            for u in range(DMA_ISSUE_UNROLL):
                for k in range(top_k):
                    row_copy(dst_ref, buf0_ref, 0, c * DMA_ISSUE_UNROLL + u, k).start(priority=k % 2)
            return carry

        lax.fori_loop(0, tb // DMA_ISSUE_UNROLL, start, 0)

    def step(cur_ref, cur, nxt_buf_ref, nxt):
        wait_block(cur_ref, cur)
        for r in range(tb):
            for k in range(top_k):
                row_copy(nxt_ref, nxt_buf_ref, nxt, r, k).start(priority=k % 2)

        diag = lax.broadcasted_iota(I32, (tb, tb), 0) == lax.broadcasted_iota(I32, (tb, tb), 1)
        ffn_lo = jnp.zeros((tb, half), F32)
        ffn_hi = jnp.zeros((tb, half), F32)
        for k in range(top_k):
            gate = jnp.sum(jnp.where(diag, gate_ref[k:k + 1, :], 0.0), axis=1, keepdims=True)
            lo, hi = _unpack_bf16_pair(cur_ref[k])
            ffn_lo = ffn_lo + lo.astype(F32) * gate
            ffn_hi = ffn_hi + hi.astype(F32) * gate
        ffn = jnp.concatenate([ffn_lo, ffn_hi], axis=1)
        out_ref[...] = _layer_norm(DEEPNORM_ALPHA * h1_ref[...] + ffn, lng_ref[...], lnb_ref[...])

        @pl.when(i == last)
        def _():
            wait_block(nxt_buf_ref, nxt)

    @pl.when(i % 2 == 0)
    def _():
        step(buf0_ref, 0, buf1_ref, 1)

    @pl.when(i % 2 == 1)
    def _():
        step(buf1_ref, 1, buf0_ref, 0)


def _combine(dst, gates, h1, ln_g, ln_b, ys):
    T, D = h1.shape
    K = dst.shape[0]
    tb = min(COMBINE_ROWS, T)
    slots = _per_block_slots(dst, tb)
    n_blocks = T // tb
    return pl.pallas_call(
        functools.partial(_combine_kernel, top_k=K),
        grid=(n_blocks,),
        in_specs=[
            pl.BlockSpec((None, 1, K * tb), lambda i: (i, 0, 0), memory_space=pltpu.SMEM),
            pl.BlockSpec((None, 1, K * tb), lambda i: (jnp.minimum(i + 1, n_blocks - 1), 0, 0),
                         memory_space=pltpu.SMEM),
            pl.BlockSpec((K, tb), lambda i: (0, i)),
            pl.BlockSpec((tb, D), lambda i: (i, 0)),
            pl.BlockSpec((1, D), lambda i: (0, 0)),
            pl.BlockSpec((1, D), lambda i: (0, 0)),
            pl.BlockSpec(memory_space=pl.ANY),
        ],
        out_specs=pl.BlockSpec((tb, D), lambda i: (i, 0)),
        out_shape=jax.ShapeDtypeStruct((T, D), F32),
        scratch_shapes=[pltpu.VMEM((K, tb, D // 2), U32), pltpu.VMEM((K, tb, D // 2), U32),
                        pltpu.SemaphoreType.DMA((2,))],
        compiler_params=_params("arbitrary"),
        name="combine",
    )(slots, slots, gates, h1, ln_g, ln_b, ys)


def _expert_schedule(counts, n_tokens_routed):
    sub, gb = EXPERT_SUB_ROWS, EXPERT_GROUP_SUBS
    n_experts = counts.shape[0]
    max_subs = n_tokens_routed // sub + n_experts
    max_groups = n_experts + max_subs // gb
    n_sub = (counts + sub - 1) // sub
    sub_end = jnp.cumsum(n_sub)
    sub_start = sub_end - n_sub
    n_grp = (n_sub + gb - 1) // gb
    grp_end = jnp.cumsum(n_grp)
    grp_start = grp_end - n_grp
    s = jnp.arange(max_groups, dtype=I32)
    expert = jnp.minimum(jnp.sum(grp_end[None, :] <= s[:, None], axis=1), n_experts - 1).astype(I32)
    onehot = (expert[:, None] == jnp.arange(n_experts, dtype=I32)[None, :]).astype(I32)
    pick = lambda v: jnp.sum(onehot * v[None, :], axis=1)
    local = s - pick(grp_start)
    row0 = (pick(sub_start) + local * gb) * sub
    nsub = jnp.clip(pick(n_sub) - local * gb, 0, gb)
    meta = jnp.stack([grp_end[-1], sub_end[-1]])
    to_i32 = lambda a: a.astype(I32)
    return to_i32(sub_start * sub), to_i32(expert), to_i32(row0), to_i32(nsub), to_i32(meta), max_subs * sub


def kernel(x, ln0_g, ln0_b, w_in, ret_decay_fwd, ret_decay_bwd, ret_gn_g, w_out, ln1_g, ln1_b, w_router, b_router,
           w_up, b_up, w_down, b_down, ln2_g, ln2_b):
    B, S, D = x.shape
    assert B == 1 and w_in.shape[0] == DEPTH
    T = S
    fw = D // 2
    rw = D - fw
    gw = fw // N_FOURIER_GROUPS
    head_dim = rw // N_RET_HEADS
    E = w_router.shape[-1]
    row = lambda a: a.reshape(1, -1)

    h, uf, ur = _in_proj(x.reshape(T, D), row(ln0_g), row(ln0_b), w_in[0].astype(BF16), head_dim)
    yf = _fourier_mix(uf, gw)
    lg_fwd = -jnp.exp(ret_decay_fwd[0].astype(F32))
    lg_bwd = -jnp.exp(ret_decay_bwd[0].astype(F32))
    o_f, o_b = _retention(ur, lg_fwd, lg_bwd, N_RET_HEADS, head_dim)
    h1, h1p = _out_proj(yf, o_f, o_b, ur, row(ret_gn_g[0]), h, w_out[0].astype(BF16), row(ln1_g[0]), row(ln1_b[0]),
                        N_RET_HEADS)
    topi, gates, rank, cnt = _router(h1p, w_router[0].T, b_router[0].reshape(E, 1))

    counts = cnt[:, 0]
    pad_start, group_expert, group_row0, group_nsub, meta, n_rows = _expert_schedule(counts, T * TOP_K)
    dst = _dst_rows(pad_start, topi, rank)
    xs = _dispatch(counts, pad_start, meta, dst, h1p, n_rows)
    ys = _experts(group_expert, group_row0, group_nsub, meta, xs, w_up.reshape(w_up.shape[1:]), b_up[0],
                  w_down.reshape(w_down.shape[1:]), b_down[0])
    out = _combine(dst, gates, h1, row(ln2_g[0]), row(ln2_b[0]), ys)
    return out.reshape(B, S, D)
```
